```python
import math
import jax, jax.numpy as jnp
from jax import lax
import numpy as np

D_MODEL = 1024
BATCH = 8
SEQ = 2048
DEPTH = 1
DEC_BATCH = 128
DEC_SEQ = 4
PAST_LEN = 16384
PAGE_SIZE = 128

POOL_WIDTH = D_MODEL // 2
POOL_WINDOWS = (2, 4, 8, 16)
POOL_GROUPS = len(POOL_WINDOWS)
POOL_GROUP_DIM = POOL_WIDTH // POOL_GROUPS
POOL_HIST = max(POOL_WINDOWS) - 1
SSM_WIDTH = D_MODEL // 2
SSM_GROUP_DIM = 16
SSM_GROUPS = SSM_WIDTH // SSM_GROUP_DIM
SSM_STATE = 64
DT_MIN = 0.001
DT_MAX = 0.1
N_BRANCHES = 2
IN_WIDTH = POOL_WIDTH + SSM_WIDTH + N_BRANCHES * D_MODEL
D_FF = -(-8 * D_MODEL // (3 * 256)) * 256
RMS_EPS = 1e-6

kernel_name = 'hybrid_pool_s5_gated_decoder_step'


def _rmsnorm(x, g):
    xf = x.astype(jnp.float32)
    y = xf * lax.rsqrt(jnp.mean(xf * xf, axis=-1, keepdims=True) + RMS_EPS)
    return (y * g.astype(jnp.float32)).astype(x.dtype)


def _pool_mixer(u_hist, u, start_pos, w_grp, scale):
    b, t, c = u.shape
    ext = jnp.concatenate([u_hist.astype(jnp.float32), u.astype(jnp.float32)], axis=1)
    cs = jnp.concatenate([jnp.zeros((b, 1, c), jnp.float32), jnp.cumsum(ext, axis=1)], axis=1)
    pos = start_pos + jnp.arange(t)
    base = POOL_HIST + 1
    pooled = []
    for g, w in enumerate(POOL_WINDOWS):
        lo_c, hi_c = g * POOL_GROUP_DIM, (g + 1) * POOL_GROUP_DIM
        win_sum = cs[:, base:base + t, lo_c:hi_c] - cs[:, base - w:base - w + t, lo_c:hi_c]
        cnt = jnp.minimum(w, pos + 1).astype(jnp.float32)[None, :, None]
        pooled.append(win_sum / cnt)
    pooled = jnp.stack(pooled, axis=2)
    diff = pooled - u.astype(jnp.float32).reshape(b, t, POOL_GROUPS, POOL_GROUP_DIM)
    mixed = jnp.einsum('btgc,gcd->btgd', diff, w_grp.astype(jnp.float32))
    return (mixed.reshape(b, t, c) * scale.astype(jnp.float32)).astype(u.dtype)


def _s5_discretise(a_re, a_im, log_dt):
    a_re = a_re.astype(jnp.float32)
    a_im = a_im.astype(jnp.float32)
    dt = jnp.exp(log_dt.astype(jnp.float32))[:, None]
    mag = jnp.exp(a_re * dt)
    ab_re = mag * jnp.cos(a_im * dt)
    ab_im = mag * jnp.sin(a_im * dt)
    num_re = ab_re - 1.0
    num_im = ab_im
    den = a_re * a_re + a_im * a_im
    coef_re = (num_re * a_re + num_im * a_im) / den
    coef_im = (num_im * a_re - num_re * a_im) / den
    return ab_re, ab_im, coef_re, coef_im


def _scan_combine(e1, e2):
    a1r, a1i, b1r, b1i = e1
    a2r, a2i, b2r, b2i = e2
    ar = a2r * a1r - a2i * a1i
    ai = a2r * a1i + a2i * a1r
    br = a2r * b1r - a2i * b1i + b2r
    bi = a2r * b1i + a2i * b1r + b2i
    return ar, ai, br, bi


def _s5_mixer(u, h_re, h_im, a_re, a_im, log_dt, b_re, b_im, c_re, c_im, d_skip):
    b, t, _ = u.shape
    uf = u.astype(jnp.float32)
    ug = uf.reshape(b, t, SSM_GROUPS, SSM_GROUP_DIM)
    ab_re, ab_im, coef_re, coef_im = _s5_discretise(a_re, a_im, log_dt)
    bu_re = jnp.einsum('btgh,gph->btgp', ug, b_re.astype(jnp.float32))
    bu_im = jnp.einsum('btgh,gph->btgp', ug, b_im.astype(jnp.float32))
    x_re = coef_re * bu_re - coef_im * bu_im
    x_im = coef_re * bu_im + coef_im * bu_re
    h_re = h_re.astype(jnp.float32)
    h_im = h_im.astype(jnp.float32)
    x_re = x_re.at[:, 0].add(ab_re * h_re - ab_im * h_im)
    x_im = x_im.at[:, 0].add(ab_re * h_im + ab_im * h_re)
    a_re_b = jnp.broadcast_to(ab_re, x_re.shape)
    a_im_b = jnp.broadcast_to(ab_im, x_im.shape)
    _, _, s_re, s_im = lax.associative_scan(_scan_combine, (a_re_b, a_im_b, x_re, x_im), axis=1)
    y = (jnp.einsum('btgp,ghp->btgh', s_re, c_re.astype(jnp.float32))
         - jnp.einsum('btgp,ghp->btgh', s_im, c_im.astype(jnp.float32)))
    y = y.reshape(b, t, SSM_WIDTH) + d_skip.astype(jnp.float32) * uf
    return y, s_re[:, -1], s_im[:, -1]


def _trunk(x, hist, st_re, st_im, start_pos, norm_mix, w_in, pool_w, pool_scale,
           ssm_a_re, ssm_a_im, ssm_log_dt, ssm_b_re, ssm_b_im, ssm_c_re, ssm_c_im, ssm_d,
           glu_w, glu_b, w_branch_pool, w_branch_ssm, w_out, norm_ffn,
           ffn_w_gate, ffn_w_up, ffn_w_down, norm_final):
    h = x
    b, t, _ = x.shape
    new_hist, new_re, new_im = [], [], []
    for l in range(DEPTH):
        xn = _rmsnorm(h, norm_mix[l])
        proj = jnp.einsum('btd,de->bte', xn, w_in[l])
        u_pool = proj[..., :POOL_WIDTH]
        u_ssm = proj[..., POOL_WIDTH:POOL_WIDTH + SSM_WIDTH]
        gates = jax.nn.sigmoid(proj[..., POOL_WIDTH + SSM_WIDTH:].astype(jnp.float32)
                               ).reshape(b, t, N_BRANCHES, D_MODEL)
        a_out = _pool_mixer(hist[l], u_pool, start_pos, pool_w[l], pool_scale[l])
        y_ssm, s_re_last, s_im_last = _s5_mixer(u_ssm, st_re[l], st_im[l], ssm_a_re[l], ssm_a_im[l],
                                                ssm_log_dt[l], ssm_b_re[l], ssm_b_im[l],
                                                ssm_c_re[l], ssm_c_im[l], ssm_d[l])
        z = jax.nn.gelu(y_ssm)
        b_out = (z * jax.nn.sigmoid(jnp.einsum('btc,ce->bte', z, glu_w[l].astype(jnp.float32))
                                    + glu_b[l].astype(jnp.float32))).astype(x.dtype)
        merged = (gates[:, :, 0] * jnp.einsum('btc,cd->btd', a_out, w_branch_pool[l]).astype(jnp.float32)
                  + gates[:, :, 1] * jnp.einsum('btc,cd->btd', b_out, w_branch_ssm[l]).astype(jnp.float32))
        h = h + jnp.einsum('btd,de->bte', merged.astype(x.dtype), w_out[l]).astype(h.dtype)
        hn = _rmsnorm(h, norm_ffn[l])
        f = jax.nn.silu(jnp.einsum('btd,df->btf', hn, ffn_w_gate[l])) * jnp.einsum('btd,df->btf', hn, ffn_w_up[l])
        h = h + jnp.einsum('btf,fd->btd', f, ffn_w_down[l]).astype(h.dtype)
        ext = jnp.concatenate([hist[l].astype(u_pool.dtype), u_pool], axis=1)
        new_hist.append(ext[:, -POOL_HIST:])
        new_re.append(s_re_last)
        new_im.append(s_im_last)
    y = _rmsnorm(h, norm_final)
    return y, jnp.stack(new_hist), jnp.stack(new_re), jnp.stack(new_im)


def setup_inputs(seed: int = 0) -> dict:
    key = jax.random.key(seed)
    ks = jax.random.split(key, 32)
    f32 = jnp.float32
    nrm = lambda k, shape, s: jax.random.normal(k, shape, f32) * s
    L = DEPTH
    a_re = -0.5 + nrm(ks[7], (L, SSM_GROUPS, SSM_STATE), 0.01)
    a_im = math.pi * jnp.arange(SSM_STATE, dtype=f32)[None, None, :] + nrm(ks[8], (L, SSM_GROUPS, SSM_STATE), 0.01)
    log_dt = jax.random.uniform(ks[9], (L, SSM_GROUPS), f32, math.log(DT_MIN), math.log(DT_MAX))
    return {
        'x_prompt': nrm(ks[0], (BATCH, SEQ, D_MODEL), 1.0),
        'x_sample': nrm(ks[1], (DEC_BATCH, DEC_SEQ, D_MODEL), 1.0),
        'state_pool': nrm(ks[2], (L, DEC_BATCH, POOL_HIST, POOL_WIDTH), 1.0),
        'state_ssm_re': nrm(ks[3], (L, DEC_BATCH, SSM_GROUPS, SSM_STATE), 0.5),
        'state_ssm_im': nrm(ks[4], (L, DEC_BATCH, SSM_GROUPS, SSM_STATE), 0.5),
        'norm_mix': 1.0 + nrm(ks[5], (L, D_MODEL), 0.02),
        'w_in': nrm(ks[6], (L, D_MODEL, IN_WIDTH), D_MODEL ** -0.5),
        'pool_w': nrm(ks[10], (L, POOL_GROUPS, POOL_GROUP_DIM, POOL_GROUP_DIM), POOL_GROUP_DIM ** -0.5),
        'pool_scale': 1.0 + nrm(ks[11], (L, POOL_WIDTH), 0.1),
        'ssm_a_re': a_re,
        'ssm_a_im': a_im,
        'ssm_log_dt': log_dt,
        'ssm_b_re': nrm(ks[12], (L, SSM_GROUPS, SSM_STATE, SSM_GROUP_DIM), (2 * SSM_GROUP_DIM) ** -0.5),
        'ssm_b_im': nrm(ks[13], (L, SSM_GROUPS, SSM_STATE, SSM_GROUP_DIM), (2 * SSM_GROUP_DIM) ** -0.5),
        'ssm_c_re': nrm(ks[14], (L, SSM_GROUPS, SSM_GROUP_DIM, SSM_STATE), (2 * SSM_STATE) ** -0.5),
        'ssm_c_im': nrm(ks[15], (L, SSM_GROUPS, SSM_GROUP_DIM, SSM_STATE), (2 * SSM_STATE) ** -0.5),
        'ssm_d': nrm(ks[16], (L, SSM_WIDTH), 1.0),
        'glu_w': nrm(ks[17], (L, SSM_WIDTH, SSM_WIDTH), SSM_WIDTH ** -0.5),
        'glu_b': nrm(ks[18], (L, SSM_WIDTH), 0.01),
        'w_branch_pool': nrm(ks[19], (L, POOL_WIDTH, D_MODEL), POOL_WIDTH ** -0.5),
        'w_branch_ssm': nrm(ks[20], (L, SSM_WIDTH, D_MODEL), SSM_WIDTH ** -0.5),
        'w_out': nrm(ks[21], (L, D_MODEL, D_MODEL), D_MODEL ** -0.5),
        'norm_ffn': 1.0 + nrm(ks[22], (L, D_MODEL), 0.02),
        'ffn_w_gate': nrm(ks[23], (L, D_MODEL, D_FF), D_MODEL ** -0.5),
        'ffn_w_up': nrm(ks[24], (L, D_MODEL, D_FF), D_MODEL ** -0.5),
        'ffn_w_down': nrm(ks[25], (L, D_FF, D_MODEL), D_FF ** -0.5),
        'norm_final': 1.0 + nrm(ks[26], (D_MODEL,), 0.02),
    }


def reference(x_prompt, x_sample, state_pool, state_ssm_re, state_ssm_im,
              norm_mix, w_in, pool_w, pool_scale, ssm_a_re, ssm_a_im, ssm_log_dt,
              ssm_b_re, ssm_b_im, ssm_c_re, ssm_c_im, ssm_d, glu_w, glu_b,
              w_branch_pool, w_branch_ssm, w_out, norm_ffn, ffn_w_gate, ffn_w_up,
              ffn_w_down, norm_final):
    bp = x_prompt.shape[0]
    zero_hist = jnp.zeros((DEPTH, bp, POOL_HIST, POOL_WIDTH), x_prompt.dtype)
    zero_ssm = jnp.zeros((DEPTH, bp, SSM_GROUPS, SSM_STATE), jnp.float32)
    y_prompt, pool_p, re_p, im_p = _trunk(
        x_prompt, zero_hist, zero_ssm, zero_ssm, 0,
        norm_mix, w_in, pool_w, pool_scale, ssm_a_re, ssm_a_im, ssm_log_dt,
        ssm_b_re, ssm_b_im, ssm_c_re, ssm_c_im, ssm_d, glu_w, glu_b,
        w_branch_pool, w_branch_ssm, w_out, norm_ffn, ffn_w_gate, ffn_w_up,
        ffn_w_down, norm_final)
    y_sample, pool_s, re_s, im_s = _trunk(
        x_sample, state_pool, state_ssm_re, state_ssm_im, PAST_LEN,
        norm_mix, w_in, pool_w, pool_scale, ssm_a_re, ssm_a_im, ssm_log_dt,
        ssm_b_re, ssm_b_im, ssm_c_re, ssm_c_im, ssm_d, glu_w, glu_b,
        w_branch_pool, w_branch_ssm, w_out, norm_ffn, ffn_w_gate, ffn_w_up,
        ffn_w_down, norm_final)
    return (y_prompt, y_sample, pool_p, re_p, im_p, pool_s, re_s, im_s)
```

```python
import functools
import math

import jax
import jax.numpy as jnp
from jax import lax
from jax.experimental import pallas as pl
from jax.experimental.pallas import tpu as pltpu

D_MODEL = 1024
PAST_LEN = 16384
POOL_WIDTH = 512
POOL_WINDOWS = (2, 4, 8, 16)
POOL_GROUP_DIM = 128
POOL_HIST = 15
HIST_STEPS = POOL_HIST + 1
SSM_WIDTH = 512
SSM_GROUPS = 32
SSM_GROUP_DIM = 16
SSM_STATE = 64
STATE_LANES = 2 * SSM_GROUPS * SSM_STATE
IN_WIDTH = POOL_WIDTH + SSM_WIDTH + 2 * D_MODEL
D_FF = 2816
RMS_EPS = 1e-6

SUBLANES = 8
LANES = 128
MXU_DIM = 256
CHUNK_LANES = 2 * LANES
N_CHUNKS = STATE_LANES // CHUNK_LANES
HALF_CHUNKS = N_CHUNKS // 2
HALF_LANES = HALF_CHUNKS * CHUNK_LANES
SCAN_GROUP = 4
ROW_TILE = 512
VMEM_LIMIT_BYTES = 56 * 1024 * 1024

_F32 = jnp.float32
_BF16 = jnp.bfloat16


def _rmsnorm(x, g):
    return x * lax.rsqrt(jnp.mean(x * x, axis=-1, keepdims=True) + RMS_EPS) * g


def _dot(a, b):
    return jnp.dot(a, b, preferred_element_type=_F32)


def _discretise_kernel(a_re_ref, a_im_ref, log_dt_ref, b_re_ref, b_im_ref,
                       ab_re_ref, ab_im_ref, bp_re_ref, bp_im_ref):
    a_re = a_re_ref[...]
    a_im = a_im_ref[...]
    dt = jnp.exp(log_dt_ref[...])
    mag = jnp.exp(a_re * dt)
    ab_re = mag * jnp.cos(a_im * dt)
    ab_im = mag * jnp.sin(a_im * dt)
    num_re = ab_re - 1.0
    num_im = ab_im
    den = a_re * a_re + a_im * a_im
    coef_re = (num_re * a_re + num_im * a_im) / den
    coef_im = (num_im * a_re - num_re * a_im) / den
    ab_re_ref[...] = ab_re
    ab_im_ref[...] = ab_im
    b_re = b_re_ref[...]
    b_im = b_im_ref[...]
    bp_re_ref[...] = coef_re * b_re - coef_im * b_im
    bp_im_ref[...] = coef_re * b_im + coef_im * b_re


def _discretise(a_re, a_im, log_dt, b_re, b_im):
    g, p, h = SSM_GROUPS, SSM_STATE, SSM_GROUP_DIM
    out = pl.pallas_call(
        _discretise_kernel,
        out_shape=(jax.ShapeDtypeStruct((g, 1, p), _F32), jax.ShapeDtypeStruct((g, 1, p), _F32),
                   jax.ShapeDtypeStruct((g, h, p), _F32), jax.ShapeDtypeStruct((g, h, p), _F32)),
        name="s5_discretise",
    )(a_re.reshape(g, 1, p), a_im.reshape(g, 1, p), log_dt.reshape(g, 1, 1),
      b_re.transpose(0, 2, 1), b_im.transpose(0, 2, 1))
    ab_re, ab_im, bp_re, bp_im = out
    return ab_re.reshape(g, p), ab_im.reshape(g, p), bp_re, bp_im


def _scan_half(xs_ref, s_ref, ab_ref, half, n_batch, n_steps):
    for cg in range(0, HALF_CHUNKS, SCAN_GROUP):
        chunks = [half * HALF_CHUNKS + cg + k for k in range(SCAN_GROUP)]
        ar = [jnp.broadcast_to(ab_ref[c:c + 1, :], (SUBLANES, LANES)) for c in chunks]
        ai = [jnp.broadcast_to(ab_ref[N_CHUNKS + c:N_CHUNKS + c + 1, :], (SUBLANES, LANES)) for c in chunks]

        def row_block(rb, _, cg=cg, chunks=chunks, ar=ar, ai=ai):
            r0 = pl.multiple_of(rb * SUBLANES, SUBLANES)
            sr = tuple(s_ref[pl.ds(r0, SUBLANES), c * CHUNK_LANES:c * CHUNK_LANES + LANES] for c in chunks)
            si = tuple(s_ref[pl.ds(r0, SUBLANES), c * CHUNK_LANES + LANES:(c + 1) * CHUNK_LANES] for c in chunks)

            def step(t, carry):
                sr, si = carry
                row = pl.multiple_of(t * n_batch + r0, SUBLANES)
                new_r, new_i = [], []
                for k in range(SCAN_GROUP):
                    lo = (cg + k) * CHUNK_LANES
                    xr = xs_ref[pl.ds(row, SUBLANES), lo:lo + LANES]
                    xi = xs_ref[pl.ds(row, SUBLANES), lo + LANES:lo + CHUNK_LANES]
                    nr = ar[k] * sr[k] - ai[k] * si[k] + xr
                    ni = ar[k] * si[k] + ai[k] * sr[k] + xi
                    xs_ref[pl.ds(row, SUBLANES), lo:lo + LANES] = nr
                    xs_ref[pl.ds(row, SUBLANES), lo + LANES:lo + CHUNK_LANES] = ni
                    new_r.append(nr)
                    new_i.append(ni)
                return tuple(new_r), tuple(new_i)

            sr, si = lax.fori_loop(0, n_steps, step, (sr, si), unroll=min(n_steps, 4))
            for k, c in enumerate(chunks):
                s_ref[pl.ds(r0, SUBLANES), c * CHUNK_LANES:c * CHUNK_LANES + LANES] = sr[k]
                s_ref[pl.ds(r0, SUBLANES), c * CHUNK_LANES + LANES:(c + 1) * CHUNK_LANES] = si[k]
            return 0

        if n_batch == SUBLANES:
            row_block(0, 0)
        else:
            lax.fori_loop(0, n_batch // SUBLANES, row_block, 0)


def _mixer_kernel(x_ref, hist0_ref, s0_ref, norm_ref, w_in_ref, pool_w_ref, pool_scale_ref,
                  ab_ref, b_ref, c_ref, d_ref, glu_w_ref, glu_b_ref, w_bp_ref, w_bs_ref, w_out_ref,
                  h_ref, hist_ref, s_ref,
                  proj_ref, ext_ref, xs_ref, *, n_batch, n_steps, start_pos):
    i = pl.program_id(0)
    rows = n_batch * n_steps
    hist_rows = HIST_STEPS * n_batch

    @pl.when(i == 0)
    def _():
        hist_ref[...] = hist0_ref[...]
        s_ref[...] = s0_ref[...]

    xn = _rmsnorm(x_ref[...], norm_ref[...]).astype(_BF16)
    proj_ref[...] = _dot(xn, w_in_ref[...])

    ext_ref[0:hist_rows, :] = hist_ref[...]
    ext_ref[hist_rows:, :] = proj_ref[:, 0:POOL_WIDTH]
    hist_ref[...] = ext_ref[rows:rows + hist_rows, :]
    diffs = []
    for g, w in enumerate(POOL_WINDOWS):
        lo = g * POOL_GROUP_DIM
        u_g = ext_ref[hist_rows:, lo:lo + POOL_GROUP_DIM]
        acc = u_g
        for j in range(1, w):
            r0 = (HIST_STEPS - j) * n_batch
            acc = acc + ext_ref[r0:r0 + rows, lo:lo + POOL_GROUP_DIM]
        if start_pos + 1 >= w:
            pooled = acc * (1.0 / w)
        else:
            row = lax.broadcasted_iota(jnp.int32, (rows, POOL_GROUP_DIM), 0)
            step = lax.shift_right_logical(row, int(math.log2(n_batch)))
            pos = start_pos + i * n_steps + step
            pooled = acc / jnp.minimum(w, pos + 1).astype(_F32)
        diffs.append(pooled - u_g)
    mixed = jnp.concatenate(
        [_dot(jnp.concatenate(diffs[0:2], axis=1).astype(_BF16), pool_w_ref[0]),
         _dot(jnp.concatenate(diffs[2:4], axis=1).astype(_BF16), pool_w_ref[1])], axis=1)
    a_out = (mixed * pool_scale_ref[...]).astype(_BF16)

    u_ssm = proj_ref[:, POOL_WIDTH:POOL_WIDTH + SSM_WIDTH]
    u_bf = u_ssm.astype(_BF16)
    ys = []
    for half in range(2):
        xs_ref[...] = _dot(u_bf[:, half * MXU_DIM:(half + 1) * MXU_DIM], b_ref[half])
        _scan_half(xs_ref, s_ref, ab_ref, half, n_batch, n_steps)
        ys.append(_dot(xs_ref[...].astype(_BF16), c_ref[half]))
    y_ssm = jnp.concatenate(ys, axis=1) + d_ref[...] * u_ssm
    z = jax.nn.gelu(y_ssm)
    glu = jax.nn.sigmoid(_dot(z.astype(_BF16), glu_w_ref[...]) + glu_b_ref[...])
    b_out = (z * glu).astype(_BF16)

    off = POOL_WIDTH + SSM_WIDTH
    merged = (jax.nn.sigmoid(proj_ref[:, off:off + D_MODEL]) * _dot(a_out, w_bp_ref[...])
              + jax.nn.sigmoid(proj_ref[:, off + D_MODEL:off + 2 * D_MODEL]) * _dot(b_out, w_bs_ref[...]))
    h_ref[...] = x_ref[...] + _dot(merged.astype(_BF16), w_out_ref[...])


def _resident(shape):
    zeros = (0,) * len(shape)
    return pl.BlockSpec(shape, lambda i: zeros, pipeline_mode=pl.Buffered(1))


def _mixer(x_rows, hist0, s0, weights, *, n_batch, n_steps_total, start_pos):
    n_steps = min(n_steps_total, ROW_TILE // n_batch)
    rows = n_steps * n_batch
    n_tiles = n_steps_total // n_steps
    assert n_tiles * n_steps == n_steps_total and n_batch % SUBLANES == 0
    assert n_batch & (n_batch - 1) == 0
    hist_rows = HIST_STEPS * n_batch
    kern = functools.partial(_mixer_kernel, n_batch=n_batch, n_steps=n_steps, start_pos=start_pos)
    row_spec = pl.BlockSpec((rows, D_MODEL), lambda i: (i, 0))
    return pl.pallas_call(
        kern,
        grid=(n_tiles,),
        in_specs=[row_spec, _resident(hist0.shape), _resident(s0.shape)] + [_resident(w.shape) for w in weights],
        out_specs=(row_spec,
                   pl.BlockSpec((hist_rows, POOL_WIDTH), lambda i: (0, 0)),
                   pl.BlockSpec((n_batch, STATE_LANES), lambda i: (0, 0))),
        out_shape=(jax.ShapeDtypeStruct(x_rows.shape, _F32),
                   jax.ShapeDtypeStruct((hist_rows, POOL_WIDTH), _F32),
                   jax.ShapeDtypeStruct((n_batch, STATE_LANES), _F32)),
        scratch_shapes=[pltpu.VMEM((rows, IN_WIDTH), _F32),
                        pltpu.VMEM((hist_rows + rows, POOL_WIDTH), _F32),
                        pltpu.VMEM((rows, HALF_LANES), _F32)],
        compiler_params=pltpu.CompilerParams(dimension_semantics=("arbitrary",),
                                             vmem_limit_bytes=VMEM_LIMIT_BYTES),
        name="mixer",
    )(x_rows, hist0, s0, *weights)


def _ffn_kernel(h_ref, norm_ffn_ref, w_gate_ref, w_up_ref, w_down_ref, norm_final_ref, y_ref,
                gate_ref, up_ref):
    hn = _rmsnorm(h_ref[...], norm_ffn_ref[...]).astype(_BF16)
    gate_ref[...] = _dot(hn, w_gate_ref[...])
    up_ref[...] = _dot(hn, w_up_ref[...])
    gate = gate_ref[...]
    f = (gate * jax.nn.sigmoid(gate) * up_ref[...]).astype(_BF16)
    h2 = h_ref[...] + _dot(f, w_down_ref[...])
    y_ref[...] = _rmsnorm(h2, norm_final_ref[...])


def _ffn(h_rows, weights):
    n_rows = h_rows.shape[0]
    rows = min(ROW_TILE, n_rows)
    assert n_rows % rows == 0
    row_spec = pl.BlockSpec((rows, D_MODEL), lambda i: (i, 0))
    return pl.pallas_call(
        _ffn_kernel,
        grid=(n_rows // rows,),
        in_specs=[row_spec] + [_resident(w.shape) for w in weights],
        out_specs=row_spec,
        out_shape=jax.ShapeDtypeStruct(h_rows.shape, _F32),
        scratch_shapes=[pltpu.VMEM((rows, D_FF), _F32), pltpu.VMEM((rows, D_FF), _F32)],
        compiler_params=pltpu.CompilerParams(dimension_semantics=("arbitrary",),
                                             vmem_limit_bytes=VMEM_LIMIT_BYTES),
        name="ffn",
    )(h_rows, *weights)


def _state_to_lanes(s_re, s_im):
    b = s_re.shape[0]
    st = jnp.stack([s_re.reshape(b, N_CHUNKS, 2, SSM_STATE), s_im.reshape(b, N_CHUNKS, 2, SSM_STATE)], axis=2)
    return st.reshape(b, STATE_LANES)


def _lanes_to_state(s):
    b = s.shape[0]
    st = s.reshape(b, N_CHUNKS, 2, 2, SSM_STATE)
    return st[:, :, 0].reshape(b, SSM_GROUPS, SSM_STATE), st[:, :, 1].reshape(b, SSM_GROUPS, SSM_STATE)


def _s5_matrices(bp_re, bp_im, c_re, c_im):
    gl = SSM_GROUPS // 2
    eye = jnp.eye(gl, dtype=_F32)
    bp = jnp.stack([bp_re, bp_im]).reshape(2, 2, gl, SSM_GROUP_DIM, SSM_STATE)
    bm = jnp.einsum('ab,qfahp->fahbqp', eye, bp)
    bm = bm.reshape(2, gl, SSM_GROUP_DIM, HALF_CHUNKS, 2, 2, SSM_STATE).transpose(0, 1, 2, 3, 5, 4, 6)
    bm = bm.reshape(2, MXU_DIM, HALF_LANES)
    cc = jnp.stack([c_re, -c_im]).reshape(2, 2, gl, SSM_GROUP_DIM, SSM_STATE)
    cm = jnp.einsum('ba,qfahp->fbqpah', eye, cc)
    cm = cm.reshape(2, HALF_CHUNKS, 2, 2, SSM_STATE, gl, SSM_GROUP_DIM).transpose(0, 1, 3, 2, 4, 5, 6)
    cm = cm.reshape(2, HALF_LANES, MXU_DIM)
    return bm.astype(_BF16), cm.astype(_BF16)


def _pool_block_diag(pool_w):
    z = jnp.zeros((POOL_GROUP_DIM, POOL_GROUP_DIM), pool_w.dtype)
    blocks = [jnp.block([[pool_w[2 * k], z], [z, pool_w[2 * k + 1]]]) for k in range(2)]
    return jnp.stack(blocks).astype(_BF16)


def _to_time_major(x):
    b, t, d = x.shape
    return x.transpose(1, 0, 2).reshape(t * b, d)


def _from_time_major(rows, b, t):
    return rows.reshape(t, b, rows.shape[-1]).transpose(1, 0, 2)


def kernel(x_prompt, x_sample, state_pool, state_ssm_re, state_ssm_im, norm_mix, w_in, pool_w, pool_scale, ssm_a_re, ssm_a_im, ssm_log_dt, ssm_b_re, ssm_b_im, ssm_c_re, ssm_c_im, ssm_d, glu_w, glu_b, w_branch_pool, w_branch_ssm, w_out, norm_ffn, ffn_w_gate, ffn_w_up, ffn_w_down, norm_final):
    assert w_in.shape[0] == 1, "single-layer trunk"
    bp_n, t_p, _ = x_prompt.shape
    bs_n, t_s, _ = x_sample.shape

    ab_re, ab_im, bp_re, bp_im = _discretise(ssm_a_re[0], ssm_a_im[0], ssm_log_dt[0], ssm_b_re[0], ssm_b_im[0])
    b_mat, c_mat = _s5_matrices(bp_re, bp_im, ssm_c_re[0], ssm_c_im[0])
    ab = jnp.concatenate([ab_re.reshape(N_CHUNKS, LANES), ab_im.reshape(N_CHUNKS, LANES)], axis=0)
    mixer_w = (norm_mix[0].reshape(1, D_MODEL), w_in[0].astype(_BF16), _pool_block_diag(pool_w[0]),
               pool_scale[0].reshape(1, POOL_WIDTH), ab, b_mat, c_mat, ssm_d[0].reshape(1, SSM_WIDTH),
               glu_w[0].astype(_BF16), glu_b[0].reshape(1, SSM_WIDTH), w_branch_pool[0].astype(_BF16),
               w_branch_ssm[0].astype(_BF16), w_out[0].astype(_BF16))
    ffn_w = (norm_ffn[0].reshape(1, D_MODEL), ffn_w_gate[0].astype(_BF16), ffn_w_up[0].astype(_BF16),
             ffn_w_down[0].astype(_BF16), norm_final.reshape(1, D_MODEL))

    def run(x, hist, s_re, s_im, start_pos):
        b, t, _ = x.shape
        hist0 = jnp.concatenate([jnp.zeros((1, b, POOL_WIDTH), _F32), hist.transpose(1, 0, 2)], axis=0)
        h_rows, hist_out, s_out = _mixer(_to_time_major(x), hist0.reshape(HIST_STEPS * b, POOL_WIDTH),
                                         _state_to_lanes(s_re, s_im), mixer_w,
                                         n_batch=b, n_steps_total=t, start_pos=start_pos)
        y = _from_time_major(_ffn(h_rows, ffn_w), b, t)
        new_hist = hist_out.reshape(HIST_STEPS, b, POOL_WIDTH)[1:].transpose(1, 0, 2)
        new_re, new_im = _lanes_to_state(s_out)
        return y, new_hist[None], new_re[None], new_im[None]

    zero_hist = jnp.zeros((bp_n, POOL_HIST, POOL_WIDTH), _F32)
    zero_ssm = jnp.zeros((bp_n, SSM_GROUPS, SSM_STATE), _F32)
    y_p, pool_p, re_p, im_p = run(x_prompt, zero_hist, zero_ssm, zero_ssm, 0)
    y_s, pool_s, re_s, im_s = run(x_sample, state_pool[0], state_ssm_re[0], state_ssm_im[0], PAST_LEN)
    return (y_p, y_s, pool_p, re_p, im_p, pool_s, re_s, im_s)
```

```python
import functools
import math

import jax
import jax.numpy as jnp
from jax import lax
from jax.experimental import pallas as pl
from jax.experimental.pallas import tpu as pltpu

D_MODEL = 1024
PAST_LEN = 16384
POOL_WIDTH = 512
POOL_WINDOWS = (2, 4, 8, 16)
POOL_GROUP_DIM = 128
POOL_HIST = 15
HIST_STEPS = POOL_HIST + 1
SSM_WIDTH = 512
SSM_GROUPS = 32
SSM_GROUP_DIM = 16
SSM_STATE = 64
N_STATES = SSM_GROUPS * SSM_STATE
IN_WIDTH = POOL_WIDTH + SSM_WIDTH + 2 * D_MODEL
D_FF = 2816
RMS_EPS = 1e-6

SUBLANES = 8
LANES = 128
MXU_DIM = 256
CHUNK_LANES = 2 * LANES
N_CHUNKS = N_STATES // LANES
HALF_CHUNKS = N_CHUNKS // 2
HALF_LANES = HALF_CHUNKS * CHUNK_LANES
HALF_ROWS = MXU_DIM
ROW_TILE = 512
VMEM_LIMIT_BYTES = 56 * 1024 * 1024

_F32 = jnp.float32
_BF16 = jnp.bfloat16


def _rmsnorm(x, g):
    return x * lax.rsqrt(jnp.mean(x * x, axis=-1, keepdims=True) + RMS_EPS) * g


def _dot(a, b):
    return jnp.dot(a, b, preferred_element_type=_F32)


def _s5_prep_kernel(a_re_ref, a_im_ref, log_dt_ref, xb_ref, xc_ref,
                    ab_re_ref, ab_im_ref, b_mat_ref, c_mat_ref):
    a_re = a_re_ref[...]
    a_im = a_im_ref[...]
    dt = jnp.exp(log_dt_ref[...])
    mag = jnp.exp(a_re * dt)
    ab_re = mag * jnp.cos(a_im * dt)
    ab_im = mag * jnp.sin(a_im * dt)
    num_re = ab_re - 1.0
    num_im = ab_im
    den = a_re * a_re + a_im * a_im
    coef_re = (num_re * a_re + num_im * a_im) / den
    coef_im = (num_im * a_re - num_re * a_im) / den
    ab_re_ref[...] = ab_re
    ab_im_ref[...] = ab_im

    n_rows = SSM_GROUPS * SSM_GROUP_DIM
    cr = jnp.broadcast_to(coef_re, (SSM_GROUPS, SSM_GROUP_DIM, LANES)).reshape(n_rows, LANES)
    ci = jnp.broadcast_to(coef_im, (SSM_GROUPS, SSM_GROUP_DIM, LANES)).reshape(n_rows, LANES)
    b_re = xb_ref[0]
    b_im = xb_ref[1]
    bp = (cr * b_re - ci * b_im, cr * b_im + ci * b_re)
    row = lax.broadcasted_iota(jnp.int32, (HALF_ROWS, LANES), 0)
    lane = lax.broadcasted_iota(jnp.int32, (HALF_ROWS, LANES), 1)
    own_lanes = (lane >> 6) == ((row >> 4) & 1)
    for half in range(2):
        for part in range(2):
            mine = jnp.where(own_lanes, bp[part][half * HALF_ROWS:(half + 1) * HALF_ROWS, :], 0.0)
            for cl in range(HALF_CHUNKS):
                blk = jnp.where((row >> 5) == cl, mine, 0.0)
                lo = cl * CHUNK_LANES + part * LANES
                b_mat_ref[half, :, lo:lo + LANES] = blk.astype(_BF16)

    row_c = lax.broadcasted_iota(jnp.int32, (LANES, HALF_ROWS), 0)
    col_c = lax.broadcasted_iota(jnp.int32, (LANES, HALF_ROWS), 1)
    own_cols = (row_c >> 6) == ((col_c >> 4) & 1)
    for half in range(2):
        for part in range(2):
            w = xc_ref[part, :, half * HALF_ROWS:(half + 1) * HALF_ROWS]
            if part == 1:
                w = -w
            mine = jnp.where(own_cols, w, 0.0)
            for cl in range(HALF_CHUNKS):
                blk = jnp.where((col_c >> 5) == cl, mine, 0.0)
                lo = cl * CHUNK_LANES + part * LANES
                c_mat_ref[half, lo:lo + LANES, :] = blk.astype(_BF16)


def _s5_prep(a_re, a_im, log_dt, b_re, b_im, c_re, c_im):
    g, p, h = SSM_GROUPS, SSM_STATE, SSM_GROUP_DIM
    xb = jnp.stack([b_re, b_im]).transpose(0, 1, 3, 2)
    xb = jnp.concatenate([xb, xb], axis=-1).reshape(2, g * h, LANES)
    xc = jnp.stack([c_re, c_im]).transpose(0, 3, 1, 2).reshape(2, p, g * h)
    xc = jnp.concatenate([xc, xc], axis=1)
    twice = lambda a: jnp.concatenate([a, a], axis=-1).reshape(g, 1, LANES)
    return pl.pallas_call(
        _s5_prep_kernel,
        out_shape=(jax.ShapeDtypeStruct((g, 1, LANES), _F32), jax.ShapeDtypeStruct((g, 1, LANES), _F32),
                   jax.ShapeDtypeStruct((2, HALF_ROWS, HALF_LANES), _BF16),
                   jax.ShapeDtypeStruct((2, HALF_LANES, HALF_ROWS), _BF16)),
        name="s5_prep",
    )(twice(a_re), twice(a_im), log_dt.reshape(g, 1, 1), xb, xc)


def _scan_half(xs_ref, s_re_ref, s_im_ref, ab_re_ref, ab_im_ref, half, n_batch, n_steps):
    lane = lax.broadcasted_iota(jnp.int32, (1, LANES), 1)

    def coeff(ref, c):
        row = jnp.where(lane < SSM_STATE, ref[2 * c], ref[2 * c + 1])
        return jnp.broadcast_to(row, (SUBLANES, LANES))

    def recur(r0, row_of_step, cls):
        chunks = [half * HALF_CHUNKS + cl for cl in cls]
        ar = [coeff(ab_re_ref, c) for c in chunks]
        ai = [coeff(ab_im_ref, c) for c in chunks]
        sr = [s_re_ref[pl.ds(r0, SUBLANES), c * LANES:(c + 1) * LANES] for c in chunks]
        si = [s_im_ref[pl.ds(r0, SUBLANES), c * LANES:(c + 1) * LANES] for c in chunks]
        for t in range(n_steps):
            rows = pl.ds(row_of_step(t), SUBLANES)
            for k, cl in enumerate(cls):
                lo = cl * CHUNK_LANES
                nr = ar[k] * sr[k] - ai[k] * si[k] + xs_ref[rows, lo:lo + LANES]
                ni = ar[k] * si[k] + ai[k] * sr[k] + xs_ref[rows, lo + LANES:lo + CHUNK_LANES]
                xs_ref[rows, lo:lo + LANES] = nr
                xs_ref[rows, lo + LANES:lo + CHUNK_LANES] = ni
                sr[k], si[k] = nr, ni
        for k, c in enumerate(chunks):
            s_re_ref[pl.ds(r0, SUBLANES), c * LANES:(c + 1) * LANES] = sr[k]
            s_im_ref[pl.ds(r0, SUBLANES), c * LANES:(c + 1) * LANES] = si[k]

    if n_batch == SUBLANES:
        recur(0, lambda t: t * n_batch, list(range(HALF_CHUNKS)))
    else:
        group = 4
        for cg in range(0, HALF_CHUNKS, group):
            def row_block(rb, carry, cg=cg):
                r0 = pl.multiple_of(rb * SUBLANES, SUBLANES)
                recur(r0, lambda t: pl.multiple_of(t * n_batch + r0, SUBLANES), list(range(cg, cg + group)))
                return carry
            lax.fori_loop(0, n_batch // SUBLANES, row_block, 0)


def _mixer_kernel(x_ref, hist0_ref, s0_re_ref, s0_im_ref, norm_ref, w_in_ref, pool_w_ref, pool_scale_ref,
                  ab_re_ref, ab_im_ref, b_ref, c_ref, d_ref, glu_w_ref, glu_b_ref, w_bp_ref, w_bs_ref, w_out_ref,
                  h_ref, hist_ref, s_re_ref, s_im_ref,
                  x_tm_ref, proj_ref, ext_ref, xs_ref, *, n_batch, n_steps, start_pos, batch_major_in):
    i = pl.program_id(0)
    rows = n_batch * n_steps
    hist_rows = HIST_STEPS * n_batch

    @pl.when(i == 0)
    def _():
        hist_ref[...] = hist0_ref[...]
        s_re_ref[...] = s0_re_ref[...]
        s_im_ref[...] = s0_im_ref[...]

    if batch_major_in:
        x_tm_ref[...] = jnp.swapaxes(x_ref[...], 0, 1).reshape(rows, D_MODEL)
    else:
        x_tm_ref[...] = x_ref[...]
    xn = _rmsnorm(x_tm_ref[...], norm_ref[...]).astype(_BF16)
    proj_ref[...] = _dot(xn, w_in_ref[...])

    ext_ref[0:hist_rows, :] = hist_ref[...]
    ext_ref[hist_rows:, :] = proj_ref[:, 0:POOL_WIDTH]
    hist_ref[...] = ext_ref[rows:rows + hist_rows, :]
    diffs = []
    for g, w in enumerate(POOL_WINDOWS):
        lo = g * POOL_GROUP_DIM
        u_g = ext_ref[hist_rows:, lo:lo + POOL_GROUP_DIM]
        acc = u_g
        for j in range(1, w):
            r0 = (HIST_STEPS - j) * n_batch
            acc = acc + ext_ref[r0:r0 + rows, lo:lo + POOL_GROUP_DIM]
        if start_pos + 1 >= w:
            pooled = acc * (1.0 / w)
        else:
            row = lax.broadcasted_iota(jnp.int32, (rows, POOL_GROUP_DIM), 0)
            step = lax.shift_right_logical(row, int(math.log2(n_batch)))
            pos = start_pos + i * n_steps + step
            pooled = acc / jnp.minimum(w, pos + 1).astype(_F32)
        diffs.append(pooled - u_g)
    mixed = jnp.concatenate(
        [_dot(jnp.concatenate(diffs[0:2], axis=1).astype(_BF16), pool_w_ref[0]),
         _dot(jnp.concatenate(diffs[2:4], axis=1).astype(_BF16), pool_w_ref[1])], axis=1)
    a_out = (mixed * pool_scale_ref[...]).astype(_BF16)

    u_ssm = proj_ref[:, POOL_WIDTH:POOL_WIDTH + SSM_WIDTH]
    u_bf = u_ssm.astype(_BF16)
    ys = []
    for half in range(2):
        xs_ref[...] = _dot(u_bf[:, half * HALF_ROWS:(half + 1) * HALF_ROWS], b_ref[half])
        _scan_half(xs_ref, s_re_ref, s_im_ref, ab_re_ref, ab_im_ref, half, n_batch, n_steps)
        ys.append(_dot(xs_ref[...].astype(_BF16), c_ref[half]))
    y_ssm = jnp.concatenate(ys, axis=1) + d_ref[...] * u_ssm
    z = jax.nn.gelu(y_ssm)
    glu = jax.nn.sigmoid(_dot(z.astype(_BF16), glu_w_ref[...]) + glu_b_ref[...])
    b_out = (z * glu).astype(_BF16)

    off = POOL_WIDTH + SSM_WIDTH
    merged = (jax.nn.sigmoid(proj_ref[:, off:off + D_MODEL]) * _dot(a_out, w_bp_ref[...])
              + jax.nn.sigmoid(proj_ref[:, off + D_MODEL:off + 2 * D_MODEL]) * _dot(b_out, w_bs_ref[...]))
    h_ref[...] = x_tm_ref[...] + _dot(merged.astype(_BF16), w_out_ref[...])


def _resident(shape):
    zeros = (0,) * len(shape)
    return pl.BlockSpec(shape, lambda i: zeros, pipeline_mode=pl.Buffered(1))


def _tiling(n_batch, n_steps_total):
    n_steps = min(n_steps_total, ROW_TILE // n_batch)
    n_tiles = n_steps_total // n_steps
    assert n_tiles * n_steps == n_steps_total and n_batch % SUBLANES == 0
    assert n_batch & (n_batch - 1) == 0
    return n_steps, n_tiles


def _mixer(x, hist0, s0_re, s0_im, weights, *, n_batch, n_steps_total, start_pos, batch_major_in):
    n_steps, n_tiles = _tiling(n_batch, n_steps_total)
    rows = n_steps * n_batch
    hist_rows = HIST_STEPS * n_batch
    kern = functools.partial(_mixer_kernel, n_batch=n_batch, n_steps=n_steps, start_pos=start_pos,
                             batch_major_in=batch_major_in)
    row_spec = pl.BlockSpec((rows, D_MODEL), lambda i: (i, 0))
    x_spec = pl.BlockSpec((n_batch, n_steps, D_MODEL), lambda i: (0, i, 0)) if batch_major_in else row_spec
    return pl.pallas_call(
        kern,
        grid=(n_tiles,),
        in_specs=[x_spec, _resident(hist0.shape), _resident(s0_re.shape), _resident(s0_im.shape)]
        + [_resident(w.shape) for w in weights],
        out_specs=(row_spec,
                   pl.BlockSpec((hist_rows, POOL_WIDTH), lambda i: (0, 0)),
                   pl.BlockSpec((n_batch, N_STATES), lambda i: (0, 0)),
                   pl.BlockSpec((n_batch, N_STATES), lambda i: (0, 0))),
        out_shape=(jax.ShapeDtypeStruct((n_steps_total * n_batch, D_MODEL), _F32),
                   jax.ShapeDtypeStruct((hist_rows, POOL_WIDTH), _F32),
                   jax.ShapeDtypeStruct((n_batch, N_STATES), _F32),
                   jax.ShapeDtypeStruct((n_batch, N_STATES), _F32)),
        scratch_shapes=[pltpu.VMEM((rows, D_MODEL), _F32),
                        pltpu.VMEM((rows, IN_WIDTH), _F32),
                        pltpu.VMEM((hist_rows + rows, POOL_WIDTH), _F32),
                        pltpu.VMEM((rows, HALF_LANES), _F32)],
        compiler_params=pltpu.CompilerParams(dimension_semantics=("arbitrary",),
                                             vmem_limit_bytes=VMEM_LIMIT_BYTES),
        name="mixer",
    )(x, hist0, s0_re, s0_im, *weights)


def _ffn_kernel(h_ref, norm_ffn_ref, w_gate_ref, w_up_ref, w_down_ref, norm_final_ref, y_ref,
                gate_ref, up_ref, *, batch_major_out):
    hn = _rmsnorm(h_ref[...], norm_ffn_ref[...]).astype(_BF16)
    gate_ref[...] = _dot(hn, w_gate_ref[...])
    up_ref[...] = _dot(hn, w_up_ref[...])
    gate = gate_ref[...]
    f = (gate * jax.nn.sigmoid(gate) * up_ref[...]).astype(_BF16)
    h2 = h_ref[...] + _dot(f, w_down_ref[...])
    y = _rmsnorm(h2, norm_final_ref[...])
    if batch_major_out:
        n_batch, n_steps, _ = y_ref.shape
        y_ref[...] = jnp.swapaxes(y.reshape(n_steps, n_batch, D_MODEL), 0, 1)
    else:
        y_ref[...] = y


def _ffn(h_rows, weights, *, n_batch, n_steps_total, batch_major_out):
    n_steps, n_tiles = _tiling(n_batch, n_steps_total)
    rows = n_steps * n_batch
    row_spec = pl.BlockSpec((rows, D_MODEL), lambda i: (i, 0))
    if batch_major_out:
        out_spec = pl.BlockSpec((n_batch, n_steps, D_MODEL), lambda i: (0, i, 0))
        out_shape = jax.ShapeDtypeStruct((n_batch, n_steps_total, D_MODEL), _F32)
    else:
        out_spec, out_shape = row_spec, jax.ShapeDtypeStruct(h_rows.shape, _F32)
    return pl.pallas_call(
        functools.partial(_ffn_kernel, batch_major_out=batch_major_out),
        grid=(n_tiles,),
        in_specs=[row_spec] + [_resident(w.shape) for w in weights],
        out_specs=out_spec,
        out_shape=out_shape,
        scratch_shapes=[pltpu.VMEM((rows, D_FF), _F32), pltpu.VMEM((rows, D_FF), _F32)],
        compiler_params=pltpu.CompilerParams(dimension_semantics=("arbitrary",),
                                             vmem_limit_bytes=VMEM_LIMIT_BYTES),
        name="ffn",
    )(h_rows, *weights)


def _pool_block_diag(pool_w):
    z = jnp.zeros((POOL_GROUP_DIM, POOL_GROUP_DIM), pool_w.dtype)
    blocks = [jnp.block([[pool_w[2 * k], z], [z, pool_w[2 * k + 1]]]) for k in range(2)]
    return jnp.stack(blocks).astype(_BF16)


def kernel(x_prompt, x_sample, state_pool, state_ssm_re, state_ssm_im, norm_mix, w_in, pool_w, pool_scale, ssm_a_re, ssm_a_im, ssm_log_dt, ssm_b_re, ssm_b_im, ssm_c_re, ssm_c_im, ssm_d, glu_w, glu_b, w_branch_pool, w_branch_ssm, w_out, norm_ffn, ffn_w_gate, ffn_w_up, ffn_w_down, norm_final):
    assert w_in.shape[0] == 1, "single-layer trunk"
    ab_re, ab_im, b_mat, c_mat = _s5_prep(ssm_a_re[0], ssm_a_im[0], ssm_log_dt[0], ssm_b_re[0], ssm_b_im[0],
                                          ssm_c_re[0], ssm_c_im[0])
    mixer_w = (norm_mix[0].reshape(1, D_MODEL), w_in[0].astype(_BF16), _pool_block_diag(pool_w[0]),
               pool_scale[0].reshape(1, POOL_WIDTH), ab_re, ab_im, b_mat, c_mat, ssm_d[0].reshape(1, SSM_WIDTH),
               glu_w[0].astype(_BF16), glu_b[0].reshape(1, SSM_WIDTH), w_branch_pool[0].astype(_BF16),
               w_branch_ssm[0].astype(_BF16), w_out[0].astype(_BF16))
    ffn_w = (norm_ffn[0].reshape(1, D_MODEL), ffn_w_gate[0].astype(_BF16), ffn_w_up[0].astype(_BF16),
             ffn_w_down[0].astype(_BF16), norm_final.reshape(1, D_MODEL))

    def run(x, hist, s_re, s_im, start_pos, batch_major):
        b, t, _ = x.shape
        hist0 = jnp.concatenate([jnp.zeros((1, b, POOL_WIDTH), _F32), hist.transpose(1, 0, 2)], axis=0)
        x_in = x if batch_major else x.transpose(1, 0, 2).reshape(t * b, D_MODEL)
        h_rows, hist_out, s_re_out, s_im_out = _mixer(
            x_in, hist0.reshape(HIST_STEPS * b, POOL_WIDTH), s_re.reshape(b, N_STATES), s_im.reshape(b, N_STATES),
            mixer_w, n_batch=b, n_steps_total=t, start_pos=start_pos, batch_major_in=batch_major)
        y = _ffn(h_rows, ffn_w, n_batch=b, n_steps_total=t, batch_major_out=batch_major)
        if not batch_major:
            y = y.reshape(t, b, D_MODEL).transpose(1, 0, 2)
        new_hist = hist_out.reshape(HIST_STEPS, b, POOL_WIDTH)[1:].transpose(1, 0, 2)
        shape = (1, b, SSM_GROUPS, SSM_STATE)
        return y, new_hist[None], s_re_out.reshape(shape), s_im_out.reshape(shape)

    bp_n = x_prompt.shape[0]
    zero_hist = jnp.zeros((bp_n, POOL_HIST, POOL_WIDTH), _F32)
    zero_ssm = jnp.zeros((bp_n, SSM_GROUPS, SSM_STATE), _F32)
    y_p, pool_p, re_p, im_p = run(x_prompt, zero_hist, zero_ssm, zero_ssm, 0, True)
    y_s, pool_s, re_s, im_s = run(x_sample, state_pool[0], state_ssm_re[0], state_ssm_im[0], PAST_LEN, False)
    return (y_p, y_s, pool_p, re_p, im_p, pool_s, re_s, im_s)
```

```python
import functools
import math

import jax
import jax.numpy as jnp
from jax import lax
from jax.experimental import pallas as pl
from jax.experimental.pallas import tpu as pltpu

D_MODEL = 1024
PAST_LEN = 16384
POOL_WIDTH = 512
POOL_WINDOWS = (2, 4, 8, 16)
POOL_GROUP_DIM = 128
POOL_HIST = 15
HIST_STEPS = POOL_HIST + 1
SSM_WIDTH = 512
SSM_GROUPS = 32
SSM_GROUP_DIM = 16
SSM_STATE = 64
N_STATES = SSM_GROUPS * SSM_STATE
IN_WIDTH = POOL_WIDTH + SSM_WIDTH + 2 * D_MODEL
D_FF = 2816
RMS_EPS = 1e-6

SUBLANES = 8
LANES = 128
MXU_DIM = 256
CHUNK_LANES = 2 * LANES
N_CHUNKS = N_STATES // LANES
HALF_CHUNKS = N_CHUNKS // 2
HALF_LANES = HALF_CHUNKS * CHUNK_LANES
HALF_ROWS = MXU_DIM
ROW_TILE = 512
FFN_SUB_TILES = 2
VMEM_LIMIT_BYTES = 56 * 1024 * 1024

_F32 = jnp.float32
_BF16 = jnp.bfloat16


def _rmsnorm(x, g):
    return x * lax.rsqrt(jnp.mean(x * x, axis=-1, keepdims=True) + RMS_EPS) * g


def _dot(a, b):
    return jnp.dot(a, b, preferred_element_type=_F32)


def _s5_prep_kernel(a_re_ref, a_im_ref, log_dt_ref, xb_ref, xc_ref,
                    ab_re_ref, ab_im_ref, b_mat_ref, c_mat_ref):
    a_re = a_re_ref[...]
    a_im = a_im_ref[...]
    dt = jnp.exp(log_dt_ref[...])
    mag = jnp.exp(a_re * dt)
    ab_re = mag * jnp.cos(a_im * dt)
    ab_im = mag * jnp.sin(a_im * dt)
    num_re = ab_re - 1.0
    num_im = ab_im
    den = a_re * a_re + a_im * a_im
    coef_re = (num_re * a_re + num_im * a_im) / den
    coef_im = (num_im * a_re - num_re * a_im) / den
    ab_re_ref[...] = ab_re
    ab_im_ref[...] = ab_im

    n_rows = SSM_GROUPS * SSM_GROUP_DIM
    cr = jnp.broadcast_to(coef_re, (SSM_GROUPS, SSM_GROUP_DIM, LANES)).reshape(n_rows, LANES)
    ci = jnp.broadcast_to(coef_im, (SSM_GROUPS, SSM_GROUP_DIM, LANES)).reshape(n_rows, LANES)
    b_re = xb_ref[0]
    b_im = xb_ref[1]
    bp = (cr * b_re - ci * b_im, cr * b_im + ci * b_re)
    row = lax.broadcasted_iota(jnp.int32, (HALF_ROWS, LANES), 0)
    lane = lax.broadcasted_iota(jnp.int32, (HALF_ROWS, LANES), 1)
    own_lanes = (lane >> 6) == ((row >> 4) & 1)
    for half in range(2):
        for part in range(2):
            mine = jnp.where(own_lanes, bp[part][half * HALF_ROWS:(half + 1) * HALF_ROWS, :], 0.0)
            for cl in range(HALF_CHUNKS):
                blk = jnp.where((row >> 5) == cl, mine, 0.0)
                lo = cl * CHUNK_LANES + part * LANES
                b_mat_ref[half, :, lo:lo + LANES] = blk.astype(_BF16)

    row_c = lax.broadcasted_iota(jnp.int32, (LANES, HALF_ROWS), 0)
    col_c = lax.broadcasted_iota(jnp.int32, (LANES, HALF_ROWS), 1)
    own_cols = (row_c >> 6) == ((col_c >> 4) & 1)
    for half in range(2):
        for part in range(2):
            w = xc_ref[part, :, half * HALF_ROWS:(half + 1) * HALF_ROWS]
            if part == 1:
                w = -w
            mine = jnp.where(own_cols, w, 0.0)
            for cl in range(HALF_CHUNKS):
                blk = jnp.where((col_c >> 5) == cl, mine, 0.0)
                lo = cl * CHUNK_LANES + part * LANES
                c_mat_ref[half, lo:lo + LANES, :] = blk.astype(_BF16)


def _s5_prep(a_re, a_im, log_dt, b_re, b_im, c_re, c_im):
    g, p, h = SSM_GROUPS, SSM_STATE, SSM_GROUP_DIM
    xb = jnp.stack([b_re, b_im]).transpose(0, 1, 3, 2)
    xb = jnp.concatenate([xb, xb], axis=-1).reshape(2, g * h, LANES)
    xc = jnp.stack([c_re, c_im]).transpose(0, 3, 1, 2).reshape(2, p, g * h)
    xc = jnp.concatenate([xc, xc], axis=1)
    twice = lambda a: jnp.concatenate([a, a], axis=-1).reshape(g, 1, LANES)
    return pl.pallas_call(
        _s5_prep_kernel,
        out_shape=(jax.ShapeDtypeStruct((g, 1, LANES), _F32), jax.ShapeDtypeStruct((g, 1, LANES), _F32),
                   jax.ShapeDtypeStruct((2, HALF_ROWS, HALF_LANES), _BF16),
                   jax.ShapeDtypeStruct((2, HALF_LANES, HALF_ROWS), _BF16)),
        name="s5_prep",
    )(twice(a_re), twice(a_im), log_dt.reshape(g, 1, 1), xb, xc)


def _scan_half(xs_ref, s_re_ref, s_im_ref, ab_re_ref, ab_im_ref, half, n_batch, n_steps):
    lane = lax.broadcasted_iota(jnp.int32, (1, LANES), 1)

    def coeff(ref, c):
        row = jnp.where(lane < SSM_STATE, ref[2 * c], ref[2 * c + 1])
        return jnp.broadcast_to(row, (SUBLANES, LANES))

    def recur(r0, row_of_step, cls):
        chunks = [half * HALF_CHUNKS + cl for cl in cls]
        ar = [coeff(ab_re_ref, c) for c in chunks]
        ai = [coeff(ab_im_ref, c) for c in chunks]
        sr = [s_re_ref[pl.ds(r0, SUBLANES), c * LANES:(c + 1) * LANES] for c in chunks]
        si = [s_im_ref[pl.ds(r0, SUBLANES), c * LANES:(c + 1) * LANES] for c in chunks]
        for t in range(n_steps):
            rows = pl.ds(row_of_step(t), SUBLANES)
            for k, cl in enumerate(cls):
                lo = cl * CHUNK_LANES
                nr = ar[k] * sr[k] - ai[k] * si[k] + xs_ref[rows, lo:lo + LANES]
                ni = ar[k] * si[k] + ai[k] * sr[k] + xs_ref[rows, lo + LANES:lo + CHUNK_LANES]
                xs_ref[rows, lo:lo + LANES] = nr
                xs_ref[rows, lo + LANES:lo + CHUNK_LANES] = ni
                sr[k], si[k] = nr, ni
        for k, c in enumerate(chunks):
            s_re_ref[pl.ds(r0, SUBLANES), c * LANES:(c + 1) * LANES] = sr[k]
            s_im_ref[pl.ds(r0, SUBLANES), c * LANES:(c + 1) * LANES] = si[k]

    if n_batch == SUBLANES:
        recur(0, lambda t: t * n_batch, list(range(HALF_CHUNKS)))
    else:
        group = 4
        for cg in range(0, HALF_CHUNKS, group):
            def row_block(rb, carry, cg=cg):
                r0 = pl.multiple_of(rb * SUBLANES, SUBLANES)
                recur(r0, lambda t: pl.multiple_of(t * n_batch + r0, SUBLANES), list(range(cg, cg + group)))
                return carry
            lax.fori_loop(0, n_batch // SUBLANES, row_block, 0)


def _mixer_kernel(x_ref, hist0_ref, s0_re_ref, s0_im_ref, norm_ref, w_in_ref, pool_w_ref, pool_scale_ref,
                  ab_re_ref, ab_im_ref, b_ref, c_ref, d_ref, glu_w_ref, glu_b_ref, w_bp_ref, w_bs_ref, w_out_ref,
                  h_ref, hist_ref, s_re_ref, s_im_ref,
                  x_tm_ref, proj_ref, ext_ref, xs_ref, *, n_batch, n_steps, start_pos, batch_major_in):
    i = pl.program_id(0)
    rows = n_batch * n_steps
    hist_rows = HIST_STEPS * n_batch

    @pl.when(i == 0)
    def _():
        hist_ref[...] = hist0_ref[...]
        s_re_ref[...] = s0_re_ref[...]
        s_im_ref[...] = s0_im_ref[...]

    if batch_major_in:
        x_tm_ref[...] = jnp.swapaxes(x_ref[...], 0, 1).reshape(rows, D_MODEL)
    else:
        x_tm_ref[...] = x_ref[...]
    xn = _rmsnorm(x_tm_ref[...], norm_ref[...]).astype(_BF16)
    proj_ref[...] = _dot(xn, w_in_ref[...])

    ext_ref[0:hist_rows, :] = hist_ref[...]
    ext_ref[hist_rows:, :] = proj_ref[:, 0:POOL_WIDTH]
    hist_ref[...] = ext_ref[rows:rows + hist_rows, :]
    diffs = []
    for g, w in enumerate(POOL_WINDOWS):
        lo = g * POOL_GROUP_DIM
        u_g = ext_ref[hist_rows:, lo:lo + POOL_GROUP_DIM]
        acc = u_g
        for j in range(1, w):
            r0 = (HIST_STEPS - j) * n_batch
            acc = acc + ext_ref[r0:r0 + rows, lo:lo + POOL_GROUP_DIM]
        if start_pos + 1 >= w:
            pooled = acc * (1.0 / w)
        else:
            row = lax.broadcasted_iota(jnp.int32, (rows, POOL_GROUP_DIM), 0)
            step = lax.shift_right_logical(row, int(math.log2(n_batch)))
            pos = start_pos + i * n_steps + step
            pooled = acc / jnp.minimum(w, pos + 1).astype(_F32)
        diffs.append(pooled - u_g)
    mixed = jnp.concatenate(
        [_dot(jnp.concatenate(diffs[0:2], axis=1).astype(_BF16), pool_w_ref[0]),
         _dot(jnp.concatenate(diffs[2:4], axis=1).astype(_BF16), pool_w_ref[1])], axis=1)
    a_out = (mixed * pool_scale_ref[...]).astype(_BF16)

    u_ssm = proj_ref[:, POOL_WIDTH:POOL_WIDTH + SSM_WIDTH]
    u_bf = u_ssm.astype(_BF16)
    for half in range(2):
        xs_ref[half] = _dot(u_bf[:, half * HALF_ROWS:(half + 1) * HALF_ROWS], b_ref[half])
    for half in range(2):
        _scan_half(xs_ref.at[half], s_re_ref, s_im_ref, ab_re_ref, ab_im_ref, half, n_batch, n_steps)
    ys = [_dot(xs_ref[half].astype(_BF16), c_ref[half]) for half in range(2)]
    y_ssm = jnp.concatenate(ys, axis=1) + d_ref[...] * u_ssm
    z = jax.nn.gelu(y_ssm)
    glu = jax.nn.sigmoid(_dot(z.astype(_BF16), glu_w_ref[...]) + glu_b_ref[...])
    b_out = (z * glu).astype(_BF16)

    off = POOL_WIDTH + SSM_WIDTH
    merged = (jax.nn.sigmoid(proj_ref[:, off:off + D_MODEL]) * _dot(a_out, w_bp_ref[...])
              + jax.nn.sigmoid(proj_ref[:, off + D_MODEL:off + 2 * D_MODEL]) * _dot(b_out, w_bs_ref[...]))
    h_ref[...] = x_tm_ref[...] + _dot(merged.astype(_BF16), w_out_ref[...])


def _resident(shape):
    zeros = (0,) * len(shape)
    return pl.BlockSpec(shape, lambda i: zeros, pipeline_mode=pl.Buffered(1))


def _tiling(n_batch, n_steps_total, rows_per_step=ROW_TILE):
    n_steps = min(n_steps_total, rows_per_step // n_batch)
    n_tiles = n_steps_total // n_steps
    assert n_tiles * n_steps == n_steps_total and n_batch % SUBLANES == 0
    assert n_batch & (n_batch - 1) == 0
    return n_steps, n_tiles


def _mixer(x, hist0, s0_re, s0_im, weights, *, n_batch, n_steps_total, start_pos, batch_major_in):
    n_steps, n_tiles = _tiling(n_batch, n_steps_total)
    rows = n_steps * n_batch
    hist_rows = HIST_STEPS * n_batch
    kern = functools.partial(_mixer_kernel, n_batch=n_batch, n_steps=n_steps, start_pos=start_pos,
                             batch_major_in=batch_major_in)
    row_spec = pl.BlockSpec((rows, D_MODEL), lambda i: (i, 0))
    x_spec = pl.BlockSpec((n_batch, n_steps, D_MODEL), lambda i: (0, i, 0)) if batch_major_in else row_spec
    return pl.pallas_call(
        kern,
        grid=(n_tiles,),
        in_specs=[x_spec, _resident(hist0.shape), _resident(s0_re.shape), _resident(s0_im.shape)]
        + [_resident(w.shape) for w in weights],
        out_specs=(row_spec,
                   pl.BlockSpec((hist_rows, POOL_WIDTH), lambda i: (0, 0)),
                   pl.BlockSpec((n_batch, N_STATES), lambda i: (0, 0)),
                   pl.BlockSpec((n_batch, N_STATES), lambda i: (0, 0))),
        out_shape=(jax.ShapeDtypeStruct((n_steps_total * n_batch, D_MODEL), _F32),
                   jax.ShapeDtypeStruct((hist_rows, POOL_WIDTH), _F32),
                   jax.ShapeDtypeStruct((n_batch, N_STATES), _F32),
                   jax.ShapeDtypeStruct((n_batch, N_STATES), _F32)),
        scratch_shapes=[pltpu.VMEM((rows, D_MODEL), _F32),
                        pltpu.VMEM((rows, IN_WIDTH), _F32),
                        pltpu.VMEM((hist_rows + rows, POOL_WIDTH), _F32),
                        pltpu.VMEM((2, rows, HALF_LANES), _F32)],
        compiler_params=pltpu.CompilerParams(dimension_semantics=("arbitrary",),
                                             vmem_limit_bytes=VMEM_LIMIT_BYTES),
        name="mixer",
    )(x, hist0, s0_re, s0_im, *weights)


def _ffn_kernel(h_ref, norm_ffn_ref, w_gate_ref, w_up_ref, w_down_ref, norm_final_ref, y_ref,
                hn_ref, gate_ref, up_ref, h2_ref, *, n_batch, n_sub, batch_major_out):
    sub_rows = h_ref.shape[0] // n_sub
    sub_steps = sub_rows // n_batch

    def head(s):
        hn_ref[s] = _rmsnorm(h_ref[s * sub_rows:(s + 1) * sub_rows, :], norm_ffn_ref[...]).astype(_BF16)

    def tail(s):
        y = _rmsnorm(h2_ref[s], norm_final_ref[...])
        if batch_major_out:
            y_ref[:, s * sub_steps:(s + 1) * sub_steps, :] = jnp.swapaxes(
                y.reshape(sub_steps, n_batch, D_MODEL), 0, 1)
        else:
            y_ref[s * sub_rows:(s + 1) * sub_rows, :] = y

    for s in range(n_sub):
        head(s)
    for s in range(n_sub):
        gate_ref[...] = _dot(hn_ref[s], w_gate_ref[...])
        up_ref[...] = _dot(hn_ref[s], w_up_ref[...])
        gate = gate_ref[...]
        f = (gate * jax.nn.sigmoid(gate) * up_ref[...]).astype(_BF16)
        h2_ref[s] = h_ref[s * sub_rows:(s + 1) * sub_rows, :] + _dot(f, w_down_ref[...])
    for s in range(n_sub):
        tail(s)


def _ffn(h_rows, weights, *, n_batch, n_steps_total, batch_major_out):
    n_sub = max(1, min(FFN_SUB_TILES, n_steps_total * n_batch // ROW_TILE))
    n_steps, n_tiles = _tiling(n_batch, n_steps_total, ROW_TILE * n_sub)
    rows = n_steps * n_batch
    row_spec = pl.BlockSpec((rows, D_MODEL), lambda i: (i, 0))
    if batch_major_out:
        out_spec = pl.BlockSpec((n_batch, n_steps, D_MODEL), lambda i: (0, i, 0))
        out_shape = jax.ShapeDtypeStruct((n_batch, n_steps_total, D_MODEL), _F32)
    else:
        out_spec, out_shape = row_spec, jax.ShapeDtypeStruct(h_rows.shape, _F32)
    return pl.pallas_call(
        functools.partial(_ffn_kernel, n_batch=n_batch, n_sub=n_sub, batch_major_out=batch_major_out),
        grid=(n_tiles,),
        in_specs=[row_spec] + [_resident(w.shape) for w in weights],
        out_specs=out_spec,
        out_shape=out_shape,
        scratch_shapes=[pltpu.VMEM((n_sub, rows // n_sub, D_MODEL), _BF16),
                        pltpu.VMEM((rows // n_sub, D_FF), _F32), pltpu.VMEM((rows // n_sub, D_FF), _F32),
                        pltpu.VMEM((n_sub, rows // n_sub, D_MODEL), _F32)],
        compiler_params=pltpu.CompilerParams(dimension_semantics=("arbitrary",),
                                             vmem_limit_bytes=VMEM_LIMIT_BYTES),
        name="ffn",
    )(h_rows, *weights)


def _pool_block_diag(pool_w):
    z = jnp.zeros((POOL_GROUP_DIM, POOL_GROUP_DIM), pool_w.dtype)
    blocks = [jnp.block([[pool_w[2 * k], z], [z, pool_w[2 * k + 1]]]) for k in range(2)]
    return jnp.stack(blocks).astype(_BF16)


def kernel(x_prompt, x_sample, state_pool, state_ssm_re, state_ssm_im, norm_mix, w_in, pool_w, pool_scale, ssm_a_re, ssm_a_im, ssm_log_dt, ssm_b_re, ssm_b_im, ssm_c_re, ssm_c_im, ssm_d, glu_w, glu_b, w_branch_pool, w_branch_ssm, w_out, norm_ffn, ffn_w_gate, ffn_w_up, ffn_w_down, norm_final):
    assert w_in.shape[0] == 1, "single-layer trunk"
    ab_re, ab_im, b_mat, c_mat = _s5_prep(ssm_a_re[0], ssm_a_im[0], ssm_log_dt[0], ssm_b_re[0], ssm_b_im[0],
                                          ssm_c_re[0], ssm_c_im[0])
    mixer_w = (norm_mix[0].reshape(1, D_MODEL), w_in[0].astype(_BF16), _pool_block_diag(pool_w[0]),
               pool_scale[0].reshape(1, POOL_WIDTH), ab_re, ab_im, b_mat, c_mat, ssm_d[0].reshape(1, SSM_WIDTH),
               glu_w[0].astype(_BF16), glu_b[0].reshape(1, SSM_WIDTH), w_branch_pool[0].astype(_BF16),
               w_branch_ssm[0].astype(_BF16), w_out[0].astype(_BF16))
    ffn_w = (norm_ffn[0].reshape(1, D_MODEL), ffn_w_gate[0].astype(_BF16), ffn_w_up[0].astype(_BF16),
             ffn_w_down[0].astype(_BF16), norm_final.reshape(1, D_MODEL))

    def run(x, hist, s_re, s_im, start_pos, batch_major):
        b, t, _ = x.shape
        hist0 = jnp.concatenate([jnp.zeros((1, b, POOL_WIDTH), _F32), hist.transpose(1, 0, 2)], axis=0)
        x_in = x if batch_major else x.transpose(1, 0, 2).reshape(t * b, D_MODEL)
        h_rows, hist_out, s_re_out, s_im_out = _mixer(
            x_in, hist0.reshape(HIST_STEPS * b, POOL_WIDTH), s_re.reshape(b, N_STATES), s_im.reshape(b, N_STATES),
            mixer_w, n_batch=b, n_steps_total=t, start_pos=start_pos, batch_major_in=batch_major)
        y = _ffn(h_rows, ffn_w, n_batch=b, n_steps_total=t, batch_major_out=batch_major)
        if not batch_major:
            y = y.reshape(t, b, D_MODEL).transpose(1, 0, 2)
        new_hist = hist_out.reshape(HIST_STEPS, b, POOL_WIDTH)[1:].transpose(1, 0, 2)
        shape = (1, b, SSM_GROUPS, SSM_STATE)
        return y, new_hist[None], s_re_out.reshape(shape), s_im_out.reshape(shape)

    bp_n = x_prompt.shape[0]
    zero_hist = jnp.zeros((bp_n, POOL_HIST, POOL_WIDTH), _F32)
    zero_ssm = jnp.zeros((bp_n, SSM_GROUPS, SSM_STATE), _F32)
    y_p, pool_p, re_p, im_p = run(x_prompt, zero_hist, zero_ssm, zero_ssm, 0, True)
    y_s, pool_s, re_s, im_s = run(x_sample, state_pool[0], state_ssm_re[0], state_ssm_im[0], PAST_LEN, False)
    return (y_p, y_s, pool_p, re_p, im_p, pool_s, re_s, im_s)
```

```python
import functools
import math

import jax
import jax.numpy as jnp
from jax import lax
from jax.experimental import pallas as pl
from jax.experimental.pallas import tpu as pltpu

D_MODEL = 1024
PAST_LEN = 16384
POOL_WIDTH = 512
POOL_WINDOWS = (2, 4, 8, 16)
POOL_GROUP_DIM = 128
POOL_HIST = 15
HIST_STEPS = POOL_HIST + 1
SSM_WIDTH = 512
SSM_GROUPS = 32
SSM_GROUP_DIM = 16
SSM_STATE = 64
N_STATES = SSM_GROUPS * SSM_STATE
D_FF = 2816
RMS_EPS = 1e-6

SUBLANES = 8
LANES = 128
MXU_DIM = 256
CHUNK_LANES = 2 * LANES
N_CHUNKS = N_STATES // LANES
HALF_CHUNKS = N_CHUNKS // 2
HALF_LANES = HALF_CHUNKS * CHUNK_LANES
HALF_ROWS = MXU_DIM
ROW_TILE = 512
FFN_SUB_TILES = 2
MERGE_CHUNK = 512
VMEM_LIMIT_BYTES = 56 * 1024 * 1024

_F32 = jnp.float32
_BF16 = jnp.bfloat16


def _rmsnorm(x, g):
    return x * lax.rsqrt(jnp.mean(x * x, axis=-1, keepdims=True) + RMS_EPS) * g


def _dot(a, b):
    return jnp.dot(a, b, preferred_element_type=_F32)


def _s5_prep_kernel(a_re_ref, a_im_ref, log_dt_ref, xb_ref, xc_ref,
                    ab_re_ref, ab_im_ref, b_mat_ref, c_mat_ref):
    a_re = a_re_ref[...]
    a_im = a_im_ref[...]
    dt = jnp.exp(log_dt_ref[...])
    mag = jnp.exp(a_re * dt)
    ab_re = mag * jnp.cos(a_im * dt)
    ab_im = mag * jnp.sin(a_im * dt)
    num_re = ab_re - 1.0
    num_im = ab_im
    den = a_re * a_re + a_im * a_im
    coef_re = (num_re * a_re + num_im * a_im) / den
    coef_im = (num_im * a_re - num_re * a_im) / den
    ab_re_ref[...] = ab_re
    ab_im_ref[...] = ab_im

    n_rows = SSM_GROUPS * SSM_GROUP_DIM
    cr = jnp.broadcast_to(coef_re, (SSM_GROUPS, SSM_GROUP_DIM, LANES)).reshape(n_rows, LANES)
    ci = jnp.broadcast_to(coef_im, (SSM_GROUPS, SSM_GROUP_DIM, LANES)).reshape(n_rows, LANES)
    b_re = xb_ref[0]
    b_im = xb_ref[1]
    bp = (cr * b_re - ci * b_im, cr * b_im + ci * b_re)
    row = lax.broadcasted_iota(jnp.int32, (HALF_ROWS, LANES), 0)
    lane = lax.broadcasted_iota(jnp.int32, (HALF_ROWS, LANES), 1)
    own_lanes = (lane >> 6) == ((row >> 4) & 1)
    for half in range(2):
        for part in range(2):
            mine = jnp.where(own_lanes, bp[part][half * HALF_ROWS:(half + 1) * HALF_ROWS, :], 0.0)
            for cl in range(HALF_CHUNKS):
                blk = jnp.where((row >> 5) == cl, mine, 0.0)
                lo = cl * CHUNK_LANES + part * LANES
                b_mat_ref[half, :, lo:lo + LANES] = blk.astype(_BF16)

    row_c = lax.broadcasted_iota(jnp.int32, (LANES, HALF_ROWS), 0)
    col_c = lax.broadcasted_iota(jnp.int32, (LANES, HALF_ROWS), 1)
    own_cols = (row_c >> 6) == ((col_c >> 4) & 1)
    for half in range(2):
        for part in range(2):
            w = xc_ref[part, :, half * HALF_ROWS:(half + 1) * HALF_ROWS]
            if part == 1:
                w = -w
            mine = jnp.where(own_cols, w, 0.0)
            for cl in range(HALF_CHUNKS):
                blk = jnp.where((col_c >> 5) == cl, mine, 0.0)
                lo = cl * CHUNK_LANES + part * LANES
                c_mat_ref[half, lo:lo + LANES, :] = blk.astype(_BF16)


def _s5_prep(a_re, a_im, log_dt, b_re, b_im, c_re, c_im):
    g, p, h = SSM_GROUPS, SSM_STATE, SSM_GROUP_DIM
    xb = jnp.stack([b_re, b_im]).transpose(0, 1, 3, 2)
    xb = jnp.concatenate([xb, xb], axis=-1).reshape(2, g * h, LANES)
    xc = jnp.stack([c_re, c_im]).transpose(0, 3, 1, 2).reshape(2, p, g * h)
    xc = jnp.concatenate([xc, xc], axis=1)
    twice = lambda a: jnp.concatenate([a, a], axis=-1).reshape(g, 1, LANES)
    return pl.pallas_call(
        _s5_prep_kernel,
        out_shape=(jax.ShapeDtypeStruct((g, 1, LANES), _F32), jax.ShapeDtypeStruct((g, 1, LANES), _F32),
                   jax.ShapeDtypeStruct((2, HALF_ROWS, HALF_LANES), _BF16),
                   jax.ShapeDtypeStruct((2, HALF_LANES, HALF_ROWS), _BF16)),
        name="s5_prep",
    )(twice(a_re), twice(a_im), log_dt.reshape(g, 1, 1), xb, xc)


def _scan_half(xs_ref, s_re_ref, s_im_ref, ab_re_ref, ab_im_ref, half, n_batch, n_steps):
    lane = lax.broadcasted_iota(jnp.int32, (1, LANES), 1)

    def coeff(ref, c):
        row = jnp.where(lane < SSM_STATE, ref[2 * c], ref[2 * c + 1])
        return jnp.broadcast_to(row, (SUBLANES, LANES))

    def recur(r0, row_of_step, cls):
        chunks = [half * HALF_CHUNKS + cl for cl in cls]
        ar = [coeff(ab_re_ref, c) for c in chunks]
        ai = [coeff(ab_im_ref, c) for c in chunks]
        sr = [s_re_ref[pl.ds(r0, SUBLANES), c * LANES:(c + 1) * LANES] for c in chunks]
        si = [s_im_ref[pl.ds(r0, SUBLANES), c * LANES:(c + 1) * LANES] for c in chunks]
        for t in range(n_steps):
            rows = pl.ds(row_of_step(t), SUBLANES)
            for k, cl in enumerate(cls):
                lo = cl * CHUNK_LANES
                nr = ar[k] * sr[k] - ai[k] * si[k] + xs_ref[rows, lo:lo + LANES]
                ni = ar[k] * si[k] + ai[k] * sr[k] + xs_ref[rows, lo + LANES:lo + CHUNK_LANES]
                xs_ref[rows, lo:lo + LANES] = nr
                xs_ref[rows, lo + LANES:lo + CHUNK_LANES] = ni
                sr[k], si[k] = nr, ni
        for k, c in enumerate(chunks):
            s_re_ref[pl.ds(r0, SUBLANES), c * LANES:(c + 1) * LANES] = sr[k]
            s_im_ref[pl.ds(r0, SUBLANES), c * LANES:(c + 1) * LANES] = si[k]

    if n_batch == SUBLANES:
        recur(0, lambda t: t * n_batch, list(range(HALF_CHUNKS)))
    else:
        group = 4
        for cg in range(0, HALF_CHUNKS, group):
            def row_block(rb, carry, cg=cg):
                r0 = pl.multiple_of(rb * SUBLANES, SUBLANES)
                recur(r0, lambda t: pl.multiple_of(t * n_batch + r0, SUBLANES), list(range(cg, cg + group)))
                return carry
            lax.fori_loop(0, n_batch // SUBLANES, row_block, 0)


def _mixer_kernel(*refs, n_batch, n_steps, n_tiles, start_pos, zero_state):
    n_state_in = 0 if zero_state else 3
    x_ref = refs[0]
    hist0_ref, s0_re_ref, s0_im_ref = refs[1:1 + n_state_in] if n_state_in else (None, None, None)
    (norm_ref, w_in_ref, pool_w_ref, pool_scale_ref, ab_re_ref, ab_im_ref, b_ref, c_ref, d_ref, glu_w_ref,
     glu_b_ref, w_bp_ref, w_bs_ref, w_out_ref,
     h_ref, hist_out_ref, s_re_ref, s_im_ref,
     hist_ref, x_tm_ref, xn_ref, proj_ref, ext_ref, xs_ref, merged_ref) = refs[1 + n_state_in:]
    i = pl.program_id(0)
    rows = n_batch * n_steps
    hist_rows = HIST_STEPS * n_batch

    @pl.when(i == 0)
    def _():
        if zero_state:
            hist_ref[...] = jnp.zeros(hist_ref.shape, _F32)
            s_re_ref[...] = jnp.zeros(s_re_ref.shape, _F32)
            s_im_ref[...] = jnp.zeros(s_im_ref.shape, _F32)
        else:
            hist_ref[0:n_batch, :] = jnp.zeros((n_batch, POOL_WIDTH), _F32)
            hist_ref[n_batch:, :] = jnp.swapaxes(hist0_ref[...], 0, 1).reshape(POOL_HIST * n_batch, POOL_WIDTH)
            s_re_ref[...] = s0_re_ref[...]
            s_im_ref[...] = s0_im_ref[...]

    x_tm_ref[...] = jnp.swapaxes(x_ref[...], 0, 1).reshape(rows, D_MODEL)
    xn_ref[...] = _rmsnorm(x_tm_ref[...], norm_ref[...]).astype(_BF16)
    u_width = POOL_WIDTH + SSM_WIDTH
    proj_ref[...] = _dot(xn_ref[...], w_in_ref[:, 0:u_width])

    ext_ref[0:hist_rows, :] = hist_ref[...]
    ext_ref[hist_rows:, :] = proj_ref[:, 0:POOL_WIDTH]
    hist_ref[...] = ext_ref[rows:rows + hist_rows, :]
    diffs = []
    for g, w in enumerate(POOL_WINDOWS):
        lo = g * POOL_GROUP_DIM
        u_g = ext_ref[hist_rows:, lo:lo + POOL_GROUP_DIM]
        acc = u_g
        for j in range(1, w):
            r0 = (HIST_STEPS - j) * n_batch
            acc = acc + ext_ref[r0:r0 + rows, lo:lo + POOL_GROUP_DIM]
        if start_pos + 1 >= w:
            pooled = acc * (1.0 / w)
        else:
            row = lax.broadcasted_iota(jnp.int32, (rows, POOL_GROUP_DIM), 0)
            step = lax.shift_right_logical(row, int(math.log2(n_batch)))
            pos = start_pos + i * n_steps + step
            pooled = acc / jnp.minimum(w, pos + 1).astype(_F32)
        diffs.append(pooled - u_g)
    mixed = jnp.concatenate(
        [_dot(jnp.concatenate(diffs[0:2], axis=1).astype(_BF16), pool_w_ref[0]),
         _dot(jnp.concatenate(diffs[2:4], axis=1).astype(_BF16), pool_w_ref[1])], axis=1)
    a_out = (mixed * pool_scale_ref[...]).astype(_BF16)

    u_ssm = proj_ref[:, POOL_WIDTH:u_width]
    u_bf = u_ssm.astype(_BF16)
    for half in range(2):
        xs_ref[half] = _dot(u_bf[:, half * HALF_ROWS:(half + 1) * HALF_ROWS], b_ref[half])
    for half in range(2):
        _scan_half(xs_ref.at[half], s_re_ref, s_im_ref, ab_re_ref, ab_im_ref, half, n_batch, n_steps)
    ys = [_dot(xs_ref[half].astype(_BF16), c_ref[half]) for half in range(2)]
    y_ssm = jnp.concatenate(ys, axis=1) + d_ref[...] * u_ssm
    z = jax.nn.gelu(y_ssm)
    glu = jax.nn.sigmoid(_dot(z.astype(_BF16), glu_w_ref[...]) + glu_b_ref[...])
    b_out = (z * glu).astype(_BF16)

    for lo in range(0, D_MODEL, MERGE_CHUNK):
        cols = slice(lo, lo + MERGE_CHUNK)
        gate_pool = jax.nn.sigmoid(_dot(xn_ref[...], w_in_ref[:, u_width + lo:u_width + lo + MERGE_CHUNK]))
        gate_ssm = jax.nn.sigmoid(
            _dot(xn_ref[...], w_in_ref[:, u_width + D_MODEL + lo:u_width + D_MODEL + lo + MERGE_CHUNK]))
        merged = gate_pool * _dot(a_out, w_bp_ref[:, cols]) + gate_ssm * _dot(b_out, w_bs_ref[:, cols])
        merged_ref[:, cols] = merged.astype(_BF16)
    h_ref[...] = x_tm_ref[...] + _dot(merged_ref[...], w_out_ref[...])

    @pl.when(i == n_tiles - 1)
    def _():
        newest = hist_ref[n_batch:, :].reshape(POOL_HIST, n_batch, POOL_WIDTH)
        hist_out_ref[...] = jnp.swapaxes(newest, 0, 1)


def _resident(shape):
    zeros = (0,) * len(shape)
    return pl.BlockSpec(shape, lambda i: zeros, pipeline_mode=pl.Buffered(1))


def _tiling(n_batch, n_steps_total, rows_per_step=ROW_TILE):
    n_steps = min(n_steps_total, rows_per_step // n_batch)
    n_tiles = n_steps_total // n_steps
    assert n_tiles * n_steps == n_steps_total and n_batch % SUBLANES == 0
    assert n_batch & (n_batch - 1) == 0
    return n_steps, n_tiles


def _mixer(x, state, weights, *, start_pos):
    n_batch, n_steps_total, _ = x.shape
    n_steps, n_tiles = _tiling(n_batch, n_steps_total)
    rows = n_steps * n_batch
    hist_rows = HIST_STEPS * n_batch
    state = () if state is None else tuple(state)
    kern = functools.partial(_mixer_kernel, n_batch=n_batch, n_steps=n_steps, n_tiles=n_tiles,
                             start_pos=start_pos, zero_state=not state)
    state_block = pl.BlockSpec((n_batch, N_STATES), lambda i: (0, 0))
    return pl.pallas_call(
        kern,
        grid=(n_tiles,),
        in_specs=[pl.BlockSpec((n_batch, n_steps, D_MODEL), lambda i: (0, i, 0))]
        + [_resident(a.shape) for a in state + tuple(weights)],
        out_specs=(pl.BlockSpec((rows, D_MODEL), lambda i: (i, 0)),
                   pl.BlockSpec((n_batch, POOL_HIST, POOL_WIDTH), lambda i: (0, 0, 0)),
                   state_block, state_block),
        out_shape=(jax.ShapeDtypeStruct((n_steps_total * n_batch, D_MODEL), _F32),
                   jax.ShapeDtypeStruct((n_batch, POOL_HIST, POOL_WIDTH), _F32),
                   jax.ShapeDtypeStruct((n_batch, N_STATES), _F32),
                   jax.ShapeDtypeStruct((n_batch, N_STATES), _F32)),
        scratch_shapes=[pltpu.VMEM((hist_rows, POOL_WIDTH), _F32),
                        pltpu.VMEM((rows, D_MODEL), _F32),
                        pltpu.VMEM((rows, D_MODEL), _BF16),
                        pltpu.VMEM((rows, POOL_WIDTH + SSM_WIDTH), _F32),
                        pltpu.VMEM((hist_rows + rows, POOL_WIDTH), _F32),
                        pltpu.VMEM((2, rows, HALF_LANES), _F32),
                        pltpu.VMEM((rows, D_MODEL), _BF16)],
        compiler_params=pltpu.CompilerParams(dimension_semantics=("arbitrary",),
                                             vmem_limit_bytes=VMEM_LIMIT_BYTES),
        name="mixer",
    )(x, *state, *weights)


def _ffn_kernel(h_ref, norm_ffn_ref, w_gate_ref, w_up_ref, w_down_ref, norm_final_ref, y_ref,
                gate_ref, up_ref, *, n_sub):
    n_batch = y_ref.shape[0]
    sub_rows = h_ref.shape[0] // n_sub
    sub_steps = sub_rows // n_batch
    for s in range(n_sub):
        h = h_ref[s * sub_rows:(s + 1) * sub_rows, :]
        hn = _rmsnorm(h, norm_ffn_ref[...]).astype(_BF16)
        gate_ref[...] = _dot(hn, w_gate_ref[...])
        up_ref[...] = _dot(hn, w_up_ref[...])
        gate = gate_ref[...]
        f = (gate * jax.nn.sigmoid(gate) * up_ref[...]).astype(_BF16)
        y = _rmsnorm(h + _dot(f, w_down_ref[...]), norm_final_ref[...])
        y_ref[:, s * sub_steps:(s + 1) * sub_steps, :] = jnp.swapaxes(y.reshape(sub_steps, n_batch, D_MODEL), 0, 1)


def _ffn(h_rows, weights, *, n_batch):
    n_steps_total = h_rows.shape[0] // n_batch
    n_sub = max(1, min(FFN_SUB_TILES, h_rows.shape[0] // ROW_TILE))
    n_steps, n_tiles = _tiling(n_batch, n_steps_total, ROW_TILE * n_sub)
    rows = n_steps * n_batch
    return pl.pallas_call(
        functools.partial(_ffn_kernel, n_sub=n_sub),
        grid=(n_tiles,),
        in_specs=[pl.BlockSpec((rows, D_MODEL), lambda i: (i, 0))] + [_resident(w.shape) for w in weights],
        out_specs=pl.BlockSpec((n_batch, n_steps, D_MODEL), lambda i: (0, i, 0)),
        out_shape=jax.ShapeDtypeStruct((n_batch, n_steps_total, D_MODEL), _F32),
        scratch_shapes=[pltpu.VMEM((rows // n_sub, D_FF), _F32), pltpu.VMEM((rows // n_sub, D_FF), _F32)],
        compiler_params=pltpu.CompilerParams(dimension_semantics=("arbitrary",),
                                             vmem_limit_bytes=VMEM_LIMIT_BYTES),
        name="ffn",
    )(h_rows, *weights)


def _pool_block_diag(pool_w):
    z = jnp.zeros((POOL_GROUP_DIM, POOL_GROUP_DIM), pool_w.dtype)
    blocks = [jnp.block([[pool_w[2 * k], z], [z, pool_w[2 * k + 1]]]) for k in range(2)]
    return jnp.stack(blocks).astype(_BF16)


def kernel(x_prompt, x_sample, state_pool, state_ssm_re, state_ssm_im, norm_mix, w_in, pool_w, pool_scale, ssm_a_re, ssm_a_im, ssm_log_dt, ssm_b_re, ssm_b_im, ssm_c_re, ssm_c_im, ssm_d, glu_w, glu_b, w_branch_pool, w_branch_ssm, w_out, norm_ffn, ffn_w_gate, ffn_w_up, ffn_w_down, norm_final):
    assert w_in.shape[0] == 1, "single-layer trunk"
    ab_re, ab_im, b_mat, c_mat = _s5_prep(ssm_a_re[0], ssm_a_im[0], ssm_log_dt[0], ssm_b_re[0], ssm_b_im[0],
                                          ssm_c_re[0], ssm_c_im[0])
    mixer_w = (norm_mix[0].reshape(1, D_MODEL), w_in[0].astype(_BF16), _pool_block_diag(pool_w[0]),
               pool_scale[0].reshape(1, POOL_WIDTH), ab_re, ab_im, b_mat, c_mat, ssm_d[0].reshape(1, SSM_WIDTH),
               glu_w[0].astype(_BF16), glu_b[0].reshape(1, SSM_WIDTH), w_branch_pool[0].astype(_BF16),
               w_branch_ssm[0].astype(_BF16), w_out[0].astype(_BF16))
    ffn_w = (norm_ffn[0].reshape(1, D_MODEL), ffn_w_gate[0].astype(_BF16), ffn_w_up[0].astype(_BF16),
             ffn_w_down[0].astype(_BF16), norm_final.reshape(1, D_MODEL))

    def run(x, state, start_pos):
        b = x.shape[0]
        h_rows, new_hist, s_re, s_im = _mixer(x, state, mixer_w, start_pos=start_pos)
        y = _ffn(h_rows, ffn_w, n_batch=b)
        shape = (1, b, SSM_GROUPS, SSM_STATE)
        return y, new_hist[None], s_re.reshape(shape), s_im.reshape(shape)

    y_p, pool_p, re_p, im_p = run(x_prompt, None, 0)
    bs_n = x_sample.shape[0]
    sample_state = (state_pool[0], state_ssm_re[0].reshape(bs_n, N_STATES), state_ssm_im[0].reshape(bs_n, N_STATES))
    y_s, pool_s, re_s, im_s = run(x_sample, sample_state, PAST_LEN)
    return (y_p, y_s, pool_p, re_p, im_p, pool_s, re_s, im_s)
```

```python
import functools
import math

import jax
import jax.numpy as jnp
from jax import lax
from jax.experimental import pallas as pl
from jax.experimental.pallas import tpu as pltpu

D_MODEL = 1024
PAST_LEN = 16384
POOL_WIDTH = 512
POOL_WINDOWS = (2, 4, 8, 16)
POOL_GROUP_DIM = 128
POOL_HIST = 15
HIST_STEPS = POOL_HIST + 1
SSM_WIDTH = 512
SSM_GROUPS = 32
SSM_GROUP_DIM = 16
SSM_STATE = 64
N_STATES = SSM_GROUPS * SSM_STATE
IN_WIDTH = POOL_WIDTH + SSM_WIDTH + 2 * D_MODEL
D_FF = 2816
RMS_EPS = 1e-6

SUBLANES = 8
LANES = 128
MXU_DIM = 256
CHUNK_LANES = 2 * LANES
N_CHUNKS = N_STATES // LANES
HALF_CHUNKS = N_CHUNKS // 2
HALF_LANES = HALF_CHUNKS * CHUNK_LANES
HALF_ROWS = MXU_DIM
ROW_TILE = 512
FFN_SUB_TILES = 2
VMEM_LIMIT_BYTES = 56 * 1024 * 1024

_F32 = jnp.float32
_BF16 = jnp.bfloat16


def _rmsnorm(x, g):
    return x * lax.rsqrt(jnp.mean(x * x, axis=-1, keepdims=True) + RMS_EPS) * g


def _dot(a, b):
    return jnp.dot(a, b, preferred_element_type=_F32)


def _s5_prep_kernel(a_re_ref, a_im_ref, log_dt_ref, xb_ref, xc_ref,
                    ab_re_ref, ab_im_ref, b_mat_ref, c_mat_ref):
    a_re = a_re_ref[...]
    a_im = a_im_ref[...]
    dt = jnp.exp(log_dt_ref[...])
    mag = jnp.exp(a_re * dt)
    ab_re = mag * jnp.cos(a_im * dt)
    ab_im = mag * jnp.sin(a_im * dt)
    num_re = ab_re - 1.0
    num_im = ab_im
    den = a_re * a_re + a_im * a_im
    coef_re = (num_re * a_re + num_im * a_im) / den
    coef_im = (num_im * a_re - num_re * a_im) / den
    ab_re_ref[...] = ab_re
    ab_im_ref[...] = ab_im

    n_rows = SSM_GROUPS * SSM_GROUP_DIM
    cr = jnp.broadcast_to(coef_re, (SSM_GROUPS, SSM_GROUP_DIM, LANES)).reshape(n_rows, LANES)
    ci = jnp.broadcast_to(coef_im, (SSM_GROUPS, SSM_GROUP_DIM, LANES)).reshape(n_rows, LANES)
    b_re = xb_ref[0]
    b_im = xb_ref[1]
    bp = (cr * b_re - ci * b_im, cr * b_im + ci * b_re)
    row = lax.broadcasted_iota(jnp.int32, (HALF_ROWS, LANES), 0)
    lane = lax.broadcasted_iota(jnp.int32, (HALF_ROWS, LANES), 1)
    own_lanes = (lane >> 6) == ((row >> 4) & 1)
    for half in range(2):
        for part in range(2):
            mine = jnp.where(own_lanes, bp[part][half * HALF_ROWS:(half + 1) * HALF_ROWS, :], 0.0)
            for cl in range(HALF_CHUNKS):
                blk = jnp.where((row >> 5) == cl, mine, 0.0)
                lo = cl * CHUNK_LANES + part * LANES
                b_mat_ref[half, :, lo:lo + LANES] = blk.astype(_BF16)

    row_c = lax.broadcasted_iota(jnp.int32, (LANES, HALF_ROWS), 0)
    col_c = lax.broadcasted_iota(jnp.int32, (LANES, HALF_ROWS), 1)
    own_cols = (row_c >> 6) == ((col_c >> 4) & 1)
    for half in range(2):
        for part in range(2):
            w = xc_ref[part, :, half * HALF_ROWS:(half + 1) * HALF_ROWS]
            if part == 1:
                w = -w
            mine = jnp.where(own_cols, w, 0.0)
            for cl in range(HALF_CHUNKS):
                blk = jnp.where((col_c >> 5) == cl, mine, 0.0)
                lo = cl * CHUNK_LANES + part * LANES
                c_mat_ref[half, lo:lo + LANES, :] = blk.astype(_BF16)


def _s5_prep(a_re, a_im, log_dt, b_re, b_im, c_re, c_im):
    g, p, h = SSM_GROUPS, SSM_STATE, SSM_GROUP_DIM
    xb = jnp.stack([b_re, b_im]).transpose(0, 1, 3, 2)
    xb = jnp.concatenate([xb, xb], axis=-1).reshape(2, g * h, LANES)
    xc = jnp.stack([c_re, c_im]).transpose(0, 3, 1, 2).reshape(2, p, g * h)
    xc = jnp.concatenate([xc, xc], axis=1)
    twice = lambda a: jnp.concatenate([a, a], axis=-1).reshape(g, 1, LANES)
    return pl.pallas_call(
        _s5_prep_kernel,
        out_shape=(jax.ShapeDtypeStruct((g, 1, LANES), _F32), jax.ShapeDtypeStruct((g, 1, LANES), _F32),
                   jax.ShapeDtypeStruct((2, HALF_ROWS, HALF_LANES), _BF16),
                   jax.ShapeDtypeStruct((2, HALF_LANES, HALF_ROWS), _BF16)),
        name="s5_prep",
    )(twice(a_re), twice(a_im), log_dt.reshape(g, 1, 1), xb, xc)


def _scan_half(xs_ref, s_re_ref, s_im_ref, ab_re_ref, ab_im_ref, half, n_batch, n_steps):
    lane = lax.broadcasted_iota(jnp.int32, (1, LANES), 1)

    def coeff(ref, c):
        row = jnp.where(lane < SSM_STATE, ref[2 * c], ref[2 * c + 1])
        return jnp.broadcast_to(row, (SUBLANES, LANES))

    def recur(r0, row_of_step, cls):
        chunks = [half * HALF_CHUNKS + cl for cl in cls]
        ar = [coeff(ab_re_ref, c) for c in chunks]
        ai = [coeff(ab_im_ref, c) for c in chunks]
        sr = [s_re_ref[pl.ds(r0, SUBLANES), c * LANES:(c + 1) * LANES] for c in chunks]
        si = [s_im_ref[pl.ds(r0, SUBLANES), c * LANES:(c + 1) * LANES] for c in chunks]
        for t in range(n_steps):
            rows = pl.ds(row_of_step(t), SUBLANES)
            for k, cl in enumerate(cls):
                lo = cl * CHUNK_LANES
                nr = ar[k] * sr[k] - ai[k] * si[k] + xs_ref[rows, lo:lo + LANES]
                ni = ar[k] * si[k] + ai[k] * sr[k] + xs_ref[rows, lo + LANES:lo + CHUNK_LANES]
                xs_ref[rows, lo:lo + LANES] = nr
                xs_ref[rows, lo + LANES:lo + CHUNK_LANES] = ni
                sr[k], si[k] = nr, ni
        for k, c in enumerate(chunks):
            s_re_ref[pl.ds(r0, SUBLANES), c * LANES:(c + 1) * LANES] = sr[k]
            s_im_ref[pl.ds(r0, SUBLANES), c * LANES:(c + 1) * LANES] = si[k]

    if n_batch == SUBLANES:
        recur(0, lambda t: t * n_batch, list(range(HALF_CHUNKS)))
    else:
        group = 4
        for cg in range(0, HALF_CHUNKS, group):
            def row_block(rb, carry, cg=cg):
                r0 = pl.multiple_of(rb * SUBLANES, SUBLANES)
                recur(r0, lambda t: pl.multiple_of(t * n_batch + r0, SUBLANES), list(range(cg, cg + group)))
                return carry
            lax.fori_loop(0, n_batch // SUBLANES, row_block, 0)


def _mixer_kernel(*refs, n_batch, n_steps, n_tiles, start_pos, zero_state):
    n_state_in = 0 if zero_state else 3
    x_ref = refs[0]
    hist0_ref, s0_re_ref, s0_im_ref = refs[1:1 + n_state_in] if n_state_in else (None, None, None)
    (norm_ref, w_in_ref, pool_w_ref, pool_scale_ref, ab_re_ref, ab_im_ref, b_ref, c_ref, d_ref, glu_w_ref,
     glu_b_ref, w_bp_ref, w_bs_ref, w_out_ref,
     h_ref, hist_out_ref, s_re_ref, s_im_ref,
     hist_ref, x_tm_ref, proj_ref, ext_ref, xs_ref) = refs[1 + n_state_in:]
    i = pl.program_id(0)
    rows = n_batch * n_steps
    hist_rows = HIST_STEPS * n_batch

    @pl.when(i == 0)
    def _():
        if zero_state:
            hist_ref[...] = jnp.zeros(hist_ref.shape, _F32)
            s_re_ref[...] = jnp.zeros(s_re_ref.shape, _F32)
            s_im_ref[...] = jnp.zeros(s_im_ref.shape, _F32)
        else:
            hist_ref[0:n_batch, :] = jnp.zeros((n_batch, POOL_WIDTH), _F32)
            hist_ref[n_batch:, :] = hist0_ref[...].reshape(POOL_HIST * n_batch, POOL_WIDTH)
            s_re_ref[...] = s0_re_ref[...]
            s_im_ref[...] = s0_im_ref[...]

    x_tm_ref[...] = jnp.swapaxes(x_ref[...], 0, 1).reshape(rows, D_MODEL)
    xn = _rmsnorm(x_tm_ref[...], norm_ref[...]).astype(_BF16)
    u_width = POOL_WIDTH + SSM_WIDTH
    proj_ref[...] = _dot(xn, w_in_ref[...])

    ext_ref[0:hist_rows, :] = hist_ref[...]
    ext_ref[hist_rows:, :] = proj_ref[:, 0:POOL_WIDTH]
    hist_ref[...] = ext_ref[rows:rows + hist_rows, :]
    diffs = []
    for g, w in enumerate(POOL_WINDOWS):
        lo = g * POOL_GROUP_DIM
        u_g = ext_ref[hist_rows:, lo:lo + POOL_GROUP_DIM]
        acc = u_g
        for j in range(1, w):
            r0 = (HIST_STEPS - j) * n_batch
            acc = acc + ext_ref[r0:r0 + rows, lo:lo + POOL_GROUP_DIM]
        if start_pos + 1 >= w:
            pooled = acc * (1.0 / w)
        else:
            row = lax.broadcasted_iota(jnp.int32, (rows, POOL_GROUP_DIM), 0)
            step = lax.shift_right_logical(row, int(math.log2(n_batch)))
            pos = start_pos + i * n_steps + step
            pooled = acc / jnp.minimum(w, pos + 1).astype(_F32)
        diffs.append(pooled - u_g)
    mixed = jnp.concatenate(
        [_dot(jnp.concatenate(diffs[0:2], axis=1).astype(_BF16), pool_w_ref[0]),
         _dot(jnp.concatenate(diffs[2:4], axis=1).astype(_BF16), pool_w_ref[1])], axis=1)
    a_out = (mixed * pool_scale_ref[...]).astype(_BF16)

    u_ssm = proj_ref[:, POOL_WIDTH:u_width]
    u_bf = u_ssm.astype(_BF16)
    for half in range(2):
        xs_ref[half] = _dot(u_bf[:, half * HALF_ROWS:(half + 1) * HALF_ROWS], b_ref[half])
    for half in range(2):
        _scan_half(xs_ref.at[half], s_re_ref, s_im_ref, ab_re_ref, ab_im_ref, half, n_batch, n_steps)
    ys = [_dot(xs_ref[half].astype(_BF16), c_ref[half]) for half in range(2)]
    y_ssm = jnp.concatenate(ys, axis=1) + d_ref[...] * u_ssm
    z = jax.nn.gelu(y_ssm)
    glu = jax.nn.sigmoid(_dot(z.astype(_BF16), glu_w_ref[...]) + glu_b_ref[...])
    b_out = (z * glu).astype(_BF16)

    merged = (jax.nn.sigmoid(proj_ref[:, u_width:u_width + D_MODEL]) * _dot(a_out, w_bp_ref[...])
              + jax.nn.sigmoid(proj_ref[:, u_width + D_MODEL:u_width + 2 * D_MODEL]) * _dot(b_out, w_bs_ref[...]))
    h_ref[...] = x_tm_ref[...] + _dot(merged.astype(_BF16), w_out_ref[...])

    @pl.when(i == n_tiles - 1)
    def _():
        hist_out_ref[...] = hist_ref[n_batch:, :].reshape(POOL_HIST, n_batch, POOL_WIDTH)


def _resident(shape):
    zeros = (0,) * len(shape)
    return pl.BlockSpec(shape, lambda i: zeros, pipeline_mode=pl.Buffered(1))


def _tiling(n_batch, n_steps_total, rows_per_step=ROW_TILE):
    n_steps = min(n_steps_total, rows_per_step // n_batch)
    n_tiles = n_steps_total // n_steps
    assert n_tiles * n_steps == n_steps_total and n_batch % SUBLANES == 0
    assert n_batch & (n_batch - 1) == 0
    return n_steps, n_tiles


def _mixer(x, state, weights, *, start_pos):
    n_batch, n_steps_total, _ = x.shape
    n_steps, n_tiles = _tiling(n_batch, n_steps_total)
    rows = n_steps * n_batch
    hist_rows = HIST_STEPS * n_batch
    state = () if state is None else tuple(state)
    kern = functools.partial(_mixer_kernel, n_batch=n_batch, n_steps=n_steps, n_tiles=n_tiles,
                             start_pos=start_pos, zero_state=not state)
    state_block = pl.BlockSpec((n_batch, N_STATES), lambda i: (0, 0))
    return pl.pallas_call(
        kern,
        grid=(n_tiles,),
        in_specs=[pl.BlockSpec((n_batch, n_steps, D_MODEL), lambda i: (0, i, 0))]
        + [_resident(a.shape) for a in state + tuple(weights)],
        out_specs=(pl.BlockSpec((rows, D_MODEL), lambda i: (i, 0)),
                   pl.BlockSpec((POOL_HIST, n_batch, POOL_WIDTH), lambda i: (0, 0, 0)),
                   state_block, state_block),
        out_shape=(jax.ShapeDtypeStruct((n_steps_total * n_batch, D_MODEL), _F32),
                   jax.ShapeDtypeStruct((POOL_HIST, n_batch, POOL_WIDTH), _F32),
                   jax.ShapeDtypeStruct((n_batch, N_STATES), _F32),
                   jax.ShapeDtypeStruct((n_batch, N_STATES), _F32)),
        scratch_shapes=[pltpu.VMEM((hist_rows, POOL_WIDTH), _F32),
                        pltpu.VMEM((rows, D_MODEL), _F32),
                        pltpu.VMEM((rows, IN_WIDTH), _F32),
                        pltpu.VMEM((hist_rows + rows, POOL_WIDTH), _F32),
                        pltpu.VMEM((2, rows, HALF_LANES), _F32)],
        compiler_params=pltpu.CompilerParams(dimension_semantics=("arbitrary",),
                                             vmem_limit_bytes=VMEM_LIMIT_BYTES),
        name="mixer",
    )(x, *state, *weights)


def _ffn_kernel(h_ref, norm_ffn_ref, w_gate_ref, w_up_ref, w_down_ref, norm_final_ref, y_ref,
                gate_ref, up_ref, *, n_sub):
    n_batch = y_ref.shape[0]
    sub_rows = h_ref.shape[0] // n_sub
    sub_steps = sub_rows // n_batch
    for s in range(n_sub):
        h = h_ref[s * sub_rows:(s + 1) * sub_rows, :]
        hn = _rmsnorm(h, norm_ffn_ref[...]).astype(_BF16)
        gate_ref[...] = _dot(hn, w_gate_ref[...])
        up_ref[...] = _dot(hn, w_up_ref[...])
        gate = gate_ref[...]
        f = (gate * jax.nn.sigmoid(gate) * up_ref[...]).astype(_BF16)
        y = _rmsnorm(h + _dot(f, w_down_ref[...]), norm_final_ref[...])
        y_ref[:, s * sub_steps:(s + 1) * sub_steps, :] = jnp.swapaxes(y.reshape(sub_steps, n_batch, D_MODEL), 0, 1)


def _ffn(h_rows, weights, *, n_batch):
    n_steps_total = h_rows.shape[0] // n_batch
    n_sub = max(1, min(FFN_SUB_TILES, h_rows.shape[0] // ROW_TILE))
    n_steps, n_tiles = _tiling(n_batch, n_steps_total, ROW_TILE * n_sub)
    rows = n_steps * n_batch
    return pl.pallas_call(
        functools.partial(_ffn_kernel, n_sub=n_sub),
        grid=(n_tiles,),
        in_specs=[pl.BlockSpec((rows, D_MODEL), lambda i: (i, 0))] + [_resident(w.shape) for w in weights],
        out_specs=pl.BlockSpec((n_batch, n_steps, D_MODEL), lambda i: (0, i, 0)),
        out_shape=jax.ShapeDtypeStruct((n_batch, n_steps_total, D_MODEL), _F32),
        scratch_shapes=[pltpu.VMEM((rows // n_sub, D_FF), _F32), pltpu.VMEM((rows // n_sub, D_FF), _F32)],
        compiler_params=pltpu.CompilerParams(dimension_semantics=("arbitrary",),
                                             vmem_limit_bytes=VMEM_LIMIT_BYTES),
        name="ffn",
    )(h_rows, *weights)


def _pool_block_diag(pool_w):
    z = jnp.zeros((POOL_GROUP_DIM, POOL_GROUP_DIM), pool_w.dtype)
    blocks = [jnp.block([[pool_w[2 * k], z], [z, pool_w[2 * k + 1]]]) for k in range(2)]
    return jnp.stack(blocks).astype(_BF16)


def kernel(x_prompt, x_sample, state_pool, state_ssm_re, state_ssm_im, norm_mix, w_in, pool_w, pool_scale, ssm_a_re, ssm_a_im, ssm_log_dt, ssm_b_re, ssm_b_im, ssm_c_re, ssm_c_im, ssm_d, glu_w, glu_b, w_branch_pool, w_branch_ssm, w_out, norm_ffn, ffn_w_gate, ffn_w_up, ffn_w_down, norm_final):
    assert w_in.shape[0] == 1, "single-layer trunk"
    ab_re, ab_im, b_mat, c_mat = _s5_prep(ssm_a_re[0], ssm_a_im[0], ssm_log_dt[0], ssm_b_re[0], ssm_b_im[0],
                                          ssm_c_re[0], ssm_c_im[0])
    mixer_w = (norm_mix[0].reshape(1, D_MODEL), w_in[0].astype(_BF16), _pool_block_diag(pool_w[0]),
               pool_scale[0].reshape(1, POOL_WIDTH), ab_re, ab_im, b_mat, c_mat, ssm_d[0].reshape(1, SSM_WIDTH),
               glu_w[0].astype(_BF16), glu_b[0].reshape(1, SSM_WIDTH), w_branch_pool[0].astype(_BF16),
               w_branch_ssm[0].astype(_BF16), w_out[0].astype(_BF16))
    ffn_w = (norm_ffn[0].reshape(1, D_MODEL), ffn_w_gate[0].astype(_BF16), ffn_w_up[0].astype(_BF16),
             ffn_w_down[0].astype(_BF16), norm_final.reshape(1, D_MODEL))

    def run(x, state, start_pos):
        b = x.shape[0]
        h_rows, new_hist, s_re, s_im = _mixer(x, state, mixer_w, start_pos=start_pos)
        y = _ffn(h_rows, ffn_w, n_batch=b)
        shape = (1, b, SSM_GROUPS, SSM_STATE)
        return y, new_hist.transpose(1, 0, 2)[None], s_re.reshape(shape), s_im.reshape(shape)

    y_p, pool_p, re_p, im_p = run(x_prompt, None, 0)
    bs_n = x_sample.shape[0]
    sample_state = (state_pool[0].transpose(1, 0, 2), state_ssm_re[0].reshape(bs_n, N_STATES), state_ssm_im[0].reshape(bs_n, N_STATES))
    y_s, pool_s, re_s, im_s = run(x_sample, sample_state, PAST_LEN)
    return (y_p, y_s, pool_p, re_p, im_p, pool_s, re_s, im_s)
```

```python
import functools
import math

import jax
import jax.numpy as jnp
from jax import lax
from jax.experimental import pallas as pl
from jax.experimental.pallas import tpu as pltpu

D_MODEL = 1024
PAST_LEN = 16384
POOL_WIDTH = 512
POOL_WINDOWS = (2, 4, 8, 16)
POOL_GROUP_DIM = 128
POOL_HIST = 15
HIST_STEPS = POOL_HIST + 1
SSM_WIDTH = 512
SSM_GROUPS = 32
SSM_GROUP_DIM = 16
SSM_STATE = 64
N_STATES = SSM_GROUPS * SSM_STATE
IN_WIDTH = POOL_WIDTH + SSM_WIDTH + 2 * D_MODEL
D_FF = 2816
RMS_EPS = 1e-6

SUBLANES = 8
LANES = 128
MXU_DIM = 256
CHUNK_LANES = 2 * LANES
N_CHUNKS = N_STATES // LANES
HALF_CHUNKS = N_CHUNKS // 2
HALF_LANES = HALF_CHUNKS * CHUNK_LANES
HALF_ROWS = MXU_DIM
ROW_TILE = 512
FFN_SUB_TILES = 2
VMEM_LIMIT_BYTES = 60 * 1024 * 1024

_F32 = jnp.float32
_BF16 = jnp.bfloat16


def _rmsnorm(x, g):
    return x * lax.rsqrt(jnp.mean(x * x, axis=-1, keepdims=True) + RMS_EPS) * g


def _dot(a, b):
    return jnp.dot(a, b, preferred_element_type=_F32)


def _s5_prep_kernel(a_re_ref, a_im_ref, log_dt_ref, xb_ref, xc_ref,
                    ab_re_ref, ab_im_ref, b_mat_ref, c_mat_ref):
    a_re = a_re_ref[...]
    a_im = a_im_ref[...]
    dt = jnp.exp(log_dt_ref[...])
    mag = jnp.exp(a_re * dt)
    ab_re = mag * jnp.cos(a_im * dt)
    ab_im = mag * jnp.sin(a_im * dt)
    num_re = ab_re - 1.0
    num_im = ab_im
    den = a_re * a_re + a_im * a_im
    coef_re = (num_re * a_re + num_im * a_im) / den
    coef_im = (num_im * a_re - num_re * a_im) / den
    ab_re_ref[...] = ab_re
    ab_im_ref[...] = ab_im

    n_rows = SSM_GROUPS * SSM_GROUP_DIM
    cr = jnp.broadcast_to(coef_re, (SSM_GROUPS, SSM_GROUP_DIM, LANES)).reshape(n_rows, LANES)
    ci = jnp.broadcast_to(coef_im, (SSM_GROUPS, SSM_GROUP_DIM, LANES)).reshape(n_rows, LANES)
    b_re = xb_ref[0]
    b_im = xb_ref[1]
    bp = (cr * b_re - ci * b_im, cr * b_im + ci * b_re)
    row = lax.broadcasted_iota(jnp.int32, (HALF_ROWS, LANES), 0)
    lane = lax.broadcasted_iota(jnp.int32, (HALF_ROWS, LANES), 1)
    own_lanes = (lane >> 6) == ((row >> 4) & 1)
    for half in range(2):
        for part in range(2):
            mine = jnp.where(own_lanes, bp[part][half * HALF_ROWS:(half + 1) * HALF_ROWS, :], 0.0)
            for cl in range(HALF_CHUNKS):
                blk = jnp.where((row >> 5) == cl, mine, 0.0)
                lo = cl * CHUNK_LANES + part * LANES
                b_mat_ref[half, :, lo:lo + LANES] = blk.astype(_BF16)

    row_c = lax.broadcasted_iota(jnp.int32, (LANES, HALF_ROWS), 0)
    col_c = lax.broadcasted_iota(jnp.int32, (LANES, HALF_ROWS), 1)
    own_cols = (row_c >> 6) == ((col_c >> 4) & 1)
    for half in range(2):
        for part in range(2):
            w = xc_ref[part, :, half * HALF_ROWS:(half + 1) * HALF_ROWS]
            if part == 1:
                w = -w
            mine = jnp.where(own_cols, w, 0.0)
            for cl in range(HALF_CHUNKS):
                blk = jnp.where((col_c >> 5) == cl, mine, 0.0)
                lo = cl * CHUNK_LANES + part * LANES
                c_mat_ref[half, lo:lo + LANES, :] = blk.astype(_BF16)


def _s5_prep(a_re, a_im, log_dt, b_re, b_im, c_re, c_im):
    g, p, h = SSM_GROUPS, SSM_STATE, SSM_GROUP_DIM
    xb = jnp.stack([b_re, b_im]).transpose(0, 1, 3, 2)
    xb = jnp.concatenate([xb, xb], axis=-1).reshape(2, g * h, LANES)
    xc = jnp.stack([c_re, c_im]).transpose(0, 3, 1, 2).reshape(2, p, g * h)
    xc = jnp.concatenate([xc, xc], axis=1)
    twice = lambda a: jnp.concatenate([a, a], axis=-1).reshape(g, 1, LANES)
    return pl.pallas_call(
        _s5_prep_kernel,
        out_shape=(jax.ShapeDtypeStruct((g, 1, LANES), _F32), jax.ShapeDtypeStruct((g, 1, LANES), _F32),
                   jax.ShapeDtypeStruct((2, HALF_ROWS, HALF_LANES), _BF16),
                   jax.ShapeDtypeStruct((2, HALF_LANES, HALF_ROWS), _BF16)),
        name="s5_prep",
    )(twice(a_re), twice(a_im), log_dt.reshape(g, 1, 1), xb, xc)


def _scan_half(xs_ref, s_re_ref, s_im_ref, ab_re_ref, ab_im_ref, half, n_batch, n_steps):
    lane = lax.broadcasted_iota(jnp.int32, (1, LANES), 1)

    def coeff(ref, c):
        row = jnp.where(lane < SSM_STATE, ref[2 * c], ref[2 * c + 1])
        return jnp.broadcast_to(row, (SUBLANES, LANES))

    def recur(r0, row_of_step, cls):
        chunks = [half * HALF_CHUNKS + cl for cl in cls]
        ar = [coeff(ab_re_ref, c) for c in chunks]
        ai = [coeff(ab_im_ref, c) for c in chunks]
        sr = [s_re_ref[pl.ds(r0, SUBLANES), c * LANES:(c + 1) * LANES] for c in chunks]
        si = [s_im_ref[pl.ds(r0, SUBLANES), c * LANES:(c + 1) * LANES] for c in chunks]
        for t in range(n_steps):
            rows = pl.ds(row_of_step(t), SUBLANES)
            for k, cl in enumerate(cls):
                lo = cl * CHUNK_LANES
                nr = ar[k] * sr[k] - ai[k] * si[k] + xs_ref[rows, lo:lo + LANES]
                ni = ar[k] * si[k] + ai[k] * sr[k] + xs_ref[rows, lo + LANES:lo + CHUNK_LANES]
                xs_ref[rows, lo:lo + LANES] = nr
                xs_ref[rows, lo + LANES:lo + CHUNK_LANES] = ni
                sr[k], si[k] = nr, ni
        for k, c in enumerate(chunks):
            s_re_ref[pl.ds(r0, SUBLANES), c * LANES:(c + 1) * LANES] = sr[k]
            s_im_ref[pl.ds(r0, SUBLANES), c * LANES:(c + 1) * LANES] = si[k]

    if n_batch == SUBLANES:
        recur(0, lambda t: t * n_batch, list(range(HALF_CHUNKS)))
    else:
        group = 4
        for cg in range(0, HALF_CHUNKS, group):
            def row_block(rb, carry, cg=cg):
                r0 = pl.multiple_of(rb * SUBLANES, SUBLANES)
                recur(r0, lambda t: pl.multiple_of(t * n_batch + r0, SUBLANES), list(range(cg, cg + group)))
                return carry
            lax.fori_loop(0, n_batch // SUBLANES, row_block, 0)


def _mixer_kernel(*refs, n_batch, n_steps, n_tiles, start_pos, zero_state, state_batch_minor):
    n_state_in = 0 if zero_state else 3
    x_ref = refs[0]
    hist0_ref, s0_re_ref, s0_im_ref = refs[1:1 + n_state_in] if n_state_in else (None, None, None)
    (norm_ref, w_in_ref, pool_w_ref, pool_scale_ref, ab_re_ref, ab_im_ref, b_ref, c_ref, d_ref, glu_w_ref,
     glu_b_ref, w_bp_ref, w_bs_ref, w_out_ref,
     h_ref, hist_out_ref, s_re_out_ref, s_im_out_ref,
     s_re_ref, s_im_ref, x_tm_ref, proj_ref, ext_ref, xs_ref) = refs[1 + n_state_in:]
    i = pl.program_id(0)
    rows = n_batch * n_steps
    hist_rows = HIST_STEPS * n_batch

    assert n_tiles == 1 or rows >= hist_rows
    @pl.when(i == 0)
    def _():
        if zero_state:
            ext_ref[0:hist_rows, :] = jnp.zeros((hist_rows, POOL_WIDTH), _F32)
            s_re_ref[...] = jnp.zeros(s_re_ref.shape, _F32)
            s_im_ref[...] = jnp.zeros(s_im_ref.shape, _F32)
        else:
            ext_ref[0:n_batch, :] = jnp.zeros((n_batch, POOL_WIDTH), _F32)
            ext_ref[n_batch:hist_rows, :] = hist0_ref[...].reshape(POOL_HIST * n_batch, POOL_WIDTH)
            s_re_ref[...] = s0_re_ref[...].T if state_batch_minor else s0_re_ref[...]
            s_im_ref[...] = s0_im_ref[...].T if state_batch_minor else s0_im_ref[...]

    x_tm_ref[...] = jnp.swapaxes(x_ref[...], 0, 1).reshape(rows, D_MODEL)
    xn = _rmsnorm(x_tm_ref[...], norm_ref[...]).astype(_BF16)
    u_width = POOL_WIDTH + SSM_WIDTH
    proj_ref[...] = _dot(xn, w_in_ref[...])

    ext_ref[hist_rows:, :] = proj_ref[:, 0:POOL_WIDTH]
    diffs = []
    for g, w in enumerate(POOL_WINDOWS):
        lo = g * POOL_GROUP_DIM
        u_g = ext_ref[hist_rows:, lo:lo + POOL_GROUP_DIM]
        acc = u_g
        for j in range(1, w):
            r0 = (HIST_STEPS - j) * n_batch
            acc = acc + ext_ref[r0:r0 + rows, lo:lo + POOL_GROUP_DIM]
        if start_pos + 1 >= w:
            pooled = acc * (1.0 / w)
        else:
            row = lax.broadcasted_iota(jnp.int32, (rows, POOL_GROUP_DIM), 0)
            step = lax.shift_right_logical(row, int(math.log2(n_batch)))
            pos = start_pos + i * n_steps + step
            pooled = acc / jnp.minimum(w, pos + 1).astype(_F32)
        diffs.append(pooled - u_g)
    if n_tiles > 1:
        ext_ref[0:hist_rows, :] = ext_ref[rows:rows + hist_rows, :]
    mixed = jnp.concatenate(
        [_dot(jnp.concatenate(diffs[0:2], axis=1).astype(_BF16), pool_w_ref[0]),
         _dot(jnp.concatenate(diffs[2:4], axis=1).astype(_BF16), pool_w_ref[1])], axis=1)
    a_out = (mixed * pool_scale_ref[...]).astype(_BF16)

    u_ssm = proj_ref[:, POOL_WIDTH:u_width]
    u_bf = u_ssm.astype(_BF16)
    for half in range(2):
        xs_ref[half] = _dot(u_bf[:, half * HALF_ROWS:(half + 1) * HALF_ROWS], b_ref[half])
    for half in range(2):
        _scan_half(xs_ref.at[half], s_re_ref, s_im_ref, ab_re_ref, ab_im_ref, half, n_batch, n_steps)
    ys = [_dot(xs_ref[half].astype(_BF16), c_ref[half]) for half in range(2)]
    y_ssm = jnp.concatenate(ys, axis=1) + d_ref[...] * u_ssm
    z = jax.nn.gelu(y_ssm)
    glu = jax.nn.sigmoid(_dot(z.astype(_BF16), glu_w_ref[...]) + glu_b_ref[...])
    b_out = (z * glu).astype(_BF16)

    merged = (jax.nn.sigmoid(proj_ref[:, u_width:u_width + D_MODEL]) * _dot(a_out, w_bp_ref[...])
              + jax.nn.sigmoid(proj_ref[:, u_width + D_MODEL:u_width + 2 * D_MODEL]) * _dot(b_out, w_bs_ref[...]))
    h_ref[...] = x_tm_ref[...] + _dot(merged.astype(_BF16), w_out_ref[...])

    @pl.when(i == n_tiles - 1)
    def _():
        hist_out_ref[...] = ext_ref[rows + n_batch:, :].reshape(POOL_HIST, n_batch, POOL_WIDTH)
        s_re_out_ref[...] = s_re_ref[...].T if state_batch_minor else s_re_ref[...]
        s_im_out_ref[...] = s_im_ref[...].T if state_batch_minor else s_im_ref[...]


def _resident(shape):
    zeros = (0,) * len(shape)
    return pl.BlockSpec(shape, lambda i: zeros, pipeline_mode=pl.Buffered(1))


def _tiling(n_batch, n_steps_total, rows_per_step=ROW_TILE):
    n_steps = min(n_steps_total, rows_per_step // n_batch)
    n_tiles = n_steps_total // n_steps
    assert n_tiles * n_steps == n_steps_total and n_batch % SUBLANES == 0
    assert n_batch & (n_batch - 1) == 0
    return n_steps, n_tiles


def _mixer(x, state, weights, *, start_pos, state_batch_minor):
    n_batch, n_steps_total, _ = x.shape
    n_steps, n_tiles = _tiling(n_batch, n_steps_total)
    rows = n_steps * n_batch
    hist_rows = HIST_STEPS * n_batch
    state = () if state is None else tuple(state)
    state_shape = (N_STATES, n_batch) if state_batch_minor else (n_batch, N_STATES)
    kern = functools.partial(_mixer_kernel, n_batch=n_batch, n_steps=n_steps, n_tiles=n_tiles,
                             start_pos=start_pos, zero_state=not state, state_batch_minor=state_batch_minor)
    state_block = pl.BlockSpec(state_shape, lambda i: (0, 0))
    return pl.pallas_call(
        kern,
        grid=(n_tiles,),
        in_specs=[pl.BlockSpec((n_batch, n_steps, D_MODEL), lambda i: (0, i, 0))]
        + [_resident(a.shape) for a in state + tuple(weights)],
        out_specs=(pl.BlockSpec((rows, D_MODEL), lambda i: (i, 0)),
                   pl.BlockSpec((POOL_HIST, n_batch, POOL_WIDTH), lambda i: (0, 0, 0)),
                   state_block, state_block),
        out_shape=(jax.ShapeDtypeStruct((n_steps_total * n_batch, D_MODEL), _F32),
                   jax.ShapeDtypeStruct((POOL_HIST, n_batch, POOL_WIDTH), _F32),
                   jax.ShapeDtypeStruct(state_shape, _F32),
                   jax.ShapeDtypeStruct(state_shape, _F32)),
        scratch_shapes=[pltpu.VMEM((n_batch, N_STATES), _F32),
                        pltpu.VMEM((n_batch, N_STATES), _F32),
                        pltpu.VMEM((rows, D_MODEL), _F32),
                        pltpu.VMEM((rows, IN_WIDTH), _F32),
                        pltpu.VMEM((hist_rows + rows, POOL_WIDTH), _F32),
                        pltpu.VMEM((2, rows, HALF_LANES), _F32)],
        compiler_params=pltpu.CompilerParams(dimension_semantics=("arbitrary",),
                                             vmem_limit_bytes=VMEM_LIMIT_BYTES),
        name="mixer",
    )(x, *state, *weights)


def _ffn_kernel(h_ref, norm_ffn_ref, w_gate_ref, w_up_ref, w_down_ref, norm_final_ref, y_ref,
                gate_ref, up_ref, *, n_sub):
    n_batch = y_ref.shape[0]
    sub_rows = h_ref.shape[0] // n_sub
    sub_steps = sub_rows // n_batch
    for s in range(n_sub):
        h = h_ref[s * sub_rows:(s + 1) * sub_rows, :]
        hn = _rmsnorm(h, norm_ffn_ref[...]).astype(_BF16)
        gate_ref[...] = _dot(hn, w_gate_ref[...])
        up_ref[...] = _dot(hn, w_up_ref[...])
        gate = gate_ref[...]
        f = (gate * jax.nn.sigmoid(gate) * up_ref[...]).astype(_BF16)
        y = _rmsnorm(h + _dot(f, w_down_ref[...]), norm_final_ref[...])
        y_ref[:, s * sub_steps:(s + 1) * sub_steps, :] = jnp.swapaxes(y.reshape(sub_steps, n_batch, D_MODEL), 0, 1)


def _ffn(h_rows, weights, *, n_batch):
    n_steps_total = h_rows.shape[0] // n_batch
    n_sub = max(1, min(FFN_SUB_TILES, h_rows.shape[0] // ROW_TILE))
    n_steps, n_tiles = _tiling(n_batch, n_steps_total, ROW_TILE * n_sub)
    rows = n_steps * n_batch
    return pl.pallas_call(
        functools.partial(_ffn_kernel, n_sub=n_sub),
        grid=(n_tiles,),
        in_specs=[pl.BlockSpec((rows, D_MODEL), lambda i: (i, 0))] + [_resident(w.shape) for w in weights],
        out_specs=pl.BlockSpec((n_batch, n_steps, D_MODEL), lambda i: (0, i, 0)),
        out_shape=jax.ShapeDtypeStruct((n_batch, n_steps_total, D_MODEL), _F32),
        scratch_shapes=[pltpu.VMEM((rows // n_sub, D_FF), _F32), pltpu.VMEM((rows // n_sub, D_FF), _F32)],
        compiler_params=pltpu.CompilerParams(dimension_semantics=("arbitrary",),
                                             vmem_limit_bytes=VMEM_LIMIT_BYTES),
        name="ffn",
    )(h_rows, *weights)


def _pool_block_diag(pool_w):
    z = jnp.zeros((POOL_GROUP_DIM, POOL_GROUP_DIM), pool_w.dtype)
    blocks = [jnp.block([[pool_w[2 * k], z], [z, pool_w[2 * k + 1]]]) for k in range(2)]
    return jnp.stack(blocks).astype(_BF16)


def kernel(x_prompt, x_sample, state_pool, state_ssm_re, state_ssm_im, norm_mix, w_in, pool_w, pool_scale, ssm_a_re, ssm_a_im, ssm_log_dt, ssm_b_re, ssm_b_im, ssm_c_re, ssm_c_im, ssm_d, glu_w, glu_b, w_branch_pool, w_branch_ssm, w_out, norm_ffn, ffn_w_gate, ffn_w_up, ffn_w_down, norm_final):
    assert w_in.shape[0] == 1, "single-layer trunk"
    ab_re, ab_im, b_mat, c_mat = _s5_prep(ssm_a_re[0], ssm_a_im[0], ssm_log_dt[0], ssm_b_re[0], ssm_b_im[0],
                                          ssm_c_re[0], ssm_c_im[0])
    mixer_w = (norm_mix[0].reshape(1, D_MODEL), w_in[0].astype(_BF16), _pool_block_diag(pool_w[0]),
               pool_scale[0].reshape(1, POOL_WIDTH), ab_re, ab_im, b_mat, c_mat, ssm_d[0].reshape(1, SSM_WIDTH),
               glu_w[0].astype(_BF16), glu_b[0].reshape(1, SSM_WIDTH), w_branch_pool[0].astype(_BF16),
               w_branch_ssm[0].astype(_BF16), w_out[0].astype(_BF16))
    ffn_w = (norm_ffn[0].reshape(1, D_MODEL), ffn_w_gate[0].astype(_BF16), ffn_w_up[0].astype(_BF16),
             ffn_w_down[0].astype(_BF16), norm_final.reshape(1, D_MODEL))

    def run(x, state, start_pos):
        b = x.shape[0]
        batch_minor = b % LANES == 0
        if state is not None:
            flat = ((lambda a: a.transpose(1, 2, 0).reshape(N_STATES, b)) if batch_minor
                    else (lambda a: a.reshape(b, N_STATES)))
            state = (state[0].transpose(1, 0, 2), flat(state[1]), flat(state[2]))
        h_rows, new_hist, s_re, s_im = _mixer(x, state, mixer_w, start_pos=start_pos, state_batch_minor=batch_minor)
        y = _ffn(h_rows, ffn_w, n_batch=b)
        unflat = ((lambda a: a.reshape(SSM_GROUPS, SSM_STATE, b).transpose(2, 0, 1)[None]) if batch_minor
                  else (lambda a: a.reshape(1, b, SSM_GROUPS, SSM_STATE)))
        return y, new_hist.transpose(1, 0, 2)[None], unflat(s_re), unflat(s_im)

    y_p, pool_p, re_p, im_p = run(x_prompt, None, 0)
    y_s, pool_s, re_s, im_s = run(x_sample, (state_pool[0], state_ssm_re[0], state_ssm_im[0]), PAST_LEN)
    return (y_p, y_s, pool_p, re_p, im_p, pool_s, re_s, im_s)
```

```python
import functools
import math

import jax
import jax.numpy as jnp
from jax import lax
from jax.experimental import pallas as pl
from jax.experimental.pallas import tpu as pltpu

D_MODEL = 1024
PAST_LEN = 16384
POOL_WIDTH = 512
POOL_WINDOWS = (2, 4, 8, 16)
POOL_GROUP_DIM = 128
POOL_HIST = 15
HIST_STEPS = POOL_HIST + 1
SSM_WIDTH = 512
SSM_GROUPS = 32
SSM_GROUP_DIM = 16
SSM_STATE = 64
N_STATES = SSM_GROUPS * SSM_STATE
IN_WIDTH = POOL_WIDTH + SSM_WIDTH + 2 * D_MODEL
D_FF = 2816
RMS_EPS = 1e-6

SUBLANES = 8
LANES = 128
MXU_DIM = 256
CHUNK_LANES = 2 * LANES
N_CHUNKS = N_STATES // LANES
HALF_CHUNKS = N_CHUNKS // 2
HALF_LANES = HALF_CHUNKS * CHUNK_LANES
HALF_ROWS = MXU_DIM
ROW_TILE = 512
FFN_SUB_TILES = 2
VMEM_LIMIT_BYTES = 56 * 1024 * 1024
VMEM_CAP_BYTES = 60 * 1024 * 1024

_F32 = jnp.float32
_BF16 = jnp.bfloat16


def _rmsnorm(x, g):
    return x * lax.rsqrt(jnp.mean(x * x, axis=-1, keepdims=True) + RMS_EPS) * g


def _dot(a, b):
    return jnp.dot(a, b, preferred_element_type=_F32)


def _s5_prep_kernel(a_re_ref, a_im_ref, log_dt_ref, xb_ref, xc_ref,
                    ab_re_ref, ab_im_ref, b_mat_ref, c_mat_ref):
    a_re = a_re_ref[...]
    a_im = a_im_ref[...]
    dt = jnp.exp(log_dt_ref[...])
    mag = jnp.exp(a_re * dt)
    ab_re = mag * jnp.cos(a_im * dt)
    ab_im = mag * jnp.sin(a_im * dt)
    num_re = ab_re - 1.0
    num_im = ab_im
    den = a_re * a_re + a_im * a_im
    coef_re = (num_re * a_re + num_im * a_im) / den
    coef_im = (num_im * a_re - num_re * a_im) / den
    ab_re_ref[...] = ab_re
    ab_im_ref[...] = ab_im

    n_rows = SSM_GROUPS * SSM_GROUP_DIM
    cr = jnp.broadcast_to(coef_re, (SSM_GROUPS, SSM_GROUP_DIM, LANES)).reshape(n_rows, LANES)
    ci = jnp.broadcast_to(coef_im, (SSM_GROUPS, SSM_GROUP_DIM, LANES)).reshape(n_rows, LANES)
    b_re = xb_ref[0]
    b_im = xb_ref[1]
    bp = (cr * b_re - ci * b_im, cr * b_im + ci * b_re)
    row = lax.broadcasted_iota(jnp.int32, (HALF_ROWS, LANES), 0)
    lane = lax.broadcasted_iota(jnp.int32, (HALF_ROWS, LANES), 1)
    own_lanes = (lane >> 6) == ((row >> 4) & 1)
    for half in range(2):
        for part in range(2):
            mine = jnp.where(own_lanes, bp[part][half * HALF_ROWS:(half + 1) * HALF_ROWS, :], 0.0)
            for cl in range(HALF_CHUNKS):
                blk = jnp.where((row >> 5) == cl, mine, 0.0)
                lo = cl * CHUNK_LANES + part * LANES
                b_mat_ref[half, :, lo:lo + LANES] = blk.astype(_BF16)

    row_c = lax.broadcasted_iota(jnp.int32, (LANES, HALF_ROWS), 0)
    col_c = lax.broadcasted_iota(jnp.int32, (LANES, HALF_ROWS), 1)
    own_cols = (row_c >> 6) == ((col_c >> 4) & 1)
    for half in range(2):
        for part in range(2):
            w = xc_ref[part, :, half * HALF_ROWS:(half + 1) * HALF_ROWS]
            if part == 1:
                w = -w
            mine = jnp.where(own_cols, w, 0.0)
            for cl in range(HALF_CHUNKS):
                blk = jnp.where((col_c >> 5) == cl, mine, 0.0)
                lo = cl * CHUNK_LANES + part * LANES
                c_mat_ref[half, lo:lo + LANES, :] = blk.astype(_BF16)


def _s5_prep(a_re, a_im, log_dt, b_re, b_im, c_re, c_im):
    g, p, h = SSM_GROUPS, SSM_STATE, SSM_GROUP_DIM
    xb = jnp.stack([b_re, b_im]).transpose(0, 1, 3, 2)
    xb = jnp.concatenate([xb, xb], axis=-1).reshape(2, g * h, LANES)
    xc = jnp.stack([c_re, c_im]).transpose(0, 3, 1, 2).reshape(2, p, g * h)
    xc = jnp.concatenate([xc, xc], axis=1)
    twice = lambda a: jnp.concatenate([a, a], axis=-1).reshape(g, 1, LANES)
    return pl.pallas_call(
        _s5_prep_kernel,
        out_shape=(jax.ShapeDtypeStruct((g, 1, LANES), _F32), jax.ShapeDtypeStruct((g, 1, LANES), _F32),
                   jax.ShapeDtypeStruct((2, HALF_ROWS, HALF_LANES), _BF16),
                   jax.ShapeDtypeStruct((2, HALF_LANES, HALF_ROWS), _BF16)),
        name="s5_prep",
    )(twice(a_re), twice(a_im), log_dt.reshape(g, 1, 1), xb, xc)


def _scan_half(xs_ref, s_re_ref, s_im_ref, ab_re_ref, ab_im_ref, half, n_batch, n_steps):
    lane = lax.broadcasted_iota(jnp.int32, (1, LANES), 1)

    def coeff(ref, c):
        row = jnp.where(lane < SSM_STATE, ref[2 * c], ref[2 * c + 1])
        return jnp.broadcast_to(row, (SUBLANES, LANES))

    def recur(r0, row_of_step, cls):
        chunks = [half * HALF_CHUNKS + cl for cl in cls]
        ar = [coeff(ab_re_ref, c) for c in chunks]
        ai = [coeff(ab_im_ref, c) for c in chunks]
        sr = [s_re_ref[pl.ds(r0, SUBLANES), c * LANES:(c + 1) * LANES] for c in chunks]
        si = [s_im_ref[pl.ds(r0, SUBLANES), c * LANES:(c + 1) * LANES] for c in chunks]
        for t in range(n_steps):
            rows = pl.ds(row_of_step(t), SUBLANES)
            for k, cl in enumerate(cls):
                lo = cl * CHUNK_LANES
                nr = ar[k] * sr[k] - ai[k] * si[k] + xs_ref[rows, lo:lo + LANES]
                ni = ar[k] * si[k] + ai[k] * sr[k] + xs_ref[rows, lo + LANES:lo + CHUNK_LANES]
                xs_ref[rows, lo:lo + LANES] = nr
                xs_ref[rows, lo + LANES:lo + CHUNK_LANES] = ni
                sr[k], si[k] = nr, ni
        for k, c in enumerate(chunks):
            s_re_ref[pl.ds(r0, SUBLANES), c * LANES:(c + 1) * LANES] = sr[k]
            s_im_ref[pl.ds(r0, SUBLANES), c * LANES:(c + 1) * LANES] = si[k]

    if n_batch == SUBLANES:
        recur(0, lambda t: t * n_batch, list(range(HALF_CHUNKS)))
    else:
        group = 4
        for cg in range(0, HALF_CHUNKS, group):
            def row_block(rb, carry, cg=cg):
                r0 = pl.multiple_of(rb * SUBLANES, SUBLANES)
                recur(r0, lambda t: pl.multiple_of(t * n_batch + r0, SUBLANES), list(range(cg, cg + group)))
                return carry
            lax.fori_loop(0, n_batch // SUBLANES, row_block, 0)


def _mixer_kernel(*refs, n_batch, n_steps, n_tiles, start_pos, zero_state, state_batch_minor):
    n_state_in = 0 if zero_state else 3
    x_ref = refs[0]
    hist0_ref, s0_re_ref, s0_im_ref = refs[1:1 + n_state_in] if n_state_in else (None, None, None)
    (norm_ref, w_in_ref, pool_w_ref, pool_scale_ref, ab_re_ref, ab_im_ref, b_ref, c_ref, d_ref, glu_w_ref,
     glu_b_ref, w_bp_ref, w_bs_ref, w_out_ref,
     h_ref, hist_out_ref, s_re_out_ref, s_im_out_ref,
     *scratch) = refs[1 + n_state_in:]
    i = pl.program_id(0)
    rows = n_batch * n_steps
    hist_rows = HIST_STEPS * n_batch

    scratch = list(scratch)
    hist_scratch = [scratch.pop(0)] if n_tiles > 1 else []
    x_tm_ref, proj_ref, ext_ref, xs_ref = scratch[:4]
    s_re_ref, s_im_ref = scratch[4:] if state_batch_minor else (s_re_out_ref, s_im_out_ref)
    hist_ref = hist_scratch[0] if hist_scratch else ext_ref.at[0:hist_rows]
    @pl.when(i == 0)
    def _():
        if zero_state:
            hist_ref[...] = jnp.zeros((hist_rows, POOL_WIDTH), _F32)
            s_re_ref[...] = jnp.zeros(s_re_ref.shape, _F32)
            s_im_ref[...] = jnp.zeros(s_im_ref.shape, _F32)
        else:
            hist_ref[0:n_batch, :] = jnp.zeros((n_batch, POOL_WIDTH), _F32)
            hist_ref[n_batch:hist_rows, :] = hist0_ref[...].reshape(POOL_HIST * n_batch, POOL_WIDTH)
            s_re_ref[...] = s0_re_ref[...].T if state_batch_minor else s0_re_ref[...]
            s_im_ref[...] = s0_im_ref[...].T if state_batch_minor else s0_im_ref[...]

    x_tm_ref[...] = jnp.swapaxes(x_ref[...], 0, 1).reshape(rows, D_MODEL)
    xn = _rmsnorm(x_tm_ref[...], norm_ref[...]).astype(_BF16)
    u_width = POOL_WIDTH + SSM_WIDTH
    proj_ref[...] = _dot(xn, w_in_ref[...])

    if hist_scratch:
        ext_ref[0:hist_rows, :] = hist_ref[...]
    ext_ref[hist_rows:, :] = proj_ref[:, 0:POOL_WIDTH]
    if hist_scratch:
        hist_ref[...] = ext_ref[rows:rows + hist_rows, :]
    diffs = []
    for g, w in enumerate(POOL_WINDOWS):
        lo = g * POOL_GROUP_DIM
        u_g = ext_ref[hist_rows:, lo:lo + POOL_GROUP_DIM]
        acc = u_g
        for j in range(1, w):
            r0 = (HIST_STEPS - j) * n_batch
            acc = acc + ext_ref[r0:r0 + rows, lo:lo + POOL_GROUP_DIM]
        if start_pos + 1 >= w:
            pooled = acc * (1.0 / w)
        else:
            row = lax.broadcasted_iota(jnp.int32, (rows, POOL_GROUP_DIM), 0)
            step = lax.shift_right_logical(row, int(math.log2(n_batch)))
            pos = start_pos + i * n_steps + step
            pooled = acc / jnp.minimum(w, pos + 1).astype(_F32)
        diffs.append(pooled - u_g)
    mixed = jnp.concatenate(
        [_dot(jnp.concatenate(diffs[0:2], axis=1).astype(_BF16), pool_w_ref[0]),
         _dot(jnp.concatenate(diffs[2:4], axis=1).astype(_BF16), pool_w_ref[1])], axis=1)
    a_out = (mixed * pool_scale_ref[...]).astype(_BF16)

    u_ssm = proj_ref[:, POOL_WIDTH:u_width]
    u_bf = u_ssm.astype(_BF16)
    for half in range(2):
        xs_ref[half] = _dot(u_bf[:, half * HALF_ROWS:(half + 1) * HALF_ROWS], b_ref[half])
    for half in range(2):
        _scan_half(xs_ref.at[half], s_re_ref, s_im_ref, ab_re_ref, ab_im_ref, half, n_batch, n_steps)
    ys = [_dot(xs_ref[half].astype(_BF16), c_ref[half]) for half in range(2)]
    y_ssm = jnp.concatenate(ys, axis=1) + d_ref[...] * u_ssm
    z = jax.nn.gelu(y_ssm)
    glu = jax.nn.sigmoid(_dot(z.astype(_BF16), glu_w_ref[...]) + glu_b_ref[...])
    b_out = (z * glu).astype(_BF16)

    merged = (jax.nn.sigmoid(proj_ref[:, u_width:u_width + D_MODEL]) * _dot(a_out, w_bp_ref[...])
              + jax.nn.sigmoid(proj_ref[:, u_width + D_MODEL:u_width + 2 * D_MODEL]) * _dot(b_out, w_bs_ref[...]))
    h_ref[...] = x_tm_ref[...] + _dot(merged.astype(_BF16), w_out_ref[...])

    @pl.when(i == n_tiles - 1)
    def _():
        newest = hist_ref[n_batch:hist_rows, :] if hist_scratch else ext_ref[rows + n_batch:, :]
        hist_out_ref[...] = newest.reshape(POOL_HIST, n_batch, POOL_WIDTH)
        if state_batch_minor:
            s_re_out_ref[...] = s_re_ref[...].T
            s_im_out_ref[...] = s_im_ref[...].T


def _resident(shape):
    zeros = (0,) * len(shape)
    return pl.BlockSpec(shape, lambda i: zeros, pipeline_mode=pl.Buffered(1))


def _tiling(n_batch, n_steps_total, rows_per_step=ROW_TILE):
    n_steps = min(n_steps_total, rows_per_step // n_batch)
    n_tiles = n_steps_total // n_steps
    assert n_tiles * n_steps == n_steps_total and n_batch % SUBLANES == 0
    assert n_batch & (n_batch - 1) == 0
    return n_steps, n_tiles


def _mixer(x, state, weights, *, start_pos, state_batch_minor):
    n_batch, n_steps_total, _ = x.shape
    n_steps, n_tiles = _tiling(n_batch, n_steps_total)
    rows = n_steps * n_batch
    hist_rows = HIST_STEPS * n_batch
    state = () if state is None else tuple(state)
    state_shape = (N_STATES, n_batch) if state_batch_minor else (n_batch, N_STATES)
    kern = functools.partial(_mixer_kernel, n_batch=n_batch, n_steps=n_steps, n_tiles=n_tiles,
                             start_pos=start_pos, zero_state=not state, state_batch_minor=state_batch_minor)
    state_block = pl.BlockSpec(state_shape, lambda i: (0, 0))
    hist_window_bytes = (2 if state else 1) * POOL_HIST * n_batch * POOL_WIDTH * 4
    vmem_limit = min(VMEM_CAP_BYTES, VMEM_LIMIT_BYTES + hist_window_bytes - hist_window_bytes % (1 << 20))
    return pl.pallas_call(
        kern,
        grid=(n_tiles,),
        in_specs=[pl.BlockSpec((n_batch, n_steps, D_MODEL), lambda i: (0, i, 0))]
        + [_resident(a.shape) for a in state + tuple(weights)],
        out_specs=(pl.BlockSpec((rows, D_MODEL), lambda i: (i, 0)),
                   pl.BlockSpec((POOL_HIST, n_batch, POOL_WIDTH), lambda i: (0, 0, 0)),
                   state_block, state_block),
        out_shape=(jax.ShapeDtypeStruct((n_steps_total * n_batch, D_MODEL), _F32),
                   jax.ShapeDtypeStruct((POOL_HIST, n_batch, POOL_WIDTH), _F32),
                   jax.ShapeDtypeStruct(state_shape, _F32),
                   jax.ShapeDtypeStruct(state_shape, _F32)),
        scratch_shapes=([pltpu.VMEM((hist_rows, POOL_WIDTH), _F32)] if n_tiles > 1 else [])
        + [pltpu.VMEM((rows, D_MODEL), _F32),
           pltpu.VMEM((rows, IN_WIDTH), _F32),
           pltpu.VMEM((hist_rows + rows, POOL_WIDTH), _F32),
           pltpu.VMEM((2, rows, HALF_LANES), _F32)]
        + ([pltpu.VMEM((n_batch, N_STATES), _F32)] * 2 if state_batch_minor else []),
        compiler_params=pltpu.CompilerParams(dimension_semantics=("arbitrary",), vmem_limit_bytes=vmem_limit),
        name="mixer",
    )(x, *state, *weights)


def _ffn_kernel(h_ref, norm_ffn_ref, w_gate_ref, w_up_ref, w_down_ref, norm_final_ref, y_ref,
                gate_ref, up_ref, *, n_sub):
    n_batch = y_ref.shape[0]
    sub_rows = h_ref.shape[0] // n_sub
    sub_steps = sub_rows // n_batch
    for s in range(n_sub):
        h = h_ref[s * sub_rows:(s + 1) * sub_rows, :]
        hn = _rmsnorm(h, norm_ffn_ref[...]).astype(_BF16)
        gate_ref[...] = _dot(hn, w_gate_ref[...])
        up_ref[...] = _dot(hn, w_up_ref[...])
        gate = gate_ref[...]
        f = (gate * jax.nn.sigmoid(gate) * up_ref[...]).astype(_BF16)
        y = _rmsnorm(h + _dot(f, w_down_ref[...]), norm_final_ref[...])
        y_ref[:, s * sub_steps:(s + 1) * sub_steps, :] = jnp.swapaxes(y.reshape(sub_steps, n_batch, D_MODEL), 0, 1)


def _ffn(h_rows, weights, *, n_batch):
    n_steps_total = h_rows.shape[0] // n_batch
    n_sub = max(1, min(FFN_SUB_TILES, h_rows.shape[0] // ROW_TILE))
    n_steps, n_tiles = _tiling(n_batch, n_steps_total, ROW_TILE * n_sub)
    rows = n_steps * n_batch
    return pl.pallas_call(
        functools.partial(_ffn_kernel, n_sub=n_sub),
        grid=(n_tiles,),
        in_specs=[pl.BlockSpec((rows, D_MODEL), lambda i: (i, 0))] + [_resident(w.shape) for w in weights],
        out_specs=pl.BlockSpec((n_batch, n_steps, D_MODEL), lambda i: (0, i, 0)),
        out_shape=jax.ShapeDtypeStruct((n_batch, n_steps_total, D_MODEL), _F32),
        scratch_shapes=[pltpu.VMEM((rows // n_sub, D_FF), _F32), pltpu.VMEM((rows // n_sub, D_FF), _F32)],
        compiler_params=pltpu.CompilerParams(dimension_semantics=("arbitrary",),
                                             vmem_limit_bytes=VMEM_LIMIT_BYTES),
        name="ffn",
    )(h_rows, *weights)


def _pool_block_diag(pool_w):
    z = jnp.zeros((POOL_GROUP_DIM, POOL_GROUP_DIM), pool_w.dtype)
    blocks = [jnp.block([[pool_w[2 * k], z], [z, pool_w[2 * k + 1]]]) for k in range(2)]
    return jnp.stack(blocks).astype(_BF16)


def kernel(x_prompt, x_sample, state_pool, state_ssm_re, state_ssm_im, norm_mix, w_in, pool_w, pool_scale, ssm_a_re, ssm_a_im, ssm_log_dt, ssm_b_re, ssm_b_im, ssm_c_re, ssm_c_im, ssm_d, glu_w, glu_b, w_branch_pool, w_branch_ssm, w_out, norm_ffn, ffn_w_gate, ffn_w_up, ffn_w_down, norm_final):
    assert w_in.shape[0] == 1, "single-layer trunk"
    ab_re, ab_im, b_mat, c_mat = _s5_prep(ssm_a_re[0], ssm_a_im[0], ssm_log_dt[0], ssm_b_re[0], ssm_b_im[0],
                                          ssm_c_re[0], ssm_c_im[0])
    mixer_w = (norm_mix[0].reshape(1, D_MODEL), w_in[0].astype(_BF16), _pool_block_diag(pool_w[0]),
               pool_scale[0].reshape(1, POOL_WIDTH), ab_re, ab_im, b_mat, c_mat, ssm_d[0].reshape(1, SSM_WIDTH),
               glu_w[0].astype(_BF16), glu_b[0].reshape(1, SSM_WIDTH), w_branch_pool[0].astype(_BF16),
               w_branch_ssm[0].astype(_BF16), w_out[0].astype(_BF16))
    ffn_w = (norm_ffn[0].reshape(1, D_MODEL), ffn_w_gate[0].astype(_BF16), ffn_w_up[0].astype(_BF16),
             ffn_w_down[0].astype(_BF16), norm_final.reshape(1, D_MODEL))

    def run(x, state, start_pos):
        b = x.shape[0]
        batch_minor = b % LANES == 0
        if state is not None:
            flat = ((lambda a: a.transpose(1, 2, 0).reshape(N_STATES, b)) if batch_minor
                    else (lambda a: a.reshape(b, N_STATES)))
            state = (state[0].transpose(1, 0, 2), flat(state[1]), flat(state[2]))
        h_rows, new_hist, s_re, s_im = _mixer(x, state, mixer_w, start_pos=start_pos, state_batch_minor=batch_minor)
        y = _ffn(h_rows, ffn_w, n_batch=b)
        unflat = ((lambda a: a.reshape(SSM_GROUPS, SSM_STATE, b).transpose(2, 0, 1)[None]) if batch_minor
                  else (lambda a: a.reshape(1, b, SSM_GROUPS, SSM_STATE)))
        return y, new_hist.transpose(1, 0, 2)[None], unflat(s_re), unflat(s_im)

    y_p, pool_p, re_p, im_p = run(x_prompt, None, 0)
    y_s, pool_s, re_s, im_s = run(x_sample, (state_pool[0], state_ssm_re[0], state_ssm_im[0]), PAST_LEN)
    return (y_p, y_s, pool_p, re_p, im_p, pool_s, re_s, im_s)
```

```python
import functools
import math

import jax
import jax.numpy as jnp
from jax import lax
from jax.experimental import pallas as pl
from jax.experimental.pallas import tpu as pltpu

D_MODEL = 1024
PAST_LEN = 16384
POOL_WIDTH = 512
POOL_WINDOWS = (2, 4, 8, 16)
POOL_GROUP_DIM = 128
POOL_HIST = 15
HIST_STEPS = POOL_HIST + 1
SSM_WIDTH = 512
SSM_GROUPS = 32
SSM_GROUP_DIM = 16
SSM_STATE = 64
N_STATES = SSM_GROUPS * SSM_STATE
IN_WIDTH = POOL_WIDTH + SSM_WIDTH + 2 * D_MODEL
D_FF = 2816
RMS_EPS = 1e-6

SUBLANES = 8
LANES = 128
MXU_DIM = 256
CHUNK_LANES = 2 * LANES
N_CHUNKS = N_STATES // LANES
HALF_CHUNKS = N_CHUNKS // 2
HALF_LANES = HALF_CHUNKS * CHUNK_LANES
HALF_ROWS = MXU_DIM
ROW_TILE = 512
FFN_SUB_TILES = 2
VMEM_LIMIT_BYTES = 56 * 1024 * 1024
VMEM_CAP_BYTES = 60 * 1024 * 1024

_F32 = jnp.float32
_BF16 = jnp.bfloat16


def _rmsnorm(x, g):
    return x * lax.rsqrt(jnp.mean(x * x, axis=-1, keepdims=True) + RMS_EPS) * g


def _dot(a, b):
    return jnp.dot(a, b, preferred_element_type=_F32)


def _s5_prep_kernel(a_re_ref, a_im_ref, log_dt_ref, xb_ref, xc_ref,
                    ab_re_ref, ab_im_ref, b_mat_ref, c_mat_ref):
    a_re = a_re_ref[...]
    a_im = a_im_ref[...]
    dt = jnp.exp(log_dt_ref[...])
    mag = jnp.exp(a_re * dt)
    ab_re = mag * jnp.cos(a_im * dt)
    ab_im = mag * jnp.sin(a_im * dt)
    num_re = ab_re - 1.0
    num_im = ab_im
    den = a_re * a_re + a_im * a_im
    coef_re = (num_re * a_re + num_im * a_im) / den
    coef_im = (num_im * a_re - num_re * a_im) / den
    ab_re_ref[...] = ab_re
    ab_im_ref[...] = ab_im

    n_rows = SSM_GROUPS * SSM_GROUP_DIM
    cr = jnp.broadcast_to(coef_re, (SSM_GROUPS, SSM_GROUP_DIM, LANES)).reshape(n_rows, LANES)
    ci = jnp.broadcast_to(coef_im, (SSM_GROUPS, SSM_GROUP_DIM, LANES)).reshape(n_rows, LANES)
    b_re = xb_ref[0]
    b_im = xb_ref[1]
    bp = (cr * b_re - ci * b_im, cr * b_im + ci * b_re)
    row = lax.broadcasted_iota(jnp.int32, (HALF_ROWS, LANES), 0)
    lane = lax.broadcasted_iota(jnp.int32, (HALF_ROWS, LANES), 1)
    own_lanes = (lane >> 6) == ((row >> 4) & 1)
    for half in range(2):
        for part in range(2):
            mine = jnp.where(own_lanes, bp[part][half * HALF_ROWS:(half + 1) * HALF_ROWS, :], 0.0)
            for cl in range(HALF_CHUNKS):
                blk = jnp.where((row >> 5) == cl, mine, 0.0)
                lo = cl * CHUNK_LANES + part * LANES
                b_mat_ref[half, :, lo:lo + LANES] = blk.astype(_BF16)

    row_c = lax.broadcasted_iota(jnp.int32, (LANES, HALF_ROWS), 0)
    col_c = lax.broadcasted_iota(jnp.int32, (LANES, HALF_ROWS), 1)
    own_cols = (row_c >> 6) == ((col_c >> 4) & 1)
    for half in range(2):
        for part in range(2):
            w = xc_ref[part, :, half * HALF_ROWS:(half + 1) * HALF_ROWS]
            if part == 1:
                w = -w
            mine = jnp.where(own_cols, w, 0.0)
            for cl in range(HALF_CHUNKS):
                blk = jnp.where((col_c >> 5) == cl, mine, 0.0)
                lo = cl * CHUNK_LANES + part * LANES
                c_mat_ref[half, lo:lo + LANES, :] = blk.astype(_BF16)


def _s5_prep(a_re, a_im, log_dt, b_re, b_im, c_re, c_im):
    g, p, h = SSM_GROUPS, SSM_STATE, SSM_GROUP_DIM
    xb = jnp.stack([b_re, b_im]).transpose(0, 1, 3, 2)
    xb = jnp.concatenate([xb, xb], axis=-1).reshape(2, g * h, LANES)
    xc = jnp.stack([c_re, c_im]).transpose(0, 3, 1, 2).reshape(2, p, g * h)
    xc = jnp.concatenate([xc, xc], axis=1)
    twice = lambda a: jnp.concatenate([a, a], axis=-1).reshape(g, 1, LANES)
    return pl.pallas_call(
        _s5_prep_kernel,
        out_shape=(jax.ShapeDtypeStruct((g, 1, LANES), _F32), jax.ShapeDtypeStruct((g, 1, LANES), _F32),
                   jax.ShapeDtypeStruct((2, HALF_ROWS, HALF_LANES), _BF16),
                   jax.ShapeDtypeStruct((2, HALF_LANES, HALF_ROWS), _BF16)),
        name="s5_prep",
    )(twice(a_re), twice(a_im), log_dt.reshape(g, 1, 1), xb, xc)


def _scan_half(xs_ref, s_re_ref, s_im_ref, ab_re_ref, ab_im_ref, half, n_batch, n_steps):
    lane = lax.broadcasted_iota(jnp.int32, (1, LANES), 1)

    def coeff(ref, c):
        row = jnp.where(lane < SSM_STATE, ref[2 * c], ref[2 * c + 1])
        return jnp.broadcast_to(row, (SUBLANES, LANES))

    def recur(r0, row_of_step, cls):
        chunks = [half * HALF_CHUNKS + cl for cl in cls]
        ar = [coeff(ab_re_ref, c) for c in chunks]
        ai = [coeff(ab_im_ref, c) for c in chunks]
        sr = [s_re_ref[pl.ds(r0, SUBLANES), c * LANES:(c + 1) * LANES] for c in chunks]
        si = [s_im_ref[pl.ds(r0, SUBLANES), c * LANES:(c + 1) * LANES] for c in chunks]
        for t in range(n_steps):
            rows = pl.ds(row_of_step(t), SUBLANES)
            for k, cl in enumerate(cls):
                lo = cl * CHUNK_LANES
                nr = ar[k] * sr[k] - ai[k] * si[k] + xs_ref[rows, lo:lo + LANES]
                ni = ar[k] * si[k] + ai[k] * sr[k] + xs_ref[rows, lo + LANES:lo + CHUNK_LANES]
                xs_ref[rows, lo:lo + LANES] = nr
                xs_ref[rows, lo + LANES:lo + CHUNK_LANES] = ni
                sr[k], si[k] = nr, ni
        for k, c in enumerate(chunks):
            s_re_ref[pl.ds(r0, SUBLANES), c * LANES:(c + 1) * LANES] = sr[k]
            s_im_ref[pl.ds(r0, SUBLANES), c * LANES:(c + 1) * LANES] = si[k]

    if n_batch == SUBLANES:
        recur(0, lambda t: t * n_batch, list(range(HALF_CHUNKS)))
    else:
        group = 4
        for cg in range(0, HALF_CHUNKS, group):
            def row_block(rb, carry, cg=cg):
                r0 = pl.multiple_of(rb * SUBLANES, SUBLANES)
                recur(r0, lambda t: pl.multiple_of(t * n_batch + r0, SUBLANES), list(range(cg, cg + group)))
                return carry
            lax.fori_loop(0, n_batch // SUBLANES, row_block, 0)


def _mixer_kernel(*refs, n_batch, n_steps, n_tiles, start_pos, zero_state, state_batch_minor):
    n_state_in = 0 if zero_state else 3
    x_ref = refs[0]
    hist0_ref, s0_re_ref, s0_im_ref = refs[1:1 + n_state_in] if n_state_in else (None, None, None)
    (norm_ref, w_in_ref, pool_w_ref, pool_scale_ref, ab_re_ref, ab_im_ref, b_ref, c_ref, d_ref, glu_w_ref,
     glu_b_ref, w_bp_ref, w_bs_ref, w_out_ref,
     h_ref, hist_out_ref, s_re_out_ref, s_im_out_ref,
     *scratch) = refs[1 + n_state_in:]
    i = pl.program_id(0)
    rows = n_batch * n_steps
    hist_rows = HIST_STEPS * n_batch

    scratch = list(scratch)
    hist_scratch = [scratch.pop(0)] if n_tiles > 1 else []
    x_tm_ref, proj_ref, ext_ref, xs_ref = scratch[:4]
    s_re_ref, s_im_ref = scratch[4:] if state_batch_minor else (s_re_out_ref, s_im_out_ref)
    hist_ref = hist_scratch[0] if hist_scratch else ext_ref.at[0:hist_rows]
    @pl.when(i == 0)
    def _():
        if zero_state:
            hist_ref[...] = jnp.zeros((hist_rows, POOL_WIDTH), _F32)
            s_re_ref[...] = jnp.zeros(s_re_ref.shape, _F32)
            s_im_ref[...] = jnp.zeros(s_im_ref.shape, _F32)
        else:
            hist_ref[0:n_batch, :] = jnp.zeros((n_batch, POOL_WIDTH), _F32)
            hist_ref[n_batch:hist_rows, :] = hist0_ref[...].reshape(POOL_HIST * n_batch, POOL_WIDTH)
            s_re_ref[...] = s0_re_ref[...].T if state_batch_minor else s0_re_ref[...]
            s_im_ref[...] = s0_im_ref[...].T if state_batch_minor else s0_im_ref[...]

    x_tm_ref[...] = jnp.swapaxes(x_ref[...], 0, 1).reshape(rows, D_MODEL)
    xn = _rmsnorm(x_tm_ref[...], norm_ref[...]).astype(_BF16)
    u_width = POOL_WIDTH + SSM_WIDTH
    proj_ref[...] = _dot(xn, w_in_ref[...])

    if hist_scratch:
        ext_ref[0:hist_rows, :] = hist_ref[...]
    ext_ref[hist_rows:, :] = proj_ref[:, 0:POOL_WIDTH]
    if hist_scratch:
        hist_ref[...] = ext_ref[rows:rows + hist_rows, :]
    diffs = []
    for g, w in enumerate(POOL_WINDOWS):
        lo = g * POOL_GROUP_DIM
        u_g = ext_ref[hist_rows:, lo:lo + POOL_GROUP_DIM]
        acc = u_g
        for j in range(1, w):
            r0 = (HIST_STEPS - j) * n_batch
            acc = acc + ext_ref[r0:r0 + rows, lo:lo + POOL_GROUP_DIM]
        if start_pos + 1 >= w:
            pooled = acc * (1.0 / w)
        else:
            row = lax.broadcasted_iota(jnp.int32, (rows, POOL_GROUP_DIM), 0)
            step = lax.shift_right_logical(row, int(math.log2(n_batch)))
            pos = start_pos + i * n_steps + step
            pooled = acc / jnp.minimum(w, pos + 1).astype(_F32)
        diffs.append(pooled - u_g)
    mixed = jnp.concatenate(
        [_dot(jnp.concatenate(diffs[0:2], axis=1).astype(_BF16), pool_w_ref[0]),
         _dot(jnp.concatenate(diffs[2:4], axis=1).astype(_BF16), pool_w_ref[1])], axis=1)
    a_out = (mixed * pool_scale_ref[...]).astype(_BF16)

    u_ssm = proj_ref[:, POOL_WIDTH:u_width]
    u_bf = u_ssm.astype(_BF16)
    for half in range(2):
        xs_ref[half] = _dot(u_bf[:, half * HALF_ROWS:(half + 1) * HALF_ROWS], b_ref[half])
    for half in range(2):
        _scan_half(xs_ref.at[half], s_re_ref, s_im_ref, ab_re_ref, ab_im_ref, half, n_batch, n_steps)
    ys = [_dot(xs_ref[half].astype(_BF16), c_ref[half]) for half in range(2)]
    y_ssm = jnp.concatenate(ys, axis=1) + d_ref[...] * u_ssm
    z = jax.nn.gelu(y_ssm)
    glu = jax.nn.sigmoid(_dot(z.astype(_BF16), glu_w_ref[...]) + glu_b_ref[...])
    b_out = (z * glu).astype(_BF16)

    merged = (jax.nn.sigmoid(proj_ref[:, u_width:u_width + D_MODEL]) * _dot(a_out, w_bp_ref[...])
              + jax.nn.sigmoid(proj_ref[:, u_width + D_MODEL:u_width + 2 * D_MODEL]) * _dot(b_out, w_bs_ref[...]))
    h_ref[...] = x_tm_ref[...] + _dot(merged.astype(_BF16), w_out_ref[...])

    @pl.when(i == n_tiles - 1)
    def _():
        newest = hist_ref[n_batch:hist_rows, :] if hist_scratch else ext_ref[rows + n_batch:, :]
        hist_out_ref[...] = newest.reshape(POOL_HIST, n_batch, POOL_WIDTH)
        if state_batch_minor:
            s_re_out_ref[...] = s_re_ref[...].T
            s_im_out_ref[...] = s_im_ref[...].T


def _resident(shape):
    zeros = (0,) * len(shape)
    return pl.BlockSpec(shape, lambda i: zeros, pipeline_mode=pl.Buffered(1))


def _tiling(n_batch, n_steps_total, rows_per_step=ROW_TILE):
    n_steps = min(n_steps_total, rows_per_step // n_batch)
    n_tiles = n_steps_total // n_steps
    assert n_tiles * n_steps == n_steps_total and n_batch % SUBLANES == 0
    assert n_batch & (n_batch - 1) == 0
    return n_steps, n_tiles


def _mixer(x, state, weights, *, start_pos, state_batch_minor):
    n_batch, n_steps_total, _ = x.shape
    n_steps, n_tiles = _tiling(n_batch, n_steps_total)
    rows = n_steps * n_batch
    hist_rows = HIST_STEPS * n_batch
    state = () if state is None else tuple(state)
    state_shape = (N_STATES, n_batch) if state_batch_minor else (n_batch, N_STATES)
    kern = functools.partial(_mixer_kernel, n_batch=n_batch, n_steps=n_steps, n_tiles=n_tiles,
                             start_pos=start_pos, zero_state=not state, state_batch_minor=state_batch_minor)
    state_block = pl.BlockSpec(state_shape, lambda i: (0, 0))
    hist_window_bytes = (2 if state else 1) * POOL_HIST * n_batch * POOL_WIDTH * 4
    vmem_limit = min(VMEM_CAP_BYTES, VMEM_LIMIT_BYTES + hist_window_bytes - hist_window_bytes % (1 << 20))
    return pl.pallas_call(
        kern,
        grid=(n_tiles,),
        in_specs=[pl.BlockSpec((n_batch, n_steps, D_MODEL), lambda i: (0, i, 0))]
        + [_resident(a.shape) for a in state + tuple(weights)],
        out_specs=(pl.BlockSpec((rows, D_MODEL), lambda i: (i, 0)),
                   pl.BlockSpec((POOL_HIST, n_batch, POOL_WIDTH), lambda i: (0, 0, 0)),
                   state_block, state_block),
        out_shape=(jax.ShapeDtypeStruct((n_steps_total * n_batch, D_MODEL), _F32),
                   jax.ShapeDtypeStruct((POOL_HIST, n_batch, POOL_WIDTH), _F32),
                   jax.ShapeDtypeStruct(state_shape, _F32),
                   jax.ShapeDtypeStruct(state_shape, _F32)),
        scratch_shapes=([pltpu.VMEM((hist_rows, POOL_WIDTH), _F32)] if n_tiles > 1 else [])
        + [pltpu.VMEM((rows, D_MODEL), _F32),
           pltpu.VMEM((rows, IN_WIDTH), _F32),
           pltpu.VMEM((hist_rows + rows, POOL_WIDTH), _F32),
           pltpu.VMEM((2, rows, HALF_LANES), _F32)]
        + ([pltpu.VMEM((n_batch, N_STATES), _F32)] * 2 if state_batch_minor else []),
        compiler_params=pltpu.CompilerParams(dimension_semantics=("arbitrary",), vmem_limit_bytes=vmem_limit),
        name="mixer",
    )(x, *state, *weights)


def _ffn_kernel(h_ref, norm_ffn_ref, w_gate_ref, w_up_ref, w_down_ref, norm_final_ref, y_ref,
                gate_ref, up_ref, *dma_scratch, n_batch, n_sub, n_tiles):
    i = pl.program_id(0)
    sub_rows = h_ref.shape[0] // n_sub
    sub_steps = sub_rows // n_batch
    steps = n_sub * sub_steps
    if dma_scratch:
        y_buf, sem = dma_scratch
        slot = lax.rem(i, 2)
    for s in range(n_sub):
        h = h_ref[s * sub_rows:(s + 1) * sub_rows, :]
        hn = _rmsnorm(h, norm_ffn_ref[...]).astype(_BF16)
        gate_ref[...] = _dot(hn, w_gate_ref[...])
        up_ref[...] = _dot(hn, w_up_ref[...])
        gate = gate_ref[...]
        f = (gate * jax.nn.sigmoid(gate) * up_ref[...]).astype(_BF16)
        y = _rmsnorm(h + _dot(f, w_down_ref[...]), norm_final_ref[...]).reshape(sub_steps, n_batch, D_MODEL)
        if dma_scratch:
            y_buf[slot, s * sub_steps:(s + 1) * sub_steps] = y
        else:
            y_ref[:, s * sub_steps:(s + 1) * sub_steps, :] = jnp.swapaxes(y, 0, 1)

    if dma_scratch:
        def copies(slot_, step):
            t0 = pl.multiple_of(step * steps, steps)
            return [pltpu.make_async_copy(y_buf.at[slot_, :, b, :], y_ref.at[b, pl.ds(t0, steps), :], sem.at[slot_])
                    for b in range(n_batch)]

        @pl.when(i > 0)
        def _():
            for c in copies(1 - slot, i - 1):
                c.wait()
        for c in copies(slot, i):
            c.start()

        @pl.when(i == n_tiles - 1)
        def _():
            for c in copies(slot, i):
                c.wait()


def _ffn(h_rows, weights, *, n_batch):
    n_steps_total = h_rows.shape[0] // n_batch
    n_sub = max(1, min(FFN_SUB_TILES, h_rows.shape[0] // ROW_TILE))
    n_steps, n_tiles = _tiling(n_batch, n_steps_total, ROW_TILE * n_sub)
    rows = n_steps * n_batch
    dma_out = n_tiles > 1
    return pl.pallas_call(
        functools.partial(_ffn_kernel, n_batch=n_batch, n_sub=n_sub, n_tiles=n_tiles),
        grid=(n_tiles,),
        in_specs=[pl.BlockSpec((rows, D_MODEL), lambda i: (i, 0))] + [_resident(w.shape) for w in weights],
        out_specs=(pl.BlockSpec(memory_space=pl.ANY) if dma_out
                   else pl.BlockSpec((n_batch, n_steps, D_MODEL), lambda i: (0, i, 0))),
        out_shape=jax.ShapeDtypeStruct((n_batch, n_steps_total, D_MODEL), _F32),
        scratch_shapes=[pltpu.VMEM((rows // n_sub, D_FF), _F32), pltpu.VMEM((rows // n_sub, D_FF), _F32)]
        + ([pltpu.VMEM((2, n_steps, n_batch, D_MODEL), _F32), pltpu.SemaphoreType.DMA((2,))] if dma_out else []),
        compiler_params=pltpu.CompilerParams(dimension_semantics=("arbitrary",),
                                             vmem_limit_bytes=VMEM_LIMIT_BYTES),
        name="ffn",
    )(h_rows, *weights)


def _pool_block_diag(pool_w):
    z = jnp.zeros((POOL_GROUP_DIM, POOL_GROUP_DIM), pool_w.dtype)
    blocks = [jnp.block([[pool_w[2 * k], z], [z, pool_w[2 * k + 1]]]) for k in range(2)]
    return jnp.stack(blocks).astype(_BF16)


def kernel(x_prompt, x_sample, state_pool, state_ssm_re, state_ssm_im, norm_mix, w_in, pool_w, pool_scale, ssm_a_re, ssm_a_im, ssm_log_dt, ssm_b_re, ssm_b_im, ssm_c_re, ssm_c_im, ssm_d, glu_w, glu_b, w_branch_pool, w_branch_ssm, w_out, norm_ffn, ffn_w_gate, ffn_w_up, ffn_w_down, norm_final):
    assert w_in.shape[0] == 1, "single-layer trunk"
    ab_re, ab_im, b_mat, c_mat = _s5_prep(ssm_a_re[0], ssm_a_im[0], ssm_log_dt[0], ssm_b_re[0], ssm_b_im[0],
                                          ssm_c_re[0], ssm_c_im[0])
    mixer_w = (norm_mix[0].reshape(1, D_MODEL), w_in[0].astype(_BF16), _pool_block_diag(pool_w[0]),
               pool_scale[0].reshape(1, POOL_WIDTH), ab_re, ab_im, b_mat, c_mat, ssm_d[0].reshape(1, SSM_WIDTH),
               glu_w[0].astype(_BF16), glu_b[0].reshape(1, SSM_WIDTH), w_branch_pool[0].astype(_BF16),
               w_branch_ssm[0].astype(_BF16), w_out[0].astype(_BF16))
    ffn_w = (norm_ffn[0].reshape(1, D_MODEL), ffn_w_gate[0].astype(_BF16), ffn_w_up[0].astype(_BF16),
             ffn_w_down[0].astype(_BF16), norm_final.reshape(1, D_MODEL))

    def run(x, state, start_pos):
        b = x.shape[0]
        batch_minor = b % LANES == 0
        if state is not None:
            flat = ((lambda a: a.transpose(1, 2, 0).reshape(N_STATES, b)) if batch_minor
                    else (lambda a: a.reshape(b, N_STATES)))
            state = (state[0].transpose(1, 0, 2), flat(state[1]), flat(state[2]))
        h_rows, new_hist, s_re, s_im = _mixer(x, state, mixer_w, start_pos=start_pos, state_batch_minor=batch_minor)
        y = _ffn(h_rows, ffn_w, n_batch=b)
        unflat = ((lambda a: a.reshape(SSM_GROUPS, SSM_STATE, b).transpose(2, 0, 1)[None]) if batch_minor
                  else (lambda a: a.reshape(1, b, SSM_GROUPS, SSM_STATE)))
        return y, new_hist.transpose(1, 0, 2)[None], unflat(s_re), unflat(s_im)

    y_p, pool_p, re_p, im_p = run(x_prompt, None, 0)
    y_s, pool_s, re_s, im_s = run(x_sample, (state_pool[0], state_ssm_re[0], state_ssm_im[0]), PAST_LEN)
    return (y_p, y_s, pool_p, re_p, im_p, pool_s, re_s, im_s)
```

```python
import functools
import math

import jax
import jax.numpy as jnp
from jax import lax
from jax.experimental import pallas as pl
from jax.experimental.pallas import tpu as pltpu

D_MODEL = 1024
PAST_LEN = 16384
POOL_WIDTH = 512
POOL_WINDOWS = (2, 4, 8, 16)
POOL_GROUP_DIM = 128
POOL_HIST = 15
HIST_STEPS = POOL_HIST + 1
SSM_WIDTH = 512
SSM_GROUPS = 32
SSM_GROUP_DIM = 16
SSM_STATE = 64
N_STATES = SSM_GROUPS * SSM_STATE
IN_WIDTH = POOL_WIDTH + SSM_WIDTH + 2 * D_MODEL
D_FF = 2816
RMS_EPS = 1e-6

SUBLANES = 8
LANES = 128
MXU_DIM = 256
CHUNK_LANES = 2 * LANES
N_CHUNKS = N_STATES // LANES
HALF_CHUNKS = N_CHUNKS // 2
HALF_LANES = HALF_CHUNKS * CHUNK_LANES
HALF_ROWS = MXU_DIM
ROW_TILE = 512
FFN_SUB_TILES = 2
VMEM_LIMIT_BYTES = 56 * 1024 * 1024
VMEM_CAP_BYTES = 60 * 1024 * 1024

_F32 = jnp.float32
_BF16 = jnp.bfloat16


def _rmsnorm(x, g):
    return x * lax.rsqrt(jnp.mean(x * x, axis=-1, keepdims=True) + RMS_EPS) * g


def _dot(a, b):
    return jnp.dot(a, b, preferred_element_type=_F32)


def _s5_prep_kernel(a_re_ref, a_im_ref, log_dt_ref, xb_ref, xc_ref,
                    ab_re_ref, ab_im_ref, b_mat_ref, c_mat_ref):
    a_re = a_re_ref[...]
    a_im = a_im_ref[...]
    dt = jnp.exp(log_dt_ref[...])
    mag = jnp.exp(a_re * dt)
    ab_re = mag * jnp.cos(a_im * dt)
    ab_im = mag * jnp.sin(a_im * dt)
    num_re = ab_re - 1.0
    num_im = ab_im
    den = a_re * a_re + a_im * a_im
    coef_re = (num_re * a_re + num_im * a_im) / den
    coef_im = (num_im * a_re - num_re * a_im) / den
    ab_re_ref[...] = ab_re
    ab_im_ref[...] = ab_im

    n_rows = SSM_GROUPS * SSM_GROUP_DIM
    cr = jnp.broadcast_to(coef_re, (SSM_GROUPS, SSM_GROUP_DIM, LANES)).reshape(n_rows, LANES)
    ci = jnp.broadcast_to(coef_im, (SSM_GROUPS, SSM_GROUP_DIM, LANES)).reshape(n_rows, LANES)
    b_re = xb_ref[0]
    b_im = xb_ref[1]
    bp = (cr * b_re - ci * b_im, cr * b_im + ci * b_re)
    row = lax.broadcasted_iota(jnp.int32, (HALF_ROWS, LANES), 0)
    lane = lax.broadcasted_iota(jnp.int32, (HALF_ROWS, LANES), 1)
    own_lanes = (lane >> 6) == ((row >> 4) & 1)
    for half in range(2):
        for part in range(2):
            mine = jnp.where(own_lanes, bp[part][half * HALF_ROWS:(half + 1) * HALF_ROWS, :], 0.0)
            for cl in range(HALF_CHUNKS):
                blk = jnp.where((row >> 5) == cl, mine, 0.0)
                lo = cl * CHUNK_LANES + part * LANES
                b_mat_ref[half, :, lo:lo + LANES] = blk.astype(_BF16)

    row_c = lax.broadcasted_iota(jnp.int32, (LANES, HALF_ROWS), 0)
    col_c = lax.broadcasted_iota(jnp.int32, (LANES, HALF_ROWS), 1)
    own_cols = (row_c >> 6) == ((col_c >> 4) & 1)
    for half in range(2):
        for part in range(2):
            w = xc_ref[part, :, half * HALF_ROWS:(half + 1) * HALF_ROWS]
            if part == 1:
                w = -w
            mine = jnp.where(own_cols, w, 0.0)
            for cl in range(HALF_CHUNKS):
                blk = jnp.where((col_c >> 5) == cl, mine, 0.0)
                lo = cl * CHUNK_LANES + part * LANES
                c_mat_ref[half, lo:lo + LANES, :] = blk.astype(_BF16)


def _s5_prep(a_re, a_im, log_dt, b_re, b_im, c_re, c_im):
    g, p, h = SSM_GROUPS, SSM_STATE, SSM_GROUP_DIM
    xb = jnp.stack([b_re, b_im]).transpose(0, 1, 3, 2)
    xb = jnp.concatenate([xb, xb], axis=-1).reshape(2, g * h, LANES)
    xc = jnp.stack([c_re, c_im]).transpose(0, 3, 1, 2).reshape(2, p, g * h)
    xc = jnp.concatenate([xc, xc], axis=1)
    twice = lambda a: jnp.concatenate([a, a], axis=-1).reshape(g, 1, LANES)
    return pl.pallas_call(
        _s5_prep_kernel,
        out_shape=(jax.ShapeDtypeStruct((g, 1, LANES), _F32), jax.ShapeDtypeStruct((g, 1, LANES), _F32),
                   jax.ShapeDtypeStruct((2, HALF_ROWS, HALF_LANES), _BF16),
                   jax.ShapeDtypeStruct((2, HALF_LANES, HALF_ROWS), _BF16)),
        name="s5_prep",
    )(twice(a_re), twice(a_im), log_dt.reshape(g, 1, 1), xb, xc)


def _scan_half(xs_ref, s_re_ref, s_im_ref, ab_re_ref, ab_im_ref, half, n_batch, n_steps):
    lane = lax.broadcasted_iota(jnp.int32, (1, LANES), 1)

    def coeff(ref, c):
        row = jnp.where(lane < SSM_STATE, ref[2 * c], ref[2 * c + 1])
        return jnp.broadcast_to(row, (SUBLANES, LANES))

    def recur(r0, row_of_step, cls):
        chunks = [half * HALF_CHUNKS + cl for cl in cls]
        ar = [coeff(ab_re_ref, c) for c in chunks]
        ai = [coeff(ab_im_ref, c) for c in chunks]
        sr = [s_re_ref[pl.ds(r0, SUBLANES), c * LANES:(c + 1) * LANES] for c in chunks]
        si = [s_im_ref[pl.ds(r0, SUBLANES), c * LANES:(c + 1) * LANES] for c in chunks]
        for t in range(n_steps):
            rows = pl.ds(row_of_step(t), SUBLANES)
            for k, cl in enumerate(cls):
                lo = cl * CHUNK_LANES
                nr = ar[k] * sr[k] - ai[k] * si[k] + xs_ref[rows, lo:lo + LANES]
                ni = ar[k] * si[k] + ai[k] * sr[k] + xs_ref[rows, lo + LANES:lo + CHUNK_LANES]
                xs_ref[rows, lo:lo + LANES] = nr
                xs_ref[rows, lo + LANES:lo + CHUNK_LANES] = ni
                sr[k], si[k] = nr, ni
        for k, c in enumerate(chunks):
            s_re_ref[pl.ds(r0, SUBLANES), c * LANES:(c + 1) * LANES] = sr[k]
            s_im_ref[pl.ds(r0, SUBLANES), c * LANES:(c + 1) * LANES] = si[k]

    if n_batch == SUBLANES:
        recur(0, lambda t: t * n_batch, list(range(HALF_CHUNKS)))
    else:
        group = 4
        for cg in range(0, HALF_CHUNKS, group):
            def row_block(rb, carry, cg=cg):
                r0 = pl.multiple_of(rb * SUBLANES, SUBLANES)
                recur(r0, lambda t: pl.multiple_of(t * n_batch + r0, SUBLANES), list(range(cg, cg + group)))
                return carry
            lax.fori_loop(0, n_batch // SUBLANES, row_block, 0)


def _mixer_kernel(*refs, n_batch, n_steps, n_tiles, start_pos, zero_state, state_batch_minor):
    n_state_in = 0 if zero_state else 3
    x_ref = refs[0]
    hist0_ref, s0_re_ref, s0_im_ref = refs[1:1 + n_state_in] if n_state_in else (None, None, None)
    (norm_ref, w_in_ref, pool_w_ref, pool_scale_ref, ab_re_ref, ab_im_ref, b_ref, c_ref, d_ref, glu_w_ref,
     glu_b_ref, w_bp_ref, w_bs_ref, w_out_ref,
     h_ref, hist_out_ref, s_re_out_ref, s_im_out_ref,
     *scratch) = refs[1 + n_state_in:]
    i = pl.program_id(0)
    rows = n_batch * n_steps
    hist_rows = HIST_STEPS * n_batch

    scratch = list(scratch)
    hist_scratch = [scratch.pop(0)] if n_tiles > 1 else []
    x_stage_ref, proj_ref, ext_ref, xs_ref = scratch[:4]
    s_re_ref, s_im_ref = scratch[4:6] if state_batch_minor else (s_re_out_ref, s_im_out_ref)
    dma_in = n_tiles > 1
    x_sem = scratch[-1] if dma_in else None
    hist_ref = hist_scratch[0] if hist_scratch else ext_ref.at[0:hist_rows]
    @pl.when(i == 0)
    def _():
        if zero_state:
            hist_ref[...] = jnp.zeros((hist_rows, POOL_WIDTH), _F32)
            s_re_ref[...] = jnp.zeros(s_re_ref.shape, _F32)
            s_im_ref[...] = jnp.zeros(s_im_ref.shape, _F32)
        else:
            hist_ref[0:n_batch, :] = jnp.zeros((n_batch, POOL_WIDTH), _F32)
            hist_ref[n_batch:hist_rows, :] = hist0_ref[...].reshape(POOL_HIST * n_batch, POOL_WIDTH)
            s_re_ref[...] = s0_re_ref[...].T if state_batch_minor else s0_re_ref[...]
            s_im_ref[...] = s0_im_ref[...].T if state_batch_minor else s0_im_ref[...]

    if dma_in:
        slot = lax.rem(i, 2)

        def copies(slot_, step):
            t0 = pl.multiple_of(step * n_steps, n_steps)
            return [pltpu.make_async_copy(x_ref.at[b, pl.ds(t0, n_steps), :], x_stage_ref.at[slot_, :, b, :],
                                          x_sem.at[slot_]) for b in range(n_batch)]

        @pl.when(i == 0)
        def _():
            for c in copies(0, 0):
                c.start()
        for c in copies(slot, i):
            c.wait()

        @pl.when(i + 1 < n_tiles)
        def _():
            for c in copies(1 - slot, i + 1):
                c.start()
        x_rows = lambda: x_stage_ref[slot].reshape(rows, D_MODEL)
    else:
        x_stage_ref[...] = jnp.swapaxes(x_ref[...], 0, 1).reshape(rows, D_MODEL)
        x_rows = lambda: x_stage_ref[...]
    xn = _rmsnorm(x_rows(), norm_ref[...]).astype(_BF16)
    u_width = POOL_WIDTH + SSM_WIDTH
    proj_ref[...] = _dot(xn, w_in_ref[...])

    if hist_scratch:
        ext_ref[0:hist_rows, :] = hist_ref[...]
    ext_ref[hist_rows:, :] = proj_ref[:, 0:POOL_WIDTH]
    if hist_scratch:
        hist_ref[...] = ext_ref[rows:rows + hist_rows, :]
    diffs = []
    for g, w in enumerate(POOL_WINDOWS):
        lo = g * POOL_GROUP_DIM
        u_g = ext_ref[hist_rows:, lo:lo + POOL_GROUP_DIM]
        acc = u_g
        for j in range(1, w):
            r0 = (HIST_STEPS - j) * n_batch
            acc = acc + ext_ref[r0:r0 + rows, lo:lo + POOL_GROUP_DIM]
        if start_pos + 1 >= w:
            pooled = acc * (1.0 / w)
        else:
            row = lax.broadcasted_iota(jnp.int32, (rows, POOL_GROUP_DIM), 0)
            step = lax.shift_right_logical(row, int(math.log2(n_batch)))
            pos = start_pos + i * n_steps + step
            pooled = acc / jnp.minimum(w, pos + 1).astype(_F32)
        diffs.append(pooled - u_g)
    mixed = jnp.concatenate(
        [_dot(jnp.concatenate(diffs[0:2], axis=1).astype(_BF16), pool_w_ref[0]),
         _dot(jnp.concatenate(diffs[2:4], axis=1).astype(_BF16), pool_w_ref[1])], axis=1)
    a_out = (mixed * pool_scale_ref[...]).astype(_BF16)

    u_ssm = proj_ref[:, POOL_WIDTH:u_width]
    u_bf = u_ssm.astype(_BF16)
    for half in range(2):
        xs_ref[half] = _dot(u_bf[:, half * HALF_ROWS:(half + 1) * HALF_ROWS], b_ref[half])
    for half in range(2):
        _scan_half(xs_ref.at[half], s_re_ref, s_im_ref, ab_re_ref, ab_im_ref, half, n_batch, n_steps)
    ys = [_dot(xs_ref[half].astype(_BF16), c_ref[half]) for half in range(2)]
    y_ssm = jnp.concatenate(ys, axis=1) + d_ref[...] * u_ssm
    z = jax.nn.gelu(y_ssm)
    glu = jax.nn.sigmoid(_dot(z.astype(_BF16), glu_w_ref[...]) + glu_b_ref[...])
    b_out = (z * glu).astype(_BF16)

    merged = (jax.nn.sigmoid(proj_ref[:, u_width:u_width + D_MODEL]) * _dot(a_out, w_bp_ref[...])
              + jax.nn.sigmoid(proj_ref[:, u_width + D_MODEL:u_width + 2 * D_MODEL]) * _dot(b_out, w_bs_ref[...]))
    h_ref[...] = x_rows() + _dot(merged.astype(_BF16), w_out_ref[...])

    @pl.when(i == n_tiles - 1)
    def _():
        newest = hist_ref[n_batch:hist_rows, :] if hist_scratch else ext_ref[rows + n_batch:, :]
        hist_out_ref[...] = newest.reshape(POOL_HIST, n_batch, POOL_WIDTH)
        if state_batch_minor:
            s_re_out_ref[...] = s_re_ref[...].T
            s_im_out_ref[...] = s_im_ref[...].T


def _resident(shape):
    zeros = (0,) * len(shape)
    return pl.BlockSpec(shape, lambda i: zeros, pipeline_mode=pl.Buffered(1))


def _tiling(n_batch, n_steps_total, rows_per_step=ROW_TILE):
    n_steps = min(n_steps_total, rows_per_step // n_batch)
    n_tiles = n_steps_total // n_steps
    assert n_tiles * n_steps == n_steps_total and n_batch % SUBLANES == 0
    assert n_batch & (n_batch - 1) == 0
    return n_steps, n_tiles


def _mixer(x, state, weights, *, start_pos, state_batch_minor):
    n_batch, n_steps_total, _ = x.shape
    n_steps, n_tiles = _tiling(n_batch, n_steps_total)
    rows = n_steps * n_batch
    hist_rows = HIST_STEPS * n_batch
    state = () if state is None else tuple(state)
    state_shape = (N_STATES, n_batch) if state_batch_minor else (n_batch, N_STATES)
    kern = functools.partial(_mixer_kernel, n_batch=n_batch, n_steps=n_steps, n_tiles=n_tiles,
                             start_pos=start_pos, zero_state=not state, state_batch_minor=state_batch_minor)
    state_block = pl.BlockSpec(state_shape, lambda i: (0, 0))
    hist_window_bytes = (2 if state else 1) * POOL_HIST * n_batch * POOL_WIDTH * 4
    vmem_limit = min(VMEM_CAP_BYTES, VMEM_LIMIT_BYTES + hist_window_bytes - hist_window_bytes % (1 << 20))
    return pl.pallas_call(
        kern,
        grid=(n_tiles,),
        in_specs=[pl.BlockSpec(memory_space=pl.ANY) if n_tiles > 1
                  else pl.BlockSpec((n_batch, n_steps, D_MODEL), lambda i: (0, i, 0))]
        + [_resident(a.shape) for a in state + tuple(weights)],
        out_specs=(pl.BlockSpec((rows, D_MODEL), lambda i: (i, 0)),
                   pl.BlockSpec((POOL_HIST, n_batch, POOL_WIDTH), lambda i: (0, 0, 0)),
                   state_block, state_block),
        out_shape=(jax.ShapeDtypeStruct((n_steps_total * n_batch, D_MODEL), _F32),
                   jax.ShapeDtypeStruct((POOL_HIST, n_batch, POOL_WIDTH), _F32),
                   jax.ShapeDtypeStruct(state_shape, _F32),
                   jax.ShapeDtypeStruct(state_shape, _F32)),
        scratch_shapes=([pltpu.VMEM((hist_rows, POOL_WIDTH), _F32)] if n_tiles > 1 else [])
        + [pltpu.VMEM((2, n_steps, n_batch, D_MODEL) if n_tiles > 1 else (rows, D_MODEL), _F32),
           pltpu.VMEM((rows, IN_WIDTH), _F32),
           pltpu.VMEM((hist_rows + rows, POOL_WIDTH), _F32),
           pltpu.VMEM((2, rows, HALF_LANES), _F32)]
        + ([pltpu.VMEM((n_batch, N_STATES), _F32)] * 2 if state_batch_minor else [])
        + ([pltpu.SemaphoreType.DMA((2,))] if n_tiles > 1 else []),
        compiler_params=pltpu.CompilerParams(dimension_semantics=("arbitrary",), vmem_limit_bytes=vmem_limit),
        name="mixer",
    )(x, *state, *weights)


def _ffn_kernel(h_ref, norm_ffn_ref, w_gate_ref, w_up_ref, w_down_ref, norm_final_ref, y_ref,
                gate_ref, up_ref, *dma_scratch, n_batch, n_sub, n_tiles):
    i = pl.program_id(0)
    sub_rows = h_ref.shape[0] // n_sub
    sub_steps = sub_rows // n_batch
    steps = n_sub * sub_steps
    if dma_scratch:
        y_buf, sem = dma_scratch
        slot = lax.rem(i, 2)
    for s in range(n_sub):
        h = h_ref[s * sub_rows:(s + 1) * sub_rows, :]
        hn = _rmsnorm(h, norm_ffn_ref[...]).astype(_BF16)
        gate_ref[...] = _dot(hn, w_gate_ref[...])
        up_ref[...] = _dot(hn, w_up_ref[...])
        gate = gate_ref[...]
        f = (gate * jax.nn.sigmoid(gate) * up_ref[...]).astype(_BF16)
        y = _rmsnorm(h + _dot(f, w_down_ref[...]), norm_final_ref[...]).reshape(sub_steps, n_batch, D_MODEL)
        if dma_scratch:
            y_buf[slot, s * sub_steps:(s + 1) * sub_steps] = y
        else:
            y_ref[:, s * sub_steps:(s + 1) * sub_steps, :] = jnp.swapaxes(y, 0, 1)

    if dma_scratch:
        def copies(slot_, step):
            t0 = pl.multiple_of(step * steps, steps)
            return [pltpu.make_async_copy(y_buf.at[slot_, :, b, :], y_ref.at[b, pl.ds(t0, steps), :], sem.at[slot_])
                    for b in range(n_batch)]

        @pl.when(i > 0)
        def _():
            for c in copies(1 - slot, i - 1):
                c.wait()
        for c in copies(slot, i):
            c.start()

        @pl.when(i == n_tiles - 1)
        def _():
            for c in copies(slot, i):
                c.wait()


def _ffn(h_rows, weights, *, n_batch):
    n_steps_total = h_rows.shape[0] // n_batch
    n_sub = max(1, min(FFN_SUB_TILES, h_rows.shape[0] // ROW_TILE))
    n_steps, n_tiles = _tiling(n_batch, n_steps_total, ROW_TILE * n_sub)
    rows = n_steps * n_batch
    dma_out = n_tiles > 1
    return pl.pallas_call(
        functools.partial(_ffn_kernel, n_batch=n_batch, n_sub=n_sub, n_tiles=n_tiles),
        grid=(n_tiles,),
        in_specs=[pl.BlockSpec((rows, D_MODEL), lambda i: (i, 0))] + [_resident(w.shape) for w in weights],
        out_specs=(pl.BlockSpec(memory_space=pl.ANY) if dma_out
                   else pl.BlockSpec((n_batch, n_steps, D_MODEL), lambda i: (0, i, 0))),
        out_shape=jax.ShapeDtypeStruct((n_batch, n_steps_total, D_MODEL), _F32),
        scratch_shapes=[pltpu.VMEM((rows // n_sub, D_FF), _F32), pltpu.VMEM((rows // n_sub, D_FF), _F32)]
        + ([pltpu.VMEM((2, n_steps, n_batch, D_MODEL), _F32), pltpu.SemaphoreType.DMA((2,))] if dma_out else []),
        compiler_params=pltpu.CompilerParams(dimension_semantics=("arbitrary",),
                                             vmem_limit_bytes=VMEM_LIMIT_BYTES),
        name="ffn",
    )(h_rows, *weights)


def _pool_block_diag(pool_w):
    z = jnp.zeros((POOL_GROUP_DIM, POOL_GROUP_DIM), pool_w.dtype)
    blocks = [jnp.block([[pool_w[2 * k], z], [z, pool_w[2 * k + 1]]]) for k in range(2)]
    return jnp.stack(blocks).astype(_BF16)


def kernel(x_prompt, x_sample, state_pool, state_ssm_re, state_ssm_im, norm_mix, w_in, pool_w, pool_scale, ssm_a_re, ssm_a_im, ssm_log_dt, ssm_b_re, ssm_b_im, ssm_c_re, ssm_c_im, ssm_d, glu_w, glu_b, w_branch_pool, w_branch_ssm, w_out, norm_ffn, ffn_w_gate, ffn_w_up, ffn_w_down, norm_final):
    assert w_in.shape[0] == 1, "single-layer trunk"
    ab_re, ab_im, b_mat, c_mat = _s5_prep(ssm_a_re[0], ssm_a_im[0], ssm_log_dt[0], ssm_b_re[0], ssm_b_im[0],
                                          ssm_c_re[0], ssm_c_im[0])
    mixer_w = (norm_mix[0].reshape(1, D_MODEL), w_in[0].astype(_BF16), _pool_block_diag(pool_w[0]),
               pool_scale[0].reshape(1, POOL_WIDTH), ab_re, ab_im, b_mat, c_mat, ssm_d[0].reshape(1, SSM_WIDTH),
               glu_w[0].astype(_BF16), glu_b[0].reshape(1, SSM_WIDTH), w_branch_pool[0].astype(_BF16),
               w_branch_ssm[0].astype(_BF16), w_out[0].astype(_BF16))
    ffn_w = (norm_ffn[0].reshape(1, D_MODEL), ffn_w_gate[0].astype(_BF16), ffn_w_up[0].astype(_BF16),
             ffn_w_down[0].astype(_BF16), norm_final.reshape(1, D_MODEL))

    def run(x, state, start_pos):
        b = x.shape[0]
        batch_minor = b % LANES == 0
        if state is not None:
            flat = ((lambda a: a.transpose(1, 2, 0).reshape(N_STATES, b)) if batch_minor
                    else (lambda a: a.reshape(b, N_STATES)))
            state = (state[0].transpose(1, 0, 2), flat(state[1]), flat(state[2]))
        h_rows, new_hist, s_re, s_im = _mixer(x, state, mixer_w, start_pos=start_pos, state_batch_minor=batch_minor)
        y = _ffn(h_rows, ffn_w, n_batch=b)
        unflat = ((lambda a: a.reshape(SSM_GROUPS, SSM_STATE, b).transpose(2, 0, 1)[None]) if batch_minor
                  else (lambda a: a.reshape(1, b, SSM_GROUPS, SSM_STATE)))
        return y, new_hist.transpose(1, 0, 2)[None], unflat(s_re), unflat(s_im)

    y_p, pool_p, re_p, im_p = run(x_prompt, None, 0)
    y_s, pool_s, re_s, im_s = run(x_sample, (state_pool[0], state_ssm_re[0], state_ssm_im[0]), PAST_LEN)
    return (y_p, y_s, pool_p, re_p, im_p, pool_s, re_s, im_s)
```

```python
import functools
import math

import jax
import jax.numpy as jnp
from jax import lax
from jax.experimental import pallas as pl
from jax.experimental.pallas import tpu as pltpu

D_MODEL = 1024
PAST_LEN = 16384
POOL_WIDTH = 512
POOL_WINDOWS = (2, 4, 8, 16)
POOL_GROUP_DIM = 128
POOL_HIST = 15
HIST_STEPS = POOL_HIST + 1
SSM_WIDTH = 512
SSM_GROUPS = 32
SSM_GROUP_DIM = 16
SSM_STATE = 64
N_STATES = SSM_GROUPS * SSM_STATE
IN_WIDTH = POOL_WIDTH + SSM_WIDTH + 2 * D_MODEL
D_FF = 2816
RMS_EPS = 1e-6

SUBLANES = 8
BF16_ROWS = 16
LANES = 128
MXU_DIM = 256
CHUNK_LANES = 2 * LANES
N_CHUNKS = N_STATES // LANES
HALF_CHUNKS = N_CHUNKS // 2
HALF_LANES = HALF_CHUNKS * CHUNK_LANES
HALF_ROWS = MXU_DIM
ROW_TILE = 512
FFN_SUB_TILES = 2
FFN_CAST_STEPS = 16
VMEM_LIMIT_BYTES = 56 * 1024 * 1024
VMEM_CAP_BYTES = 60 * 1024 * 1024

_F32 = jnp.float32
_BF16 = jnp.bfloat16


def _rmsnorm(x, g):
    return x * lax.rsqrt(jnp.mean(x * x, axis=-1, keepdims=True) + RMS_EPS) * g


def _dot(a, b):
    return jnp.dot(a, b, preferred_element_type=_F32)


def _s5_prep_kernel(a_re_ref, a_im_ref, log_dt_ref, xb_ref, xc_ref,
                    ab_re_ref, ab_im_ref, b_mat_ref, c_mat_ref):
    a_re = a_re_ref[...]
    a_im = a_im_ref[...]
    dt = jnp.exp(log_dt_ref[...])
    mag = jnp.exp(a_re * dt)
    ab_re = mag * jnp.cos(a_im * dt)
    ab_im = mag * jnp.sin(a_im * dt)
    num_re = ab_re - 1.0
    num_im = ab_im
    den = a_re * a_re + a_im * a_im
    coef_re = (num_re * a_re + num_im * a_im) / den
    coef_im = (num_im * a_re - num_re * a_im) / den
    ab_re_ref[...] = ab_re
    ab_im_ref[...] = ab_im

    n_rows = SSM_GROUPS * SSM_GROUP_DIM
    cr = jnp.broadcast_to(coef_re, (SSM_GROUPS, SSM_GROUP_DIM, LANES)).reshape(n_rows, LANES)
    ci = jnp.broadcast_to(coef_im, (SSM_GROUPS, SSM_GROUP_DIM, LANES)).reshape(n_rows, LANES)
    b_re = xb_ref[0]
    b_im = xb_ref[1]
    bp = (cr * b_re - ci * b_im, cr * b_im + ci * b_re)
    row = lax.broadcasted_iota(jnp.int32, (HALF_ROWS, LANES), 0)
    lane = lax.broadcasted_iota(jnp.int32, (HALF_ROWS, LANES), 1)
    own_lanes = (lane >> 6) == ((row >> 4) & 1)
    for half in range(2):
        for part in range(2):
            mine = jnp.where(own_lanes, bp[part][half * HALF_ROWS:(half + 1) * HALF_ROWS, :], 0.0)
            for cl in range(HALF_CHUNKS):
                blk = jnp.where((row >> 5) == cl, mine, 0.0)
                lo = cl * CHUNK_LANES + part * LANES
                b_mat_ref[half, :, lo:lo + LANES] = blk.astype(_BF16)

    row_c = lax.broadcasted_iota(jnp.int32, (LANES, HALF_ROWS), 0)
    col_c = lax.broadcasted_iota(jnp.int32, (LANES, HALF_ROWS), 1)
    own_cols = (row_c >> 6) == ((col_c >> 4) & 1)
    for half in range(2):
        for part in range(2):
            w = xc_ref[part, :, half * HALF_ROWS:(half + 1) * HALF_ROWS]
            if part == 1:
                w = -w
            mine = jnp.where(own_cols, w, 0.0)
            for cl in range(HALF_CHUNKS):
                blk = jnp.where((col_c >> 5) == cl, mine, 0.0)
                lo = cl * CHUNK_LANES + part * LANES
                c_mat_ref[half, lo:lo + LANES, :] = blk.astype(_BF16)


def _s5_prep(a_re, a_im, log_dt, b_re, b_im, c_re, c_im):
    g, p, h = SSM_GROUPS, SSM_STATE, SSM_GROUP_DIM
    xb = jnp.stack([b_re, b_im]).transpose(0, 1, 3, 2)
    xb = jnp.concatenate([xb, xb], axis=-1).reshape(2, g * h, LANES)
    xc = jnp.stack([c_re, c_im]).transpose(0, 3, 1, 2).reshape(2, p, g * h)
    xc = jnp.concatenate([xc, xc], axis=1)
    twice = lambda a: jnp.concatenate([a, a], axis=-1).reshape(g, 1, LANES)
    return pl.pallas_call(
        _s5_prep_kernel,
        out_shape=(jax.ShapeDtypeStruct((g, 1, LANES), _F32), jax.ShapeDtypeStruct((g, 1, LANES), _F32),
                   jax.ShapeDtypeStruct((2, HALF_ROWS, HALF_LANES), _BF16),
                   jax.ShapeDtypeStruct((2, HALF_LANES, HALF_ROWS), _BF16)),
        name="s5_prep",
    )(twice(a_re), twice(a_im), log_dt.reshape(g, 1, 1), xb, xc)


def _scan_half(xs_ref, s_re_ref, s_im_ref, ab_re_ref, ab_im_ref, half, n_batch, n_steps):
    lane = lax.broadcasted_iota(jnp.int32, (1, LANES), 1)

    def coeff(ref, c):
        row = jnp.where(lane < SSM_STATE, ref[2 * c], ref[2 * c + 1])
        return jnp.broadcast_to(row, (SUBLANES, LANES))

    def recur(r0, row_of_step, cls):
        chunks = [half * HALF_CHUNKS + cl for cl in cls]
        ar = [coeff(ab_re_ref, c) for c in chunks]
        ai = [coeff(ab_im_ref, c) for c in chunks]
        sr = [s_re_ref[pl.ds(r0, SUBLANES), c * LANES:(c + 1) * LANES] for c in chunks]
        si = [s_im_ref[pl.ds(r0, SUBLANES), c * LANES:(c + 1) * LANES] for c in chunks]
        for t in range(n_steps):
            rows = pl.ds(row_of_step(t), SUBLANES)
            for k, cl in enumerate(cls):
                lo = cl * CHUNK_LANES
                nr = ar[k] * sr[k] - ai[k] * si[k] + xs_ref[rows, lo:lo + LANES]
                ni = ar[k] * si[k] + ai[k] * sr[k] + xs_ref[rows, lo + LANES:lo + CHUNK_LANES]
                xs_ref[rows, lo:lo + LANES] = nr
                xs_ref[rows, lo + LANES:lo + CHUNK_LANES] = ni
                sr[k], si[k] = nr, ni
        for k, c in enumerate(chunks):
            s_re_ref[pl.ds(r0, SUBLANES), c * LANES:(c + 1) * LANES] = sr[k]
            s_im_ref[pl.ds(r0, SUBLANES), c * LANES:(c + 1) * LANES] = si[k]

    if n_batch == SUBLANES:
        recur(0, lambda t: t * n_batch, list(range(HALF_CHUNKS)))
    else:
        group = 4
        for cg in range(0, HALF_CHUNKS, group):
            def row_block(rb, carry, cg=cg):
                r0 = pl.multiple_of(rb * SUBLANES, SUBLANES)
                recur(r0, lambda t: pl.multiple_of(t * n_batch + r0, SUBLANES), list(range(cg, cg + group)))
                return carry
            lax.fori_loop(0, n_batch // SUBLANES, row_block, 0)


def _mixer_kernel(*refs, n_batch, n_steps, n_tiles, start_pos, zero_state, state_batch_minor):
    n_state_in = 0 if zero_state else 3
    x_ref = refs[0]
    hist0_ref, s0_re_ref, s0_im_ref = refs[1:1 + n_state_in] if n_state_in else (None, None, None)
    (norm_ref, w_in_ref, pool_w_ref, pool_scale_ref, ab_re_ref, ab_im_ref, b_ref, c_ref, d_ref, glu_w_ref,
     glu_b_ref, w_bp_ref, w_bs_ref, w_out_ref,
     h_ref, hist_out_ref, s_re_out_ref, s_im_out_ref,
     *scratch) = refs[1 + n_state_in:]
    i = pl.program_id(0)
    rows = n_batch * n_steps
    hist_rows = HIST_STEPS * n_batch

    scratch = list(scratch)
    hist_scratch = [scratch.pop(0)] if n_tiles > 1 else []
    x_stage_ref, proj_ref, ext_ref, xs_ref = scratch[:4]
    s_re_ref, s_im_ref = scratch[4:6] if state_batch_minor else (s_re_out_ref, s_im_out_ref)
    dma_in = n_tiles > 1
    x_sem = scratch[-1] if dma_in else None
    hist_ref = hist_scratch[0] if hist_scratch else ext_ref.at[0:hist_rows]
    @pl.when(i == 0)
    def _():
        if zero_state:
            hist_ref[...] = jnp.zeros((hist_rows, POOL_WIDTH), _F32)
            s_re_ref[...] = jnp.zeros(s_re_ref.shape, _F32)
            s_im_ref[...] = jnp.zeros(s_im_ref.shape, _F32)
        else:
            hist_ref[0:n_batch, :] = jnp.zeros((n_batch, POOL_WIDTH), _F32)
            hist_ref[n_batch:hist_rows, :] = hist0_ref[...].reshape(POOL_HIST * n_batch, POOL_WIDTH)
            s_re_ref[...] = s0_re_ref[...].T if state_batch_minor else s0_re_ref[...]
            s_im_ref[...] = s0_im_ref[...].T if state_batch_minor else s0_im_ref[...]

    if dma_in:
        slot = lax.rem(i, 2)

        def copies(slot_, step):
            t0 = pl.multiple_of(step * n_steps, n_steps)
            return [pltpu.make_async_copy(x_ref.at[b, pl.ds(t0, n_steps), :], x_stage_ref.at[slot_, :, b, :],
                                          x_sem.at[slot_]) for b in range(n_batch)]

        @pl.when(i == 0)
        def _():
            for c in copies(0, 0):
                c.start()
        for c in copies(slot, i):
            c.wait()

        @pl.when(i + 1 < n_tiles)
        def _():
            for c in copies(1 - slot, i + 1):
                c.start()
        x_rows = lambda: x_stage_ref[slot].reshape(rows, D_MODEL)
    else:
        x_stage_ref[...] = jnp.swapaxes(x_ref[...], 0, 1).reshape(rows, D_MODEL)
        x_rows = lambda: x_stage_ref[...]
    xn = _rmsnorm(x_rows(), norm_ref[...]).astype(_BF16)
    u_width = POOL_WIDTH + SSM_WIDTH
    proj_ref[...] = _dot(xn, w_in_ref[...])

    if hist_scratch:
        ext_ref[0:hist_rows, :] = hist_ref[...]
    ext_ref[hist_rows:, :] = proj_ref[:, 0:POOL_WIDTH]
    if hist_scratch:
        hist_ref[...] = ext_ref[rows:rows + hist_rows, :]
    diffs = []
    for g, w in enumerate(POOL_WINDOWS):
        lo = g * POOL_GROUP_DIM
        u_g = ext_ref[hist_rows:, lo:lo + POOL_GROUP_DIM]
        acc = u_g
        for j in range(1, w):
            r0 = (HIST_STEPS - j) * n_batch
            acc = acc + ext_ref[r0:r0 + rows, lo:lo + POOL_GROUP_DIM]
        if start_pos + 1 >= w:
            pooled = acc * (1.0 / w)
        else:
            row = lax.broadcasted_iota(jnp.int32, (rows, POOL_GROUP_DIM), 0)
            step = lax.shift_right_logical(row, int(math.log2(n_batch)))
            pos = start_pos + i * n_steps + step
            pooled = acc / jnp.minimum(w, pos + 1).astype(_F32)
        diffs.append(pooled - u_g)
    mixed = jnp.concatenate(
        [_dot(jnp.concatenate(diffs[0:2], axis=1).astype(_BF16), pool_w_ref[0]),
         _dot(jnp.concatenate(diffs[2:4], axis=1).astype(_BF16), pool_w_ref[1])], axis=1)
    a_out = (mixed * pool_scale_ref[...]).astype(_BF16)

    u_ssm = proj_ref[:, POOL_WIDTH:u_width]
    u_bf = u_ssm.astype(_BF16)
    for half in range(2):
        xs_ref[half] = _dot(u_bf[:, half * HALF_ROWS:(half + 1) * HALF_ROWS], b_ref[half])
    for half in range(2):
        _scan_half(xs_ref.at[half], s_re_ref, s_im_ref, ab_re_ref, ab_im_ref, half, n_batch, n_steps)
    ys = [_dot(xs_ref[half].astype(_BF16), c_ref[half]) for half in range(2)]
    y_ssm = jnp.concatenate(ys, axis=1) + d_ref[...] * u_ssm
    z = jax.nn.gelu(y_ssm)
    glu = jax.nn.sigmoid(_dot(z.astype(_BF16), glu_w_ref[...]) + glu_b_ref[...])
    b_out = (z * glu).astype(_BF16)

    merged = (jax.nn.sigmoid(proj_ref[:, u_width:u_width + D_MODEL]) * _dot(a_out, w_bp_ref[...])
              + jax.nn.sigmoid(proj_ref[:, u_width + D_MODEL:u_width + 2 * D_MODEL]) * _dot(b_out, w_bs_ref[...]))
    h_ref[...] = x_rows() + _dot(merged.astype(_BF16), w_out_ref[...])

    @pl.when(i == n_tiles - 1)
    def _():
        newest = hist_ref[n_batch:hist_rows, :] if hist_scratch else ext_ref[rows + n_batch:, :]
        hist_out_ref[...] = newest.reshape(POOL_HIST, n_batch, POOL_WIDTH)
        if state_batch_minor:
            s_re_out_ref[...] = s_re_ref[...].T
            s_im_out_ref[...] = s_im_ref[...].T


def _resident(shape):
    zeros = (0,) * len(shape)
    return pl.BlockSpec(shape, lambda i: zeros, pipeline_mode=pl.Buffered(1))


def _tiling(n_batch, n_steps_total, rows_per_step=ROW_TILE):
    n_steps = min(n_steps_total, rows_per_step // n_batch)
    n_tiles = n_steps_total // n_steps
    assert n_tiles * n_steps == n_steps_total and n_batch % SUBLANES == 0
    assert n_batch & (n_batch - 1) == 0
    return n_steps, n_tiles


def _mixer(x, state, weights, *, start_pos, state_batch_minor):
    n_batch, n_steps_total, _ = x.shape
    n_steps, n_tiles = _tiling(n_batch, n_steps_total)
    rows = n_steps * n_batch
    hist_rows = HIST_STEPS * n_batch
    state = () if state is None else tuple(state)
    state_shape = (N_STATES, n_batch) if state_batch_minor else (n_batch, N_STATES)
    kern = functools.partial(_mixer_kernel, n_batch=n_batch, n_steps=n_steps, n_tiles=n_tiles,
                             start_pos=start_pos, zero_state=not state, state_batch_minor=state_batch_minor)
    state_block = pl.BlockSpec(state_shape, lambda i: (0, 0))
    hist_window_bytes = (2 if state else 1) * POOL_HIST * n_batch * POOL_WIDTH * 4
    vmem_limit = min(VMEM_CAP_BYTES, VMEM_LIMIT_BYTES + hist_window_bytes - hist_window_bytes % (1 << 20))
    return pl.pallas_call(
        kern,
        grid=(n_tiles,),
        in_specs=[pl.BlockSpec(memory_space=pl.ANY) if n_tiles > 1
                  else pl.BlockSpec((n_batch, n_steps, D_MODEL), lambda i: (0, i, 0))]
        + [_resident(a.shape) for a in state + tuple(weights)],
        out_specs=(pl.BlockSpec((rows, D_MODEL), lambda i: (i, 0)),
                   pl.BlockSpec((POOL_HIST, n_batch, POOL_WIDTH), lambda i: (0, 0, 0)),
                   state_block, state_block),
        out_shape=(jax.ShapeDtypeStruct((n_steps_total * n_batch, D_MODEL), _F32),
                   jax.ShapeDtypeStruct((POOL_HIST, n_batch, POOL_WIDTH), _F32),
                   jax.ShapeDtypeStruct(state_shape, _F32),
                   jax.ShapeDtypeStruct(state_shape, _F32)),
        scratch_shapes=([pltpu.VMEM((hist_rows, POOL_WIDTH), _F32)] if n_tiles > 1 else [])
        + [pltpu.VMEM((2, n_steps, n_batch, D_MODEL) if n_tiles > 1 else (rows, D_MODEL), _F32),
           pltpu.VMEM((rows, IN_WIDTH), _F32),
           pltpu.VMEM((hist_rows + rows, POOL_WIDTH), _F32),
           pltpu.VMEM((2, rows, HALF_LANES), _F32)]
        + ([pltpu.VMEM((n_batch, N_STATES), _F32)] * 2 if state_batch_minor else [])
        + ([pltpu.SemaphoreType.DMA((2,))] if n_tiles > 1 else []),
        compiler_params=pltpu.CompilerParams(dimension_semantics=("arbitrary",), vmem_limit_bytes=vmem_limit),
        name="mixer",
    )(x, *state, *weights)


def _ffn_kernel(h_a_ref, h_b_ref, norm_ffn_ref, w_gate32_ref, w_up32_ref, w_down32_ref, norm_final_ref,
                y_a_ref, y_b_ref,
                w_gate_ref, w_up_ref, w_down_ref, gate_ref, up_ref, y_buf, sem, *, n_batch_a, n_sub, n_steps_a):
    i = pl.program_id(0)
    j = i - FFN_CAST_STEPS
    sub_rows = h_a_ref.shape[0] // n_sub
    sub_steps = sub_rows // n_batch_a
    steps = n_sub * sub_steps

    def tile(h):
        hn = _rmsnorm(h, norm_ffn_ref[...]).astype(_BF16)
        gate_ref[...] = _dot(hn, w_gate_ref[...])
        up_ref[...] = _dot(hn, w_up_ref[...])
        gate = gate_ref[...]
        f = (gate * jax.nn.sigmoid(gate) * up_ref[...]).astype(_BF16)
        return _rmsnorm(h + _dot(f, w_down_ref[...]), norm_final_ref[...])

    def copies(slot, step):
        t0 = step * steps if isinstance(step, int) else pl.multiple_of(step * steps, steps)
        return [pltpu.make_async_copy(y_buf.at[slot, :, b, :], y_a_ref.at[b, pl.ds(t0, steps), :], sem.at[slot])
                for b in range(n_batch_a)]

    @pl.when(i < FFN_CAST_STEPS)
    def _():
        for src, dst in ((w_gate32_ref, w_gate_ref), (w_up32_ref, w_up_ref), (w_down32_ref, w_down_ref)):
            r = src.shape[0]
            dst[pl.ds(pl.multiple_of(i * r, r), r), :] = src[...].astype(_BF16)

    @pl.when((j >= 0) & (j < n_steps_a))
    def _():
        slot = lax.rem(j, 2)
        for s in range(n_sub):
            y = tile(h_a_ref[s * sub_rows:(s + 1) * sub_rows, :])
            y_buf[slot, s * sub_steps:(s + 1) * sub_steps] = y.reshape(sub_steps, n_batch_a, D_MODEL)

        @pl.when(j > 0)
        def _():
            for c in copies(1 - slot, j - 1):
                c.wait()
        for c in copies(slot, j):
            c.start()

    @pl.when(j == n_steps_a)
    def _():
        n_batch_b, steps_b, _ = y_b_ref.shape
        y = tile(h_b_ref[...]).reshape(steps_b, n_batch_b, D_MODEL)
        y_b_ref[...] = jnp.swapaxes(y, 0, 1)
        for c in copies((n_steps_a - 1) % 2, n_steps_a - 1):
            c.wait()


def _ffn(h_a, h_b, norms, weights32, *, n_batch_a, n_batch_b):
    norm_ffn, norm_final = norms
    w_gate, w_up, w_down = weights32
    n_sub = FFN_SUB_TILES
    rows_a = ROW_TILE * n_sub
    n_steps_a = h_a.shape[0] // rows_a
    assert n_steps_a * rows_a == h_a.shape[0] and rows_a % n_batch_a == 0 and h_b.shape[0] == ROW_TILE
    assert all(w.shape[0] % (FFN_CAST_STEPS * BF16_ROWS) == 0 for w in weights32)
    t_a, t_b = h_a.shape[0] // n_batch_a, h_b.shape[0] // n_batch_b
    w_chunk = lambda w: pl.BlockSpec((w.shape[0] // FFN_CAST_STEPS, w.shape[1]),
                                     lambda i: (jnp.minimum(i, FFN_CAST_STEPS - 1), 0))
    return pl.pallas_call(
        functools.partial(_ffn_kernel, n_batch_a=n_batch_a, n_sub=n_sub, n_steps_a=n_steps_a),
        grid=(FFN_CAST_STEPS + n_steps_a + 1,),
        in_specs=[pl.BlockSpec((rows_a, D_MODEL), lambda i: (jnp.clip(i - FFN_CAST_STEPS, 0, n_steps_a - 1), 0)),
                  _resident(h_b.shape), _resident(norm_ffn.shape),
                  w_chunk(w_gate), w_chunk(w_up), w_chunk(w_down), _resident(norm_final.shape)],
        out_specs=(pl.BlockSpec(memory_space=pl.ANY),
                   pl.BlockSpec((n_batch_b, t_b, D_MODEL), lambda i: (0, 0, 0))),
        out_shape=(jax.ShapeDtypeStruct((n_batch_a, t_a, D_MODEL), _F32),
                   jax.ShapeDtypeStruct((n_batch_b, t_b, D_MODEL), _F32)),
        scratch_shapes=[pltpu.VMEM(w_gate.shape, _BF16), pltpu.VMEM(w_up.shape, _BF16), pltpu.VMEM(w_down.shape, _BF16),
                        pltpu.VMEM((ROW_TILE, D_FF), _F32), pltpu.VMEM((ROW_TILE, D_FF), _F32),
                        pltpu.VMEM((2, rows_a // n_batch_a, n_batch_a, D_MODEL), _F32),
                        pltpu.SemaphoreType.DMA((2,))],
        compiler_params=pltpu.CompilerParams(dimension_semantics=("arbitrary",),
                                             vmem_limit_bytes=VMEM_LIMIT_BYTES),
        name="ffn",
    )(h_a, h_b, norm_ffn, w_gate, w_up, w_down, norm_final)


def _pool_block_diag(pool_w):
    z = jnp.zeros((POOL_GROUP_DIM, POOL_GROUP_DIM), pool_w.dtype)
    blocks = [jnp.block([[pool_w[2 * k], z], [z, pool_w[2 * k + 1]]]) for k in range(2)]
    return jnp.stack(blocks).astype(_BF16)


def kernel(x_prompt, x_sample, state_pool, state_ssm_re, state_ssm_im, norm_mix, w_in, pool_w, pool_scale, ssm_a_re, ssm_a_im, ssm_log_dt, ssm_b_re, ssm_b_im, ssm_c_re, ssm_c_im, ssm_d, glu_w, glu_b, w_branch_pool, w_branch_ssm, w_out, norm_ffn, ffn_w_gate, ffn_w_up, ffn_w_down, norm_final):
    assert w_in.shape[0] == 1, "single-layer trunk"
    ab_re, ab_im, b_mat, c_mat = _s5_prep(ssm_a_re[0], ssm_a_im[0], ssm_log_dt[0], ssm_b_re[0], ssm_b_im[0],
                                          ssm_c_re[0], ssm_c_im[0])
    mixer_w = (norm_mix[0].reshape(1, D_MODEL), w_in[0].astype(_BF16), _pool_block_diag(pool_w[0]),
               pool_scale[0].reshape(1, POOL_WIDTH), ab_re, ab_im, b_mat, c_mat, ssm_d[0].reshape(1, SSM_WIDTH),
               glu_w[0].astype(_BF16), glu_b[0].reshape(1, SSM_WIDTH), w_branch_pool[0].astype(_BF16),
               w_branch_ssm[0].astype(_BF16), w_out[0].astype(_BF16))
    ffn_norms = (norm_ffn[0].reshape(1, D_MODEL), norm_final.reshape(1, D_MODEL))

    def mix(x, state, start_pos):
        b = x.shape[0]
        batch_minor = b % LANES == 0
        if state is not None:
            flat = ((lambda a: a.transpose(1, 2, 0).reshape(N_STATES, b)) if batch_minor
                    else (lambda a: a.reshape(b, N_STATES)))
            state = (state[0].transpose(1, 0, 2), flat(state[1]), flat(state[2]))
        h_rows, new_hist, s_re, s_im = _mixer(x, state, mixer_w, start_pos=start_pos, state_batch_minor=batch_minor)
        unflat = ((lambda a: a.reshape(SSM_GROUPS, SSM_STATE, b).transpose(2, 0, 1)[None]) if batch_minor
                  else (lambda a: a.reshape(1, b, SSM_GROUPS, SSM_STATE)))
        return h_rows, new_hist.transpose(1, 0, 2)[None], unflat(s_re), unflat(s_im)

    h_p, pool_p, re_p, im_p = mix(x_prompt, None, 0)
    h_s, pool_s, re_s, im_s = mix(x_sample, (state_pool[0], state_ssm_re[0], state_ssm_im[0]), PAST_LEN)
    y_p, y_s = _ffn(h_p, h_s, ffn_norms, (ffn_w_gate[0], ffn_w_up[0], ffn_w_down[0]),
                    n_batch_a=x_prompt.shape[0], n_batch_b=x_sample.shape[0])
    return (y_p, y_s, pool_p, re_p, im_p, pool_s, re_s, im_s)
```

```python
import functools
import math

import jax
import jax.numpy as jnp
from jax import lax
from jax.experimental import pallas as pl
from jax.experimental.pallas import tpu as pltpu

D_MODEL = 1024
PAST_LEN = 16384
POOL_WIDTH = 512
POOL_WINDOWS = (2, 4, 8, 16)
POOL_GROUP_DIM = 128
POOL_HIST = 15
HIST_STEPS = POOL_HIST + 1
SSM_WIDTH = 512
SSM_GROUPS = 32
SSM_GROUP_DIM = 16
SSM_STATE = 64
N_STATES = SSM_GROUPS * SSM_STATE
IN_WIDTH = POOL_WIDTH + SSM_WIDTH + 2 * D_MODEL
D_FF = 2816
RMS_EPS = 1e-6

SUBLANES = 8
BF16_ROWS = 16
LANES = 128
MXU_DIM = 256
CHUNK_LANES = 2 * LANES
N_CHUNKS = N_STATES // LANES
HALF_CHUNKS = N_CHUNKS // 2
HALF_LANES = HALF_CHUNKS * CHUNK_LANES
HALF_ROWS = MXU_DIM
ROW_TILE = 512
FFN_SUB_TILES = 2
FFN_CAST_STEPS = 8
VMEM_LIMIT_BYTES = 56 * 1024 * 1024
VMEM_CAP_BYTES = 60 * 1024 * 1024

_F32 = jnp.float32
_BF16 = jnp.bfloat16


def _rmsnorm(x, g):
    return x * lax.rsqrt(jnp.mean(x * x, axis=-1, keepdims=True) + RMS_EPS) * g


def _dot(a, b):
    return jnp.dot(a, b, preferred_element_type=_F32)


def _prep_kernel(a_re_ref, a_im_ref, log_dt_ref, b_re_ref, b_im_ref, c_re_ref, c_im_ref, pool_w_ref,
                 ab_re_ref, ab_im_ref, b_mat_ref, c_mat_ref, pool_mat_ref):
    g_n, h_n, p_n = SSM_GROUPS, SSM_GROUP_DIM, SSM_STATE
    twice = lambda v: jnp.concatenate([v, v], axis=-1)

    diag = lax.broadcasted_iota(jnp.int32, (g_n, g_n), 0) == lax.broadcasted_iota(jnp.int32, (g_n, g_n), 1)
    log_dt = jnp.sum(jnp.where(diag, jnp.broadcast_to(log_dt_ref[...], (g_n, g_n)), 0.0), axis=1, keepdims=True)
    dt = jnp.exp(log_dt)
    a_re = a_re_ref[...]
    a_im = a_im_ref[...]
    mag = jnp.exp(a_re * dt)
    ab_re = mag * jnp.cos(a_im * dt)
    ab_im = mag * jnp.sin(a_im * dt)
    num_re = ab_re - 1.0
    num_im = ab_im
    den = a_re * a_re + a_im * a_im
    coef_re = ((num_re * a_re + num_im * a_im) / den)[:, None, :]
    coef_im = ((num_im * a_re - num_re * a_im) / den)[:, None, :]
    ab_re_ref[...] = twice(ab_re)
    ab_im_ref[...] = twice(ab_im)

    b_re = b_re_ref[...]
    b_im = b_im_ref[...]
    bp = [twice(v).reshape(g_n * h_n, LANES)
          for v in (coef_re * b_re - coef_im * b_im, coef_re * b_im + coef_im * b_re)]
    row = lax.broadcasted_iota(jnp.int32, (HALF_ROWS, LANES), 0)
    lane = lax.broadcasted_iota(jnp.int32, (HALF_ROWS, LANES), 1)
    own_lanes = (lane >> 6) == ((row >> 4) & 1)
    for half in range(2):
        for part in range(2):
            mine = jnp.where(own_lanes, bp[part][half * HALF_ROWS:(half + 1) * HALF_ROWS, :], 0.0)
            for cl in range(HALF_CHUNKS):
                blk = jnp.where((row >> 5) == cl, mine, 0.0)
                lo = cl * CHUNK_LANES + part * LANES
                b_mat_ref[half, :, lo:lo + LANES] = blk.astype(_BF16)

    row_c = lax.broadcasted_iota(jnp.int32, (LANES, HALF_ROWS), 0)
    col_c = lax.broadcasted_iota(jnp.int32, (LANES, HALF_ROWS), 1)
    own_cols = (row_c >> 6) == ((col_c >> 4) & 1)
    for part, c_ref in enumerate((c_re_ref, c_im_ref)):
        ct = c_ref[...].reshape(g_n * h_n, p_n).T
        ct = jnp.concatenate([ct, ct], axis=0)
        if part == 1:
            ct = -ct
        for half in range(2):
            mine = jnp.where(own_cols, ct[:, half * HALF_ROWS:(half + 1) * HALF_ROWS], 0.0)
            for cl in range(HALF_CHUNKS):
                blk = jnp.where((col_c >> 5) == cl, mine, 0.0)
                lo = cl * CHUNK_LANES + part * LANES
                c_mat_ref[half, lo:lo + LANES, :] = blk.astype(_BF16)

    pool_mat_ref[...] = jnp.zeros(pool_mat_ref.shape, _BF16)
    for g in range(len(POOL_WINDOWS)):
        lo = (g % 2) * POOL_GROUP_DIM
        pool_mat_ref[g // 2, lo:lo + POOL_GROUP_DIM, lo:lo + POOL_GROUP_DIM] = pool_w_ref[g].astype(_BF16)


def _prep(a_re, a_im, log_dt, b_re, b_im, c_re, c_im, pool_w):
    return pl.pallas_call(
        _prep_kernel,
        out_shape=(jax.ShapeDtypeStruct((SSM_GROUPS, LANES), _F32), jax.ShapeDtypeStruct((SSM_GROUPS, LANES), _F32),
                   jax.ShapeDtypeStruct((2, HALF_ROWS, HALF_LANES), _BF16),
                   jax.ShapeDtypeStruct((2, HALF_LANES, HALF_ROWS), _BF16),
                   jax.ShapeDtypeStruct((len(POOL_WINDOWS) // 2, MXU_DIM, MXU_DIM), _BF16)),
        name="prep",
    )(a_re, a_im, log_dt, b_re.transpose(0, 2, 1), b_im.transpose(0, 2, 1), c_re, c_im, pool_w)


def _scan_half(xs_ref, s_re_ref, s_im_ref, ab_re_ref, ab_im_ref, half, n_batch, n_steps):
    lane = lax.broadcasted_iota(jnp.int32, (1, LANES), 1)

    def coeff(ref, c):
        row = jnp.where(lane < SSM_STATE, ref[2 * c:2 * c + 1, :], ref[2 * c + 1:2 * c + 2, :])
        return jnp.broadcast_to(row, (SUBLANES, LANES))

    def recur(r0, row_of_step, cls):
        chunks = [half * HALF_CHUNKS + cl for cl in cls]
        ar = [coeff(ab_re_ref, c) for c in chunks]
        ai = [coeff(ab_im_ref, c) for c in chunks]
        sr = [s_re_ref[pl.ds(r0, SUBLANES), c * LANES:(c + 1) * LANES] for c in chunks]
        si = [s_im_ref[pl.ds(r0, SUBLANES), c * LANES:(c + 1) * LANES] for c in chunks]
        for t in range(n_steps):
            rows = pl.ds(row_of_step(t), SUBLANES)
            for k, cl in enumerate(cls):
                lo = cl * CHUNK_LANES
                nr = ar[k] * sr[k] - ai[k] * si[k] + xs_ref[rows, lo:lo + LANES]
                ni = ar[k] * si[k] + ai[k] * sr[k] + xs_ref[rows, lo + LANES:lo + CHUNK_LANES]
                xs_ref[rows, lo:lo + LANES] = nr
                xs_ref[rows, lo + LANES:lo + CHUNK_LANES] = ni
                sr[k], si[k] = nr, ni
        for k, c in enumerate(chunks):
            s_re_ref[pl.ds(r0, SUBLANES), c * LANES:(c + 1) * LANES] = sr[k]
            s_im_ref[pl.ds(r0, SUBLANES), c * LANES:(c + 1) * LANES] = si[k]

    if n_batch == SUBLANES:
        recur(0, lambda t: t * n_batch, list(range(HALF_CHUNKS)))
    else:
        group = 4
        for cg in range(0, HALF_CHUNKS, group):
            def row_block(rb, carry, cg=cg):
                r0 = pl.multiple_of(rb * SUBLANES, SUBLANES)
                recur(r0, lambda t: pl.multiple_of(t * n_batch + r0, SUBLANES), list(range(cg, cg + group)))
                return carry
            lax.fori_loop(0, n_batch // SUBLANES, row_block, 0)


def _mixer_kernel(*refs, n_batch, n_steps, n_tiles, start_pos, zero_state, state_batch_minor):
    n_state_in = 0 if zero_state else 3
    x_ref = refs[0]
    hist0_ref, s0_re_ref, s0_im_ref = refs[1:1 + n_state_in] if n_state_in else (None, None, None)
    (norm_ref, w_in_ref, pool_w_ref, pool_scale_ref, ab_re_ref, ab_im_ref, b_ref, c_ref, d_ref, glu_w_ref,
     glu_b_ref, w_bp_ref, w_bs_ref, w_out_ref,
     h_ref, hist_out_ref, s_re_out_ref, s_im_out_ref,
     *scratch) = refs[1 + n_state_in:]
    i = pl.program_id(0)
    rows = n_batch * n_steps
    hist_rows = HIST_STEPS * n_batch

    scratch = list(scratch)
    hist_scratch = [scratch.pop(0)] if n_tiles > 1 else []
    x_stage_ref, proj_ref, ext_ref, xs_ref = scratch[:4]
    s_re_ref, s_im_ref = scratch[4:6] if state_batch_minor else (s_re_out_ref, s_im_out_ref)
    dma_in = n_tiles > 1
    x_sem = scratch[-1] if dma_in else None
    hist_ref = hist_scratch[0] if hist_scratch else ext_ref.at[0:hist_rows]
    @pl.when(i == 0)
    def _():
        if zero_state:
            hist_ref[...] = jnp.zeros((hist_rows, POOL_WIDTH), _F32)
            s_re_ref[...] = jnp.zeros(s_re_ref.shape, _F32)
            s_im_ref[...] = jnp.zeros(s_im_ref.shape, _F32)
        else:
            hist_ref[0:n_batch, :] = jnp.zeros((n_batch, POOL_WIDTH), _F32)
            hist_ref[n_batch:hist_rows, :] = hist0_ref[...].reshape(POOL_HIST * n_batch, POOL_WIDTH)
            s_re_ref[...] = s0_re_ref[...].T if state_batch_minor else s0_re_ref[...]
            s_im_ref[...] = s0_im_ref[...].T if state_batch_minor else s0_im_ref[...]

    if dma_in:
        slot = lax.rem(i, 2)

        def copies(slot_, step):
            t0 = pl.multiple_of(step * n_steps, n_steps)
            return [pltpu.make_async_copy(x_ref.at[b, pl.ds(t0, n_steps), :], x_stage_ref.at[slot_, :, b, :],
                                          x_sem.at[slot_]) for b in range(n_batch)]

        @pl.when(i == 0)
        def _():
            for c in copies(0, 0):
                c.start()
        for c in copies(slot, i):
            c.wait()

        @pl.when(i + 1 < n_tiles)
        def _():
            for c in copies(1 - slot, i + 1):
                c.start()
        x_rows = lambda: x_stage_ref[slot].reshape(rows, D_MODEL)
    else:
        x_stage_ref[...] = jnp.swapaxes(x_ref[...], 0, 1).reshape(rows, D_MODEL)
        x_rows = lambda: x_stage_ref[...]
    xn = _rmsnorm(x_rows(), norm_ref[...]).astype(_BF16)
    u_width = POOL_WIDTH + SSM_WIDTH
    proj_ref[...] = _dot(xn, w_in_ref[...])

    if hist_scratch:
        ext_ref[0:hist_rows, :] = hist_ref[...]
    ext_ref[hist_rows:, :] = proj_ref[:, 0:POOL_WIDTH]
    if hist_scratch:
        hist_ref[...] = ext_ref[rows:rows + hist_rows, :]
    diffs = []
    for g, w in enumerate(POOL_WINDOWS):
        lo = g * POOL_GROUP_DIM
        u_g = ext_ref[hist_rows:, lo:lo + POOL_GROUP_DIM]
        acc = u_g
        for j in range(1, w):
            r0 = (HIST_STEPS - j) * n_batch
            acc = acc + ext_ref[r0:r0 + rows, lo:lo + POOL_GROUP_DIM]
        if start_pos + 1 >= w:
            pooled = acc * (1.0 / w)
        else:
            row = lax.broadcasted_iota(jnp.int32, (rows, POOL_GROUP_DIM), 0)
            step = lax.shift_right_logical(row, int(math.log2(n_batch)))
            pos = start_pos + i * n_steps + step
            pooled = acc / jnp.minimum(w, pos + 1).astype(_F32)
        diffs.append(pooled - u_g)
    mixed = jnp.concatenate(
        [_dot(jnp.concatenate(diffs[0:2], axis=1).astype(_BF16), pool_w_ref[0]),
         _dot(jnp.concatenate(diffs[2:4], axis=1).astype(_BF16), pool_w_ref[1])], axis=1)
    a_out = (mixed * pool_scale_ref[...]).astype(_BF16)

    u_ssm = proj_ref[:, POOL_WIDTH:u_width]
    u_bf = u_ssm.astype(_BF16)
    for half in range(2):
        xs_ref[half] = _dot(u_bf[:, half * HALF_ROWS:(half + 1) * HALF_ROWS], b_ref[half])
    for half in range(2):
        _scan_half(xs_ref.at[half], s_re_ref, s_im_ref, ab_re_ref, ab_im_ref, half, n_batch, n_steps)
    ys = [_dot(xs_ref[half].astype(_BF16), c_ref[half]) for half in range(2)]
    y_ssm = jnp.concatenate(ys, axis=1) + d_ref[...] * u_ssm
    z = jax.nn.gelu(y_ssm)
    glu = jax.nn.sigmoid(_dot(z.astype(_BF16), glu_w_ref[...]) + glu_b_ref[...])
    b_out = (z * glu).astype(_BF16)

    merged = (jax.nn.sigmoid(proj_ref[:, u_width:u_width + D_MODEL]) * _dot(a_out, w_bp_ref[...])
              + jax.nn.sigmoid(proj_ref[:, u_width + D_MODEL:u_width + 2 * D_MODEL]) * _dot(b_out, w_bs_ref[...]))
    h_ref[...] = x_rows() + _dot(merged.astype(_BF16), w_out_ref[...])

    @pl.when(i == n_tiles - 1)
    def _():
        newest = hist_ref[n_batch:hist_rows, :] if hist_scratch else ext_ref[rows + n_batch:, :]
        hist_out_ref[...] = newest.reshape(POOL_HIST, n_batch, POOL_WIDTH)
        if state_batch_minor:
            s_re_out_ref[...] = s_re_ref[...].T
            s_im_out_ref[...] = s_im_ref[...].T


def _resident(shape):
    zeros = (0,) * len(shape)
    return pl.BlockSpec(shape, lambda i: zeros, pipeline_mode=pl.Buffered(1))


def _tiling(n_batch, n_steps_total, rows_per_step=ROW_TILE):
    n_steps = min(n_steps_total, rows_per_step // n_batch)
    n_tiles = n_steps_total // n_steps
    assert n_tiles * n_steps == n_steps_total and n_batch % SUBLANES == 0
    assert n_batch & (n_batch - 1) == 0
    return n_steps, n_tiles


def _mixer(x, state, weights, *, start_pos, state_batch_minor):
    n_batch, n_steps_total, _ = x.shape
    n_steps, n_tiles = _tiling(n_batch, n_steps_total)
    rows = n_steps * n_batch
    hist_rows = HIST_STEPS * n_batch
    state = () if state is None else tuple(state)
    state_shape = (N_STATES, n_batch) if state_batch_minor else (n_batch, N_STATES)
    kern = functools.partial(_mixer_kernel, n_batch=n_batch, n_steps=n_steps, n_tiles=n_tiles,
                             start_pos=start_pos, zero_state=not state, state_batch_minor=state_batch_minor)
    state_block = pl.BlockSpec(state_shape, lambda i: (0, 0))
    hist_window_bytes = (2 if state else 1) * POOL_HIST * n_batch * POOL_WIDTH * 4
    vmem_limit = min(VMEM_CAP_BYTES, VMEM_LIMIT_BYTES + hist_window_bytes - hist_window_bytes % (1 << 20))
    return pl.pallas_call(
        kern,
        grid=(n_tiles,),
        in_specs=[pl.BlockSpec(memory_space=pl.ANY) if n_tiles > 1
                  else pl.BlockSpec((n_batch, n_steps, D_MODEL), lambda i: (0, i, 0))]
        + [_resident(a.shape) for a in state + tuple(weights)],
        out_specs=(pl.BlockSpec((rows, D_MODEL), lambda i: (i, 0)),
                   pl.BlockSpec((POOL_HIST, n_batch, POOL_WIDTH), lambda i: (0, 0, 0)),
                   state_block, state_block),
        out_shape=(jax.ShapeDtypeStruct((n_steps_total * n_batch, D_MODEL), _F32),
                   jax.ShapeDtypeStruct((POOL_HIST, n_batch, POOL_WIDTH), _F32),
                   jax.ShapeDtypeStruct(state_shape, _F32),
                   jax.ShapeDtypeStruct(state_shape, _F32)),
        scratch_shapes=([pltpu.VMEM((hist_rows, POOL_WIDTH), _F32)] if n_tiles > 1 else [])
        + [pltpu.VMEM((2, n_steps, n_batch, D_MODEL) if n_tiles > 1 else (rows, D_MODEL), _F32),
           pltpu.VMEM((rows, IN_WIDTH), _F32),
           pltpu.VMEM((hist_rows + rows, POOL_WIDTH), _F32),
           pltpu.VMEM((2, rows, HALF_LANES), _F32)]
        + ([pltpu.VMEM((n_batch, N_STATES), _F32)] * 2 if state_batch_minor else [])
        + ([pltpu.SemaphoreType.DMA((2,))] if n_tiles > 1 else []),
        compiler_params=pltpu.CompilerParams(dimension_semantics=("arbitrary",), vmem_limit_bytes=vmem_limit),
        name="mixer",
    )(x, *state, *weights)


def _ffn_kernel(h_a_ref, h_b_ref, norm_ffn_ref, w_gate32_ref, w_up32_ref, w_down32_ref, norm_final_ref,
                y_a_ref, y_b_ref,
                w_gate_ref, w_up_ref, w_down_ref, gate_ref, up_ref, y_buf, sem, *, n_batch_a, n_sub, n_steps_a):
    i = pl.program_id(0)
    j = i - FFN_CAST_STEPS
    sub_rows = h_a_ref.shape[0] // n_sub
    sub_steps = sub_rows // n_batch_a
    steps = n_sub * sub_steps

    def tile(h):
        hn = _rmsnorm(h, norm_ffn_ref[...]).astype(_BF16)
        gate_ref[...] = _dot(hn, w_gate_ref[...])
        up_ref[...] = _dot(hn, w_up_ref[...])
        gate = gate_ref[...]
        f = (gate * jax.nn.sigmoid(gate) * up_ref[...]).astype(_BF16)
        return _rmsnorm(h + _dot(f, w_down_ref[...]), norm_final_ref[...])

    def copies(slot, step):
        t0 = step * steps if isinstance(step, int) else pl.multiple_of(step * steps, steps)
        return [pltpu.make_async_copy(y_buf.at[slot, :, b, :], y_a_ref.at[b, pl.ds(t0, steps), :], sem.at[slot])
                for b in range(n_batch_a)]

    @pl.when(i < FFN_CAST_STEPS)
    def _():
        for src, dst in ((w_gate32_ref, w_gate_ref), (w_up32_ref, w_up_ref), (w_down32_ref, w_down_ref)):
            r = src.shape[0]
            dst[pl.ds(pl.multiple_of(i * r, r), r), :] = src[...].astype(_BF16)

    @pl.when((j >= 0) & (j < n_steps_a))
    def _():
        slot = lax.rem(j, 2)
        for s in range(n_sub):
            y = tile(h_a_ref[s * sub_rows:(s + 1) * sub_rows, :])
            y_buf[slot, s * sub_steps:(s + 1) * sub_steps] = y.reshape(sub_steps, n_batch_a, D_MODEL)

        @pl.when(j > 0)
        def _():
            for c in copies(1 - slot, j - 1):
                c.wait()
        for c in copies(slot, j):
            c.start()

    @pl.when(j == n_steps_a)
    def _():
        n_batch_b, steps_b, _ = y_b_ref.shape
        y = tile(h_b_ref[...]).reshape(steps_b, n_batch_b, D_MODEL)
        y_b_ref[...] = jnp.swapaxes(y, 0, 1)
        for c in copies((n_steps_a - 1) % 2, n_steps_a - 1):
            c.wait()


def _ffn(h_a, h_b, norms, weights32, *, n_batch_a, n_batch_b):
    norm_ffn, norm_final = norms
    w_gate, w_up, w_down = weights32
    n_sub = FFN_SUB_TILES
    rows_a = ROW_TILE * n_sub
    n_steps_a = h_a.shape[0] // rows_a
    assert n_steps_a * rows_a == h_a.shape[0] and rows_a % n_batch_a == 0 and h_b.shape[0] == ROW_TILE
    assert all(w.shape[0] % (FFN_CAST_STEPS * BF16_ROWS) == 0 for w in weights32)
    t_a, t_b = h_a.shape[0] // n_batch_a, h_b.shape[0] // n_batch_b
    w_chunk = lambda w: pl.BlockSpec((w.shape[0] // FFN_CAST_STEPS, w.shape[1]),
                                     lambda i: (jnp.minimum(i, FFN_CAST_STEPS - 1), 0))
    return pl.pallas_call(
        functools.partial(_ffn_kernel, n_batch_a=n_batch_a, n_sub=n_sub, n_steps_a=n_steps_a),
        grid=(FFN_CAST_STEPS + n_steps_a + 1,),
        in_specs=[pl.BlockSpec((rows_a, D_MODEL), lambda i: (jnp.clip(i - FFN_CAST_STEPS, 0, n_steps_a - 1), 0)),
                  _resident(h_b.shape), _resident(norm_ffn.shape),
                  w_chunk(w_gate), w_chunk(w_up), w_chunk(w_down), _resident(norm_final.shape)],
        out_specs=(pl.BlockSpec(memory_space=pl.ANY),
                   pl.BlockSpec((n_batch_b, t_b, D_MODEL), lambda i: (0, 0, 0))),
        out_shape=(jax.ShapeDtypeStruct((n_batch_a, t_a, D_MODEL), _F32),
                   jax.ShapeDtypeStruct((n_batch_b, t_b, D_MODEL), _F32)),
        scratch_shapes=[pltpu.VMEM(w_gate.shape, _BF16), pltpu.VMEM(w_up.shape, _BF16), pltpu.VMEM(w_down.shape, _BF16),
                        pltpu.VMEM((ROW_TILE, D_FF), _F32), pltpu.VMEM((ROW_TILE, D_FF), _F32),
                        pltpu.VMEM((2, rows_a // n_batch_a, n_batch_a, D_MODEL), _F32),
                        pltpu.SemaphoreType.DMA((2,))],
        compiler_params=pltpu.CompilerParams(dimension_semantics=("arbitrary",),
                                             vmem_limit_bytes=VMEM_CAP_BYTES),
        name="ffn",
    )(h_a, h_b, norm_ffn, w_gate, w_up, w_down, norm_final)


def kernel(x_prompt, x_sample, state_pool, state_ssm_re, state_ssm_im, norm_mix, w_in, pool_w, pool_scale, ssm_a_re, ssm_a_im, ssm_log_dt, ssm_b_re, ssm_b_im, ssm_c_re, ssm_c_im, ssm_d, glu_w, glu_b, w_branch_pool, w_branch_ssm, w_out, norm_ffn, ffn_w_gate, ffn_w_up, ffn_w_down, norm_final):
    assert w_in.shape[0] == 1, "single-layer trunk"
    ab_re, ab_im, b_mat, c_mat, pool_mat = _prep(ssm_a_re[0], ssm_a_im[0], ssm_log_dt, ssm_b_re[0], ssm_b_im[0],
                                                 ssm_c_re[0], ssm_c_im[0], pool_w[0])
    mixer_w = (norm_mix[0].reshape(1, D_MODEL), w_in[0].astype(_BF16), pool_mat,
               pool_scale[0].reshape(1, POOL_WIDTH), ab_re, ab_im, b_mat, c_mat, ssm_d[0].reshape(1, SSM_WIDTH),
               glu_w[0].astype(_BF16), glu_b[0].reshape(1, SSM_WIDTH), w_branch_pool[0].astype(_BF16),
               w_branch_ssm[0].astype(_BF16), w_out[0].astype(_BF16))
    ffn_norms = (norm_ffn[0].reshape(1, D_MODEL), norm_final.reshape(1, D_MODEL))

    def mix(x, state, start_pos):
        b = x.shape[0]
        batch_minor = b % LANES == 0
        if state is not None:
            flat = ((lambda a: a.transpose(1, 2, 0).reshape(N_STATES, b)) if batch_minor
                    else (lambda a: a.reshape(b, N_STATES)))
            state = (state[0].transpose(1, 0, 2), flat(state[1]), flat(state[2]))
        h_rows, new_hist, s_re, s_im = _mixer(x, state, mixer_w, start_pos=start_pos, state_batch_minor=batch_minor)
        unflat = ((lambda a: a.reshape(SSM_GROUPS, SSM_STATE, b).transpose(2, 0, 1)[None]) if batch_minor
                  else (lambda a: a.reshape(1, b, SSM_GROUPS, SSM_STATE)))
        return h_rows, new_hist.transpose(1, 0, 2)[None], unflat(s_re), unflat(s_im)

    h_p, pool_p, re_p, im_p = mix(x_prompt, None, 0)
    h_s, pool_s, re_s, im_s = mix(x_sample, (state_pool[0], state_ssm_re[0], state_ssm_im[0]), PAST_LEN)
    y_p, y_s = _ffn(h_p, h_s, ffn_norms, (ffn_w_gate[0], ffn_w_up[0], ffn_w_down[0]),
                    n_batch_a=x_prompt.shape[0], n_batch_b=x_sample.shape[0])
    return (y_p, y_s, pool_p, re_p, im_p, pool_s, re_s, im_s)
```

```python
import functools
import math

import jax
import jax.numpy as jnp
from jax import lax
from jax.experimental import pallas as pl
from jax.experimental.pallas import tpu as pltpu

D_MODEL = 1024
PAST_LEN = 16384
POOL_WIDTH = 512
POOL_WINDOWS = (2, 4, 8, 16)
POOL_GROUP_DIM = 128
POOL_HIST = 15
HIST_STEPS = POOL_HIST + 1
SSM_WIDTH = 512
SSM_GROUPS = 32
SSM_GROUP_DIM = 16
SSM_STATE = 64
N_STATES = SSM_GROUPS * SSM_STATE
IN_WIDTH = POOL_WIDTH + SSM_WIDTH + 2 * D_MODEL
D_FF = 2816
RMS_EPS = 1e-6

SUBLANES = 8
BF16_ROWS = 16
LANES = 128
MXU_DIM = 256
CHUNK_LANES = 2 * LANES
N_CHUNKS = N_STATES // LANES
HALF_CHUNKS = N_CHUNKS // 2
HALF_LANES = HALF_CHUNKS * CHUNK_LANES
HALF_ROWS = MXU_DIM
ROW_TILE = 512
PREP_STEPS = 8
N_STRUCTURED_IN = 8
N_STRUCTURED_OUT = 5
FFN_SUB_TILES = 2
FFN_CAST_STEPS = 8
VMEM_LIMIT_BYTES = 56 * 1024 * 1024
VMEM_CAP_BYTES = 60 * 1024 * 1024

_F32 = jnp.float32
_BF16 = jnp.bfloat16


def _rmsnorm(x, g):
    return x * lax.rsqrt(jnp.mean(x * x, axis=-1, keepdims=True) + RMS_EPS) * g


def _dot(a, b):
    return jnp.dot(a, b, preferred_element_type=_F32)


def _structured_weights(a_re_ref, a_im_ref, log_dt_ref, b_re_ref, b_im_ref, c_re_ref, c_im_ref, pool_w_ref,
                        ab_re_ref, ab_im_ref, b_mat_ref, c_mat_ref, pool_mat_ref):
    g_n, h_n, p_n = SSM_GROUPS, SSM_GROUP_DIM, SSM_STATE
    twice = lambda v: jnp.concatenate([v, v], axis=-1)

    diag = lax.broadcasted_iota(jnp.int32, (g_n, g_n), 0) == lax.broadcasted_iota(jnp.int32, (g_n, g_n), 1)
    log_dt = jnp.sum(jnp.where(diag, jnp.broadcast_to(log_dt_ref[...], (g_n, g_n)), 0.0), axis=1, keepdims=True)
    dt = jnp.exp(log_dt)
    a_re = a_re_ref[...]
    a_im = a_im_ref[...]
    mag = jnp.exp(a_re * dt)
    ab_re = mag * jnp.cos(a_im * dt)
    ab_im = mag * jnp.sin(a_im * dt)
    num_re = ab_re - 1.0
    num_im = ab_im
    den = a_re * a_re + a_im * a_im
    coef_re = ((num_re * a_re + num_im * a_im) / den)[:, None, :]
    coef_im = ((num_im * a_re - num_re * a_im) / den)[:, None, :]
    ab_re_ref[...] = twice(ab_re)
    ab_im_ref[...] = twice(ab_im)

    b_re = b_re_ref[...]
    b_im = b_im_ref[...]
    bp = [twice(v).reshape(g_n * h_n, LANES)
          for v in (coef_re * b_re - coef_im * b_im, coef_re * b_im + coef_im * b_re)]
    row = lax.broadcasted_iota(jnp.int32, (HALF_ROWS, LANES), 0)
    lane = lax.broadcasted_iota(jnp.int32, (HALF_ROWS, LANES), 1)
    own_lanes = (lane >> 6) == ((row >> 4) & 1)
    for half in range(2):
        for part in range(2):
            mine = jnp.where(own_lanes, bp[part][half * HALF_ROWS:(half + 1) * HALF_ROWS, :], 0.0)
            for cl in range(HALF_CHUNKS):
                blk = jnp.where((row >> 5) == cl, mine, 0.0)
                lo = cl * CHUNK_LANES + part * LANES
                b_mat_ref[half, :, lo:lo + LANES] = blk.astype(_BF16)

    row_c = lax.broadcasted_iota(jnp.int32, (LANES, HALF_ROWS), 0)
    col_c = lax.broadcasted_iota(jnp.int32, (LANES, HALF_ROWS), 1)
    own_cols = (row_c >> 6) == ((col_c >> 4) & 1)
    for part, c_ref in enumerate((c_re_ref, c_im_ref)):
        ct = c_ref[...].reshape(g_n * h_n, p_n).T
        ct = jnp.concatenate([ct, ct], axis=0)
        if part == 1:
            ct = -ct
        for half in range(2):
            mine = jnp.where(own_cols, ct[:, half * HALF_ROWS:(half + 1) * HALF_ROWS], 0.0)
            for cl in range(HALF_CHUNKS):
                blk = jnp.where((col_c >> 5) == cl, mine, 0.0)
                lo = cl * CHUNK_LANES + part * LANES
                c_mat_ref[half, lo:lo + LANES, :] = blk.astype(_BF16)

    pool_mat_ref[...] = jnp.zeros(pool_mat_ref.shape, _BF16)
    for g in range(len(POOL_WINDOWS)):
        lo = (g % 2) * POOL_GROUP_DIM
        pool_mat_ref[g // 2, lo:lo + POOL_GROUP_DIM, lo:lo + POOL_GROUP_DIM] = pool_w_ref[g].astype(_BF16)


def _prep_kernel(*refs):
    n_dense = (len(refs) - N_STRUCTURED_IN - N_STRUCTURED_OUT) // 2
    dense_in = refs[N_STRUCTURED_IN:N_STRUCTURED_IN + n_dense]
    structured_out = refs[N_STRUCTURED_IN + n_dense:N_STRUCTURED_IN + n_dense + N_STRUCTURED_OUT]
    dense_out = refs[N_STRUCTURED_IN + n_dense + N_STRUCTURED_OUT:]

    @pl.when(pl.program_id(0) == 0)
    def _():
        _structured_weights(*refs[:N_STRUCTURED_IN], *structured_out)

    for src, dst in zip(dense_in, dense_out):
        dst[...] = src[...].astype(_BF16)


def _prep(a_re, a_im, log_dt, b_re, b_im, c_re, c_im, pool_w, dense):
    structured = (a_re, a_im, log_dt, b_re.transpose(0, 2, 1), b_im.transpose(0, 2, 1), c_re, c_im, pool_w)
    structured_out = (jax.ShapeDtypeStruct((SSM_GROUPS, LANES), _F32), jax.ShapeDtypeStruct((SSM_GROUPS, LANES), _F32),
                      jax.ShapeDtypeStruct((2, HALF_ROWS, HALF_LANES), _BF16),
                      jax.ShapeDtypeStruct((2, HALF_LANES, HALF_ROWS), _BF16),
                      jax.ShapeDtypeStruct((len(POOL_WINDOWS) // 2, MXU_DIM, MXU_DIM), _BF16))
    assert len(structured) == N_STRUCTURED_IN and len(structured_out) == N_STRUCTURED_OUT
    assert all(w.shape[0] % (PREP_STEPS * BF16_ROWS) == 0 for w in dense)
    whole = lambda a: pl.BlockSpec(a.shape, lambda i, n=len(a.shape): (0,) * n)
    chunk = lambda w: pl.BlockSpec((w.shape[0] // PREP_STEPS, w.shape[1]), lambda i: (i, 0))
    return pl.pallas_call(
        _prep_kernel,
        grid=(PREP_STEPS,),
        in_specs=[whole(a) for a in structured] + [chunk(w) for w in dense],
        out_specs=[whole(a) for a in structured_out] + [chunk(w) for w in dense],
        out_shape=structured_out + tuple(jax.ShapeDtypeStruct(w.shape, _BF16) for w in dense),
        compiler_params=pltpu.CompilerParams(dimension_semantics=("arbitrary",)),
        name="prep",
    )(*structured, *dense)


def _scan_half(xs_ref, s_re_ref, s_im_ref, ab_re_ref, ab_im_ref, half, n_batch, n_steps):
    lane = lax.broadcasted_iota(jnp.int32, (1, LANES), 1)

    def coeff(ref, c):
        row = jnp.where(lane < SSM_STATE, ref[2 * c:2 * c + 1, :], ref[2 * c + 1:2 * c + 2, :])
        return jnp.broadcast_to(row, (SUBLANES, LANES))

    def recur(r0, row_of_step, cls):
        chunks = [half * HALF_CHUNKS + cl for cl in cls]
        ar = [coeff(ab_re_ref, c) for c in chunks]
        ai = [coeff(ab_im_ref, c) for c in chunks]
        sr = [s_re_ref[pl.ds(r0, SUBLANES), c * LANES:(c + 1) * LANES] for c in chunks]
        si = [s_im_ref[pl.ds(r0, SUBLANES), c * LANES:(c + 1) * LANES] for c in chunks]
        for t in range(n_steps):
            rows = pl.ds(row_of_step(t), SUBLANES)
            for k, cl in enumerate(cls):
                lo = cl * CHUNK_LANES
                nr = ar[k] * sr[k] - ai[k] * si[k] + xs_ref[rows, lo:lo + LANES]
                ni = ar[k] * si[k] + ai[k] * sr[k] + xs_ref[rows, lo + LANES:lo + CHUNK_LANES]
                xs_ref[rows, lo:lo + LANES] = nr
                xs_ref[rows, lo + LANES:lo + CHUNK_LANES] = ni
                sr[k], si[k] = nr, ni
        for k, c in enumerate(chunks):
            s_re_ref[pl.ds(r0, SUBLANES), c * LANES:(c + 1) * LANES] = sr[k]
            s_im_ref[pl.ds(r0, SUBLANES), c * LANES:(c + 1) * LANES] = si[k]

    if n_batch == SUBLANES:
        recur(0, lambda t: t * n_batch, list(range(HALF_CHUNKS)))
    else:
        group = 4
        for cg in range(0, HALF_CHUNKS, group):
            def row_block(rb, carry, cg=cg):
                r0 = pl.multiple_of(rb * SUBLANES, SUBLANES)
                recur(r0, lambda t: pl.multiple_of(t * n_batch + r0, SUBLANES), list(range(cg, cg + group)))
                return carry
            lax.fori_loop(0, n_batch // SUBLANES, row_block, 0)


def _mixer_kernel(*refs, n_batch, n_steps, n_tiles, start_pos, zero_state, state_batch_minor):
    n_state_in = 0 if zero_state else 3
    x_ref = refs[0]
    hist0_ref, s0_re_ref, s0_im_ref = refs[1:1 + n_state_in] if n_state_in else (None, None, None)
    (norm_ref, w_in_ref, pool_w_ref, pool_scale_ref, ab_re_ref, ab_im_ref, b_ref, c_ref, d_ref, glu_w_ref,
     glu_b_ref, w_bp_ref, w_bs_ref, w_out_ref,
     h_ref, hist_out_ref, s_re_out_ref, s_im_out_ref,
     *scratch) = refs[1 + n_state_in:]
    i = pl.program_id(0)
    rows = n_batch * n_steps
    hist_rows = HIST_STEPS * n_batch

    scratch = list(scratch)
    hist_scratch = [scratch.pop(0)] if n_tiles > 1 else []
    x_stage_ref, proj_ref, ext_ref, xs_ref = scratch[:4]
    s_re_ref, s_im_ref = scratch[4:6] if state_batch_minor else (s_re_out_ref, s_im_out_ref)
    dma_in = n_tiles > 1
    x_sem = scratch[-1] if dma_in else None
    hist_ref = hist_scratch[0] if hist_scratch else ext_ref.at[0:hist_rows]
    @pl.when(i == 0)
    def _():
        if zero_state:
            hist_ref[...] = jnp.zeros((hist_rows, POOL_WIDTH), _F32)
            s_re_ref[...] = jnp.zeros(s_re_ref.shape, _F32)
            s_im_ref[...] = jnp.zeros(s_im_ref.shape, _F32)
        else:
            hist_ref[0:n_batch, :] = jnp.zeros((n_batch, POOL_WIDTH), _F32)
            hist_ref[n_batch:hist_rows, :] = hist0_ref[...].reshape(POOL_HIST * n_batch, POOL_WIDTH)
            s_re_ref[...] = s0_re_ref[...].T if state_batch_minor else s0_re_ref[...]
            s_im_ref[...] = s0_im_ref[...].T if state_batch_minor else s0_im_ref[...]

    if dma_in:
        slot = lax.rem(i, 2)

        def copies(slot_, step):
            t0 = pl.multiple_of(step * n_steps, n_steps)
            return [pltpu.make_async_copy(x_ref.at[b, pl.ds(t0, n_steps), :], x_stage_ref.at[slot_, :, b, :],
                                          x_sem.at[slot_]) for b in range(n_batch)]

        @pl.when(i == 0)
        def _():
            for c in copies(0, 0):
                c.start()
        for c in copies(slot, i):
            c.wait()

        @pl.when(i + 1 < n_tiles)
        def _():
            for c in copies(1 - slot, i + 1):
                c.start()
        x_rows = lambda: x_stage_ref[slot].reshape(rows, D_MODEL)
    else:
        x_stage_ref[...] = jnp.swapaxes(x_ref[...], 0, 1).reshape(rows, D_MODEL)
        x_rows = lambda: x_stage_ref[...]
    xn = _rmsnorm(x_rows(), norm_ref[...]).astype(_BF16)
    u_width = POOL_WIDTH + SSM_WIDTH
    proj_ref[...] = _dot(xn, w_in_ref[...])

    if hist_scratch:
        ext_ref[0:hist_rows, :] = hist_ref[...]
    ext_ref[hist_rows:, :] = proj_ref[:, 0:POOL_WIDTH]
    if hist_scratch:
        hist_ref[...] = ext_ref[rows:rows + hist_rows, :]
    diffs = []
    for g, w in enumerate(POOL_WINDOWS):
        lo = g * POOL_GROUP_DIM
        u_g = ext_ref[hist_rows:, lo:lo + POOL_GROUP_DIM]
        acc = u_g
        for j in range(1, w):
            r0 = (HIST_STEPS - j) * n_batch
            acc = acc + ext_ref[r0:r0 + rows, lo:lo + POOL_GROUP_DIM]
        if start_pos + 1 >= w:
            pooled = acc * (1.0 / w)
        else:
            row = lax.broadcasted_iota(jnp.int32, (rows, POOL_GROUP_DIM), 0)
            step = lax.shift_right_logical(row, int(math.log2(n_batch)))
            pos = start_pos + i * n_steps + step
            pooled = acc / jnp.minimum(w, pos + 1).astype(_F32)
        diffs.append(pooled - u_g)
    mixed = jnp.concatenate(
        [_dot(jnp.concatenate(diffs[0:2], axis=1).astype(_BF16), pool_w_ref[0]),
         _dot(jnp.concatenate(diffs[2:4], axis=1).astype(_BF16), pool_w_ref[1])], axis=1)
    a_out = (mixed * pool_scale_ref[...]).astype(_BF16)

    u_ssm = proj_ref[:, POOL_WIDTH:u_width]
    u_bf = u_ssm.astype(_BF16)
    for half in range(2):
        xs_ref[half] = _dot(u_bf[:, half * HALF_ROWS:(half + 1) * HALF_ROWS], b_ref[half])
    for half in range(2):
        _scan_half(xs_ref.at[half], s_re_ref, s_im_ref, ab_re_ref, ab_im_ref, half, n_batch, n_steps)
    ys = [_dot(xs_ref[half].astype(_BF16), c_ref[half]) for half in range(2)]
    y_ssm = jnp.concatenate(ys, axis=1) + d_ref[...] * u_ssm
    z = jax.nn.gelu(y_ssm)
    glu = jax.nn.sigmoid(_dot(z.astype(_BF16), glu_w_ref[...]) + glu_b_ref[...])
    b_out = (z * glu).astype(_BF16)

    merged = (jax.nn.sigmoid(proj_ref[:, u_width:u_width + D_MODEL]) * _dot(a_out, w_bp_ref[...])
              + jax.nn.sigmoid(proj_ref[:, u_width + D_MODEL:u_width + 2 * D_MODEL]) * _dot(b_out, w_bs_ref[...]))
    h_ref[...] = x_rows() + _dot(merged.astype(_BF16), w_out_ref[...])

    @pl.when(i == n_tiles - 1)
    def _():
        newest = hist_ref[n_batch:hist_rows, :] if hist_scratch else ext_ref[rows + n_batch:, :]
        hist_out_ref[...] = newest.reshape(POOL_HIST, n_batch, POOL_WIDTH)
        if state_batch_minor:
            s_re_out_ref[...] = s_re_ref[...].T
            s_im_out_ref[...] = s_im_ref[...].T


def _resident(shape):
    zeros = (0,) * len(shape)
    return pl.BlockSpec(shape, lambda i: zeros, pipeline_mode=pl.Buffered(1))


def _tiling(n_batch, n_steps_total, rows_per_step=ROW_TILE):
    n_steps = min(n_steps_total, rows_per_step // n_batch)
    n_tiles = n_steps_total // n_steps
    assert n_tiles * n_steps == n_steps_total and n_batch % SUBLANES == 0
    assert n_batch & (n_batch - 1) == 0
    return n_steps, n_tiles


def _mixer(x, state, weights, *, start_pos, state_batch_minor):
    n_batch, n_steps_total, _ = x.shape
    n_steps, n_tiles = _tiling(n_batch, n_steps_total)
    rows = n_steps * n_batch
    hist_rows = HIST_STEPS * n_batch
    state = () if state is None else tuple(state)
    state_shape = (N_STATES, n_batch) if state_batch_minor else (n_batch, N_STATES)
    kern = functools.partial(_mixer_kernel, n_batch=n_batch, n_steps=n_steps, n_tiles=n_tiles,
                             start_pos=start_pos, zero_state=not state, state_batch_minor=state_batch_minor)
    state_block = pl.BlockSpec(state_shape, lambda i: (0, 0))
    hist_window_bytes = (2 if state else 1) * POOL_HIST * n_batch * POOL_WIDTH * 4
    vmem_limit = min(VMEM_CAP_BYTES, VMEM_LIMIT_BYTES + hist_window_bytes - hist_window_bytes % (1 << 20))
    return pl.pallas_call(
        kern,
        grid=(n_tiles,),
        in_specs=[pl.BlockSpec(memory_space=pl.ANY) if n_tiles > 1
                  else pl.BlockSpec((n_batch, n_steps, D_MODEL), lambda i: (0, i, 0))]
        + [_resident(a.shape) for a in state + tuple(weights)],
        out_specs=(pl.BlockSpec((rows, D_MODEL), lambda i: (i, 0)),
                   pl.BlockSpec((POOL_HIST, n_batch, POOL_WIDTH), lambda i: (0, 0, 0)),
                   state_block, state_block),
        out_shape=(jax.ShapeDtypeStruct((n_steps_total * n_batch, D_MODEL), _F32),
                   jax.ShapeDtypeStruct((POOL_HIST, n_batch, POOL_WIDTH), _F32),
                   jax.ShapeDtypeStruct(state_shape, _F32),
                   jax.ShapeDtypeStruct(state_shape, _F32)),
        scratch_shapes=([pltpu.VMEM((hist_rows, POOL_WIDTH), _F32)] if n_tiles > 1 else [])
        + [pltpu.VMEM((2, n_steps, n_batch, D_MODEL) if n_tiles > 1 else (rows, D_MODEL), _F32),
           pltpu.VMEM((rows, IN_WIDTH), _F32),
           pltpu.VMEM((hist_rows + rows, POOL_WIDTH), _F32),
           pltpu.VMEM((2, rows, HALF_LANES), _F32)]
        + ([pltpu.VMEM((n_batch, N_STATES), _F32)] * 2 if state_batch_minor else [])
        + ([pltpu.SemaphoreType.DMA((2,))] if n_tiles > 1 else []),
        compiler_params=pltpu.CompilerParams(dimension_semantics=("arbitrary",), vmem_limit_bytes=vmem_limit),
        name="mixer",
    )(x, *state, *weights)


def _ffn_kernel(h_a_ref, h_b_ref, norm_ffn_ref, w_gate32_ref, w_up32_ref, w_down32_ref, norm_final_ref,
                y_a_ref, y_b_ref,
                w_gate_ref, w_up_ref, w_down_ref, gate_ref, up_ref, y_buf, sem, *, n_batch_a, n_sub, n_steps_a):
    i = pl.program_id(0)
    j = i - FFN_CAST_STEPS
    sub_rows = h_a_ref.shape[0] // n_sub
    sub_steps = sub_rows // n_batch_a
    steps = n_sub * sub_steps

    def tile(h):
        hn = _rmsnorm(h, norm_ffn_ref[...]).astype(_BF16)
        gate_ref[...] = _dot(hn, w_gate_ref[...])
        up_ref[...] = _dot(hn, w_up_ref[...])
        gate = gate_ref[...]
        f = (gate * jax.nn.sigmoid(gate) * up_ref[...]).astype(_BF16)
        return _rmsnorm(h + _dot(f, w_down_ref[...]), norm_final_ref[...])

    def copies(slot, step):
        t0 = step * steps if isinstance(step, int) else pl.multiple_of(step * steps, steps)
        return [pltpu.make_async_copy(y_buf.at[slot, :, b, :], y_a_ref.at[b, pl.ds(t0, steps), :], sem.at[slot])
                for b in range(n_batch_a)]

    @pl.when(i < FFN_CAST_STEPS)
    def _():
        for src, dst in ((w_gate32_ref, w_gate_ref), (w_up32_ref, w_up_ref), (w_down32_ref, w_down_ref)):
            r = src.shape[0]
            dst[pl.ds(pl.multiple_of(i * r, r), r), :] = src[...].astype(_BF16)

    @pl.when((j >= 0) & (j < n_steps_a))
    def _():
        slot = lax.rem(j, 2)
        for s in range(n_sub):
            y = tile(h_a_ref[s * sub_rows:(s + 1) * sub_rows, :])
            y_buf[slot, s * sub_steps:(s + 1) * sub_steps] = y.reshape(sub_steps, n_batch_a, D_MODEL)

        @pl.when(j > 0)
        def _():
            for c in copies(1 - slot, j - 1):
                c.wait()
        for c in copies(slot, j):
            c.start()

    @pl.when(j == n_steps_a)
    def _():
        n_batch_b, steps_b, _ = y_b_ref.shape
        y = tile(h_b_ref[...]).reshape(steps_b, n_batch_b, D_MODEL)
        y_b_ref[...] = jnp.swapaxes(y, 0, 1)
        for c in copies((n_steps_a - 1) % 2, n_steps_a - 1):
            c.wait()


def _ffn(h_a, h_b, norms, weights32, *, n_batch_a, n_batch_b):
    norm_ffn, norm_final = norms
    w_gate, w_up, w_down = weights32
    n_sub = FFN_SUB_TILES
    rows_a = ROW_TILE * n_sub
    n_steps_a = h_a.shape[0] // rows_a
    assert n_steps_a * rows_a == h_a.shape[0] and rows_a % n_batch_a == 0 and h_b.shape[0] == ROW_TILE
    assert all(w.shape[0] % (FFN_CAST_STEPS * BF16_ROWS) == 0 for w in weights32)
    t_a, t_b = h_a.shape[0] // n_batch_a, h_b.shape[0] // n_batch_b
    w_chunk = lambda w: pl.BlockSpec((w.shape[0] // FFN_CAST_STEPS, w.shape[1]),
                                     lambda i: (jnp.minimum(i, FFN_CAST_STEPS - 1), 0))
    return pl.pallas_call(
        functools.partial(_ffn_kernel, n_batch_a=n_batch_a, n_sub=n_sub, n_steps_a=n_steps_a),
        grid=(FFN_CAST_STEPS + n_steps_a + 1,),
        in_specs=[pl.BlockSpec((rows_a, D_MODEL), lambda i: (jnp.clip(i - FFN_CAST_STEPS, 0, n_steps_a - 1), 0)),
                  _resident(h_b.shape), _resident(norm_ffn.shape),
                  w_chunk(w_gate), w_chunk(w_up), w_chunk(w_down), _resident(norm_final.shape)],
        out_specs=(pl.BlockSpec(memory_space=pl.ANY),
                   pl.BlockSpec((n_batch_b, t_b, D_MODEL), lambda i: (0, 0, 0))),
        out_shape=(jax.ShapeDtypeStruct((n_batch_a, t_a, D_MODEL), _F32),
                   jax.ShapeDtypeStruct((n_batch_b, t_b, D_MODEL), _F32)),
        scratch_shapes=[pltpu.VMEM(w_gate.shape, _BF16), pltpu.VMEM(w_up.shape, _BF16), pltpu.VMEM(w_down.shape, _BF16),
                        pltpu.VMEM((ROW_TILE, D_FF), _F32), pltpu.VMEM((ROW_TILE, D_FF), _F32),
                        pltpu.VMEM((2, rows_a // n_batch_a, n_batch_a, D_MODEL), _F32),
                        pltpu.SemaphoreType.DMA((2,))],
        compiler_params=pltpu.CompilerParams(dimension_semantics=("arbitrary",),
                                             vmem_limit_bytes=VMEM_CAP_BYTES),
        name="ffn",
    )(h_a, h_b, norm_ffn, w_gate, w_up, w_down, norm_final)


def kernel(x_prompt, x_sample, state_pool, state_ssm_re, state_ssm_im, norm_mix, w_in, pool_w, pool_scale, ssm_a_re, ssm_a_im, ssm_log_dt, ssm_b_re, ssm_b_im, ssm_c_re, ssm_c_im, ssm_d, glu_w, glu_b, w_branch_pool, w_branch_ssm, w_out, norm_ffn, ffn_w_gate, ffn_w_up, ffn_w_down, norm_final):
    assert w_in.shape[0] == 1, "single-layer trunk"
    (ab_re, ab_im, b_mat, c_mat, pool_mat, w_in_bf, glu_w_bf, w_bp_bf, w_bs_bf, w_out_bf) = _prep(
        ssm_a_re[0], ssm_a_im[0], ssm_log_dt, ssm_b_re[0], ssm_b_im[0], ssm_c_re[0], ssm_c_im[0], pool_w[0],
        (w_in[0], glu_w[0], w_branch_pool[0], w_branch_ssm[0], w_out[0]))
    mixer_w = (norm_mix[0].reshape(1, D_MODEL), w_in_bf, pool_mat,
               pool_scale[0].reshape(1, POOL_WIDTH), ab_re, ab_im, b_mat, c_mat, ssm_d[0].reshape(1, SSM_WIDTH),
               glu_w_bf, glu_b[0].reshape(1, SSM_WIDTH), w_bp_bf, w_bs_bf, w_out_bf)
    ffn_norms = (norm_ffn[0].reshape(1, D_MODEL), norm_final.reshape(1, D_MODEL))

    def mix(x, state, start_pos):
        b = x.shape[0]
        batch_minor = b % LANES == 0
        if state is not None:
            flat = ((lambda a: a.transpose(1, 2, 0).reshape(N_STATES, b)) if batch_minor
                    else (lambda a: a.reshape(b, N_STATES)))
            state = (state[0].transpose(1, 0, 2), flat(state[1]), flat(state[2]))
        h_rows, new_hist, s_re, s_im = _mixer(x, state, mixer_w, start_pos=start_pos, state_batch_minor=batch_minor)
        unflat = ((lambda a: a.reshape(SSM_GROUPS, SSM_STATE, b).transpose(2, 0, 1)[None]) if batch_minor
                  else (lambda a: a.reshape(1, b, SSM_GROUPS, SSM_STATE)))
        return h_rows, new_hist.transpose(1, 0, 2)[None], unflat(s_re), unflat(s_im)

    h_p, pool_p, re_p, im_p = mix(x_prompt, None, 0)
    h_s, pool_s, re_s, im_s = mix(x_sample, (state_pool[0], state_ssm_re[0], state_ssm_im[0]), PAST_LEN)
    y_p, y_s = _ffn(h_p, h_s, ffn_norms, (ffn_w_gate[0], ffn_w_up[0], ffn_w_down[0]),
                    n_batch_a=x_prompt.shape[0], n_batch_b=x_sample.shape[0])
    return (y_p, y_s, pool_p, re_p, im_p, pool_s, re_s, im_s)
```

```python
import functools
import math

import jax
import jax.numpy as jnp
from jax import lax
from jax.experimental import pallas as pl
from jax.experimental.pallas import tpu as pltpu

D_MODEL = 1024
PAST_LEN = 16384
POOL_WIDTH = 512
POOL_WINDOWS = (2, 4, 8, 16)
POOL_GROUP_DIM = 128
POOL_HIST = 15
HIST_STEPS = POOL_HIST + 1
SSM_WIDTH = 512
SSM_GROUPS = 32
SSM_GROUP_DIM = 16
SSM_STATE = 64
N_STATES = SSM_GROUPS * SSM_STATE
IN_WIDTH = POOL_WIDTH + SSM_WIDTH + 2 * D_MODEL
D_FF = 2816
RMS_EPS = 1e-6

SUBLANES = 8
BF16_ROWS = 16
LANES = 128
MXU_DIM = 256
CHUNK_LANES = 2 * LANES
N_CHUNKS = N_STATES // LANES
HALF_CHUNKS = N_CHUNKS // 2
HALF_LANES = HALF_CHUNKS * CHUNK_LANES
HALF_ROWS = MXU_DIM
ROW_TILE = 512
PREP_STEPS = 4
N_STRUCTURED_IN = 8
N_STRUCTURED_OUT = 5
FFN_SUB_TILES = 2
FFN_CAST_STEPS = 8
VMEM_LIMIT_BYTES = 56 * 1024 * 1024
VMEM_CAP_BYTES = 60 * 1024 * 1024

_F32 = jnp.float32
_BF16 = jnp.bfloat16


def _rmsnorm(x, g):
    return x * lax.rsqrt(jnp.mean(x * x, axis=-1, keepdims=True) + RMS_EPS) * g


def _dot(a, b):
    return jnp.dot(a, b, preferred_element_type=_F32)


def _structured_weights(a_re_ref, a_im_ref, log_dt_ref, b_re_ref, b_im_ref, c_re_ref, c_im_ref, pool_w_ref,
                        ab_re_ref, ab_im_ref, b_mat_ref, c_mat_ref, pool_mat_ref):
    g_n, h_n, p_n = SSM_GROUPS, SSM_GROUP_DIM, SSM_STATE
    twice = lambda v: jnp.concatenate([v, v], axis=-1)

    diag = lax.broadcasted_iota(jnp.int32, (g_n, g_n), 0) == lax.broadcasted_iota(jnp.int32, (g_n, g_n), 1)
    log_dt = jnp.sum(jnp.where(diag, jnp.broadcast_to(log_dt_ref[...], (g_n, g_n)), 0.0), axis=1, keepdims=True)
    dt = jnp.exp(log_dt)
    a_re = a_re_ref[...]
    a_im = a_im_ref[...]
    mag = jnp.exp(a_re * dt)
    ab_re = mag * jnp.cos(a_im * dt)
    ab_im = mag * jnp.sin(a_im * dt)
    num_re = ab_re - 1.0
    num_im = ab_im
    den = a_re * a_re + a_im * a_im
    coef_re = ((num_re * a_re + num_im * a_im) / den)[:, None, :]
    coef_im = ((num_im * a_re - num_re * a_im) / den)[:, None, :]
    ab_re_ref[...] = twice(ab_re)
    ab_im_ref[...] = twice(ab_im)

    b_re = b_re_ref[...]
    b_im = b_im_ref[...]
    bp = [twice(v).reshape(g_n * h_n, LANES)
          for v in (coef_re * b_re - coef_im * b_im, coef_re * b_im + coef_im * b_re)]
    row = lax.broadcasted_iota(jnp.int32, (HALF_ROWS, LANES), 0)
    lane = lax.broadcasted_iota(jnp.int32, (HALF_ROWS, LANES), 1)
    own_lanes = (lane >> 6) == ((row >> 4) & 1)
    for half in range(2):
        for part in range(2):
            mine = jnp.where(own_lanes, bp[part][half * HALF_ROWS:(half + 1) * HALF_ROWS, :], 0.0)
            for cl in range(HALF_CHUNKS):
                blk = jnp.where((row >> 5) == cl, mine, 0.0)
                lo = cl * CHUNK_LANES + part * LANES
                b_mat_ref[half, :, lo:lo + LANES] = blk.astype(_BF16)

    row_c = lax.broadcasted_iota(jnp.int32, (LANES, HALF_ROWS), 0)
    col_c = lax.broadcasted_iota(jnp.int32, (LANES, HALF_ROWS), 1)
    own_cols = (row_c >> 6) == ((col_c >> 4) & 1)
    for part, c_ref in enumerate((c_re_ref, c_im_ref)):
        ct = c_ref[...].reshape(g_n * h_n, p_n).T
        ct = jnp.concatenate([ct, ct], axis=0)
        if part == 1:
            ct = -ct
        for half in range(2):
            mine = jnp.where(own_cols, ct[:, half * HALF_ROWS:(half + 1) * HALF_ROWS], 0.0)
            for cl in range(HALF_CHUNKS):
                blk = jnp.where((col_c >> 5) == cl, mine, 0.0)
                lo = cl * CHUNK_LANES + part * LANES
                c_mat_ref[half, lo:lo + LANES, :] = blk.astype(_BF16)

    pool_mat_ref[...] = jnp.zeros(pool_mat_ref.shape, _BF16)
    for g in range(len(POOL_WINDOWS)):
        lo = (g % 2) * POOL_GROUP_DIM
        pool_mat_ref[g // 2, lo:lo + POOL_GROUP_DIM, lo:lo + POOL_GROUP_DIM] = pool_w_ref[g].astype(_BF16)


def _prep_kernel(*refs):
    n_dense = (len(refs) - N_STRUCTURED_IN - N_STRUCTURED_OUT) // 2
    dense_in = refs[N_STRUCTURED_IN:N_STRUCTURED_IN + n_dense]
    structured_out = refs[N_STRUCTURED_IN + n_dense:N_STRUCTURED_IN + n_dense + N_STRUCTURED_OUT]
    dense_out = refs[N_STRUCTURED_IN + n_dense + N_STRUCTURED_OUT:]

    @pl.when(pl.program_id(0) == 0)
    def _():
        _structured_weights(*refs[:N_STRUCTURED_IN], *structured_out)

    for src, dst in zip(dense_in, dense_out):
        dst[...] = src[...].astype(_BF16)


def _prep(a_re, a_im, log_dt, b_re, b_im, c_re, c_im, pool_w, dense):
    structured = (a_re, a_im, log_dt, b_re.transpose(0, 2, 1), b_im.transpose(0, 2, 1), c_re, c_im, pool_w)
    structured_out = (jax.ShapeDtypeStruct((SSM_GROUPS, LANES), _F32), jax.ShapeDtypeStruct((SSM_GROUPS, LANES), _F32),
                      jax.ShapeDtypeStruct((2, HALF_ROWS, HALF_LANES), _BF16),
                      jax.ShapeDtypeStruct((2, HALF_LANES, HALF_ROWS), _BF16),
                      jax.ShapeDtypeStruct((len(POOL_WINDOWS) // 2, MXU_DIM, MXU_DIM), _BF16))
    assert len(structured) == N_STRUCTURED_IN and len(structured_out) == N_STRUCTURED_OUT
    assert all(w.shape[0] % (PREP_STEPS * BF16_ROWS) == 0 for w in dense)
    whole = lambda a: pl.BlockSpec(a.shape, lambda i, n=len(a.shape): (0,) * n)
    chunk = lambda w: pl.BlockSpec((w.shape[0] // PREP_STEPS, w.shape[1]), lambda i: (i, 0))
    return pl.pallas_call(
        _prep_kernel,
        grid=(PREP_STEPS,),
        in_specs=[whole(a) for a in structured] + [chunk(w) for w in dense],
        out_specs=[whole(a) for a in structured_out] + [chunk(w) for w in dense],
        out_shape=structured_out + tuple(jax.ShapeDtypeStruct(w.shape, _BF16) for w in dense),
        compiler_params=pltpu.CompilerParams(dimension_semantics=("arbitrary",)),
        name="prep",
    )(*structured, *dense)


def _scan_half(xs_ref, s_re_ref, s_im_ref, ab_re_ref, ab_im_ref, half, n_batch, n_steps):
    lane = lax.broadcasted_iota(jnp.int32, (1, LANES), 1)

    def coeff(ref, c):
        row = jnp.where(lane < SSM_STATE, ref[2 * c:2 * c + 1, :], ref[2 * c + 1:2 * c + 2, :])
        return jnp.broadcast_to(row, (SUBLANES, LANES))

    def recur(r0, row_of_step, cls):
        chunks = [half * HALF_CHUNKS + cl for cl in cls]
        ar = [coeff(ab_re_ref, c) for c in chunks]
        ai = [coeff(ab_im_ref, c) for c in chunks]
        sr = [s_re_ref[pl.ds(r0, SUBLANES), c * LANES:(c + 1) * LANES] for c in chunks]
        si = [s_im_ref[pl.ds(r0, SUBLANES), c * LANES:(c + 1) * LANES] for c in chunks]
        for t in range(n_steps):
            rows = pl.ds(row_of_step(t), SUBLANES)
            for k, cl in enumerate(cls):
                lo = cl * CHUNK_LANES
                nr = ar[k] * sr[k] - ai[k] * si[k] + xs_ref[rows, lo:lo + LANES]
                ni = ar[k] * si[k] + ai[k] * sr[k] + xs_ref[rows, lo + LANES:lo + CHUNK_LANES]
                xs_ref[rows, lo:lo + LANES] = nr
                xs_ref[rows, lo + LANES:lo + CHUNK_LANES] = ni
                sr[k], si[k] = nr, ni
        for k, c in enumerate(chunks):
            s_re_ref[pl.ds(r0, SUBLANES), c * LANES:(c + 1) * LANES] = sr[k]
            s_im_ref[pl.ds(r0, SUBLANES), c * LANES:(c + 1) * LANES] = si[k]

    if n_batch == SUBLANES:
        recur(0, lambda t: t * n_batch, list(range(HALF_CHUNKS)))
    else:
        group = 4
        for cg in range(0, HALF_CHUNKS, group):
            def row_block(rb, carry, cg=cg):
                r0 = pl.multiple_of(rb * SUBLANES, SUBLANES)
                recur(r0, lambda t: pl.multiple_of(t * n_batch + r0, SUBLANES), list(range(cg, cg + group)))
                return carry
            lax.fori_loop(0, n_batch // SUBLANES, row_block, 0)


def _mixer_kernel(*refs, n_batch, n_steps, n_tiles, start_pos, zero_state, state_batch_minor):
    n_state_in = 0 if zero_state else 3
    x_ref = refs[0]
    hist0_ref, s0_re_ref, s0_im_ref = refs[1:1 + n_state_in] if n_state_in else (None, None, None)
    (norm_ref, w_in_ref, pool_w_ref, pool_scale_ref, ab_re_ref, ab_im_ref, b_ref, c_ref, d_ref, glu_w_ref,
     glu_b_ref, w_bp_ref, w_bs_ref, w_out_ref,
     h_ref, hist_out_ref, s_re_out_ref, s_im_out_ref,
     *scratch) = refs[1 + n_state_in:]
    i = pl.program_id(0)
    rows = n_batch * n_steps
    hist_rows = HIST_STEPS * n_batch

    scratch = list(scratch)
    hist_scratch = [scratch.pop(0)] if n_tiles > 1 else []
    x_stage_ref, proj_ref, ext_ref, xs_ref = scratch[:4]
    s_re_ref, s_im_ref = scratch[4:6] if state_batch_minor else (s_re_out_ref, s_im_out_ref)
    dma_in = n_tiles > 1
    x_sem = scratch[-1] if dma_in else None
    hist_ref = hist_scratch[0] if hist_scratch else ext_ref.at[0:hist_rows]
    @pl.when(i == 0)
    def _():
        if zero_state:
            hist_ref[...] = jnp.zeros((hist_rows, POOL_WIDTH), _F32)
            s_re_ref[...] = jnp.zeros(s_re_ref.shape, _F32)
            s_im_ref[...] = jnp.zeros(s_im_ref.shape, _F32)
        else:
            hist_ref[0:n_batch, :] = jnp.zeros((n_batch, POOL_WIDTH), _F32)
            hist_ref[n_batch:hist_rows, :] = hist0_ref[...].reshape(POOL_HIST * n_batch, POOL_WIDTH)
            s_re_ref[...] = s0_re_ref[...].T if state_batch_minor else s0_re_ref[...]
            s_im_ref[...] = s0_im_ref[...].T if state_batch_minor else s0_im_ref[...]

    if dma_in:
        slot = lax.rem(i, 2)

        def copies(slot_, step):
            t0 = pl.multiple_of(step * n_steps, n_steps)
            return [pltpu.make_async_copy(x_ref.at[b, pl.ds(t0, n_steps), :], x_stage_ref.at[slot_, :, b, :],
                                          x_sem.at[slot_]) for b in range(n_batch)]

        @pl.when(i == 0)
        def _():
            for c in copies(0, 0):
                c.start()
        for c in copies(slot, i):
            c.wait()

        @pl.when(i + 1 < n_tiles)
        def _():
            for c in copies(1 - slot, i + 1):
                c.start()
        x_rows = lambda: x_stage_ref[slot].reshape(rows, D_MODEL)
    else:
        x_stage_ref[...] = jnp.swapaxes(x_ref[...], 0, 1).reshape(rows, D_MODEL)
        x_rows = lambda: x_stage_ref[...]
    xn = _rmsnorm(x_rows(), norm_ref[...]).astype(_BF16)
    u_width = POOL_WIDTH + SSM_WIDTH
    proj_ref[...] = _dot(xn, w_in_ref[...])

    if hist_scratch:
        ext_ref[0:hist_rows, :] = hist_ref[...]
    ext_ref[hist_rows:, :] = proj_ref[:, 0:POOL_WIDTH]
    if hist_scratch:
        hist_ref[...] = ext_ref[rows:rows + hist_rows, :]
    diffs = []
    for g, w in enumerate(POOL_WINDOWS):
        lo = g * POOL_GROUP_DIM
        u_g = ext_ref[hist_rows:, lo:lo + POOL_GROUP_DIM]
        acc = u_g
        for j in range(1, w):
            r0 = (HIST_STEPS - j) * n_batch
            acc = acc + ext_ref[r0:r0 + rows, lo:lo + POOL_GROUP_DIM]
        if start_pos + 1 >= w:
            pooled = acc * (1.0 / w)
        else:
            head = min(rows, hist_rows)
            row = lax.broadcasted_iota(jnp.int32, (head, POOL_GROUP_DIM), 0)
            step = lax.shift_right_logical(row, int(math.log2(n_batch)))
            pos = start_pos + i * n_steps + step
            pooled = acc[:head] / jnp.minimum(w, pos + 1).astype(_F32)
            if head < rows:
                pooled = jnp.concatenate([pooled, acc[head:] * (1.0 / w)], axis=0)
        diffs.append(pooled - u_g)
    mixed = jnp.concatenate(
        [_dot(jnp.concatenate(diffs[0:2], axis=1).astype(_BF16), pool_w_ref[0]),
         _dot(jnp.concatenate(diffs[2:4], axis=1).astype(_BF16), pool_w_ref[1])], axis=1)
    a_out = (mixed * pool_scale_ref[...]).astype(_BF16)

    u_ssm = proj_ref[:, POOL_WIDTH:u_width]
    u_bf = u_ssm.astype(_BF16)
    for half in range(2):
        xs_ref[half] = _dot(u_bf[:, half * HALF_ROWS:(half + 1) * HALF_ROWS], b_ref[half])
    for half in range(2):
        _scan_half(xs_ref.at[half], s_re_ref, s_im_ref, ab_re_ref, ab_im_ref, half, n_batch, n_steps)
    ys = [_dot(xs_ref[half].astype(_BF16), c_ref[half]) for half in range(2)]
    y_ssm = jnp.concatenate(ys, axis=1) + d_ref[...] * u_ssm
    z = jax.nn.gelu(y_ssm)
    glu = jax.nn.sigmoid(_dot(z.astype(_BF16), glu_w_ref[...]) + glu_b_ref[...])
    b_out = (z * glu).astype(_BF16)

    merged = (jax.nn.sigmoid(proj_ref[:, u_width:u_width + D_MODEL]) * _dot(a_out, w_bp_ref[...])
              + jax.nn.sigmoid(proj_ref[:, u_width + D_MODEL:u_width + 2 * D_MODEL]) * _dot(b_out, w_bs_ref[...]))
    h_ref[...] = x_rows() + _dot(merged.astype(_BF16), w_out_ref[...])

    @pl.when(i == n_tiles - 1)
    def _():
        newest = hist_ref[n_batch:hist_rows, :] if hist_scratch else ext_ref[rows + n_batch:, :]
        hist_out_ref[...] = newest.reshape(POOL_HIST, n_batch, POOL_WIDTH)
        if state_batch_minor:
            s_re_out_ref[...] = s_re_ref[...].T
            s_im_out_ref[...] = s_im_ref[...].T


def _resident(shape):
    zeros = (0,) * len(shape)
    return pl.BlockSpec(shape, lambda i: zeros, pipeline_mode=pl.Buffered(1))


def _tiling(n_batch, n_steps_total, rows_per_step=ROW_TILE):
    n_steps = min(n_steps_total, rows_per_step // n_batch)
    n_tiles = n_steps_total // n_steps
    assert n_tiles * n_steps == n_steps_total and n_batch % SUBLANES == 0
    assert n_batch & (n_batch - 1) == 0
    return n_steps, n_tiles


def _mixer(x, state, weights, *, start_pos, state_batch_minor):
    n_batch, n_steps_total, _ = x.shape
    n_steps, n_tiles = _tiling(n_batch, n_steps_total)
    rows = n_steps * n_batch
    hist_rows = HIST_STEPS * n_batch
    state = () if state is None else tuple(state)
    state_shape = (N_STATES, n_batch) if state_batch_minor else (n_batch, N_STATES)
    kern = functools.partial(_mixer_kernel, n_batch=n_batch, n_steps=n_steps, n_tiles=n_tiles,
                             start_pos=start_pos, zero_state=not state, state_batch_minor=state_batch_minor)
    state_block = pl.BlockSpec(state_shape, lambda i: (0, 0))
    hist_window_bytes = (2 if state else 1) * POOL_HIST * n_batch * POOL_WIDTH * 4
    vmem_limit = min(VMEM_CAP_BYTES, VMEM_LIMIT_BYTES + hist_window_bytes - hist_window_bytes % (1 << 20))
    return pl.pallas_call(
        kern,
        grid=(n_tiles,),
        in_specs=[pl.BlockSpec(memory_space=pl.ANY) if n_tiles > 1
                  else pl.BlockSpec((n_batch, n_steps, D_MODEL), lambda i: (0, i, 0))]
        + [_resident(a.shape) for a in state + tuple(weights)],
        out_specs=(pl.BlockSpec((rows, D_MODEL), lambda i: (i, 0)),
                   pl.BlockSpec((POOL_HIST, n_batch, POOL_WIDTH), lambda i: (0, 0, 0)),
                   state_block, state_block),
        out_shape=(jax.ShapeDtypeStruct((n_steps_total * n_batch, D_MODEL), _F32),
                   jax.ShapeDtypeStruct((POOL_HIST, n_batch, POOL_WIDTH), _F32),
                   jax.ShapeDtypeStruct(state_shape, _F32),
                   jax.ShapeDtypeStruct(state_shape, _F32)),
        scratch_shapes=([pltpu.VMEM((hist_rows, POOL_WIDTH), _F32)] if n_tiles > 1 else [])
        + [pltpu.VMEM((2, n_steps, n_batch, D_MODEL) if n_tiles > 1 else (rows, D_MODEL), _F32),
           pltpu.VMEM((rows, IN_WIDTH), _F32),
           pltpu.VMEM((hist_rows + rows, POOL_WIDTH), _F32),
           pltpu.VMEM((2, rows, HALF_LANES), _F32)]
        + ([pltpu.VMEM((n_batch, N_STATES), _F32)] * 2 if state_batch_minor else [])
        + ([pltpu.SemaphoreType.DMA((2,))] if n_tiles > 1 else []),
        compiler_params=pltpu.CompilerParams(dimension_semantics=("arbitrary",), vmem_limit_bytes=vmem_limit),
        name="mixer",
    )(x, *state, *weights)


def _ffn_kernel(h_a_ref, h_b_ref, norm_ffn_ref, w_gate32_ref, w_up32_ref, w_down32_ref, norm_final_ref,
                y_a_ref, y_b_ref,
                w_gate_ref, w_up_ref, w_down_ref, gate_ref, up_ref, y_buf, sem, *, n_batch_a, n_sub, n_steps_a):
    i = pl.program_id(0)
    j = i - FFN_CAST_STEPS
    sub_rows = h_a_ref.shape[0] // n_sub
    sub_steps = sub_rows // n_batch_a
    steps = n_sub * sub_steps

    def tile(h):
        hn = _rmsnorm(h, norm_ffn_ref[...]).astype(_BF16)
        gate_ref[...] = _dot(hn, w_gate_ref[...])
        up_ref[...] = _dot(hn, w_up_ref[...])
        gate = gate_ref[...]
        f = (gate * jax.nn.sigmoid(gate) * up_ref[...]).astype(_BF16)
        return _rmsnorm(h + _dot(f, w_down_ref[...]), norm_final_ref[...])

    def copies(slot, step):
        t0 = step * steps if isinstance(step, int) else pl.multiple_of(step * steps, steps)
        return [pltpu.make_async_copy(y_buf.at[slot, :, b, :], y_a_ref.at[b, pl.ds(t0, steps), :], sem.at[slot])
                for b in range(n_batch_a)]

    @pl.when(i < FFN_CAST_STEPS)
    def _():
        for src, dst in ((w_gate32_ref, w_gate_ref), (w_up32_ref, w_up_ref), (w_down32_ref, w_down_ref)):
            r = src.shape[0]
            dst[pl.ds(pl.multiple_of(i * r, r), r), :] = src[...].astype(_BF16)

    @pl.when((j >= 0) & (j < n_steps_a))
    def _():
        slot = lax.rem(j, 2)
        for s in range(n_sub):
            y = tile(h_a_ref[s * sub_rows:(s + 1) * sub_rows, :])
            y_buf[slot, s * sub_steps:(s + 1) * sub_steps] = y.reshape(sub_steps, n_batch_a, D_MODEL)

        @pl.when(j > 0)
        def _():
            for c in copies(1 - slot, j - 1):
                c.wait()
        for c in copies(slot, j):
            c.start()

    @pl.when(j == n_steps_a)
    def _():
        n_batch_b, steps_b, _ = y_b_ref.shape
        y = tile(h_b_ref[...]).reshape(steps_b, n_batch_b, D_MODEL)
        y_b_ref[...] = jnp.swapaxes(y, 0, 1)
        for c in copies((n_steps_a - 1) % 2, n_steps_a - 1):
            c.wait()


def _ffn(h_a, h_b, norms, weights32, *, n_batch_a, n_batch_b):
    norm_ffn, norm_final = norms
    w_gate, w_up, w_down = weights32
    n_sub = FFN_SUB_TILES
    rows_a = ROW_TILE * n_sub
    n_steps_a = h_a.shape[0] // rows_a
    assert n_steps_a * rows_a == h_a.shape[0] and rows_a % n_batch_a == 0 and h_b.shape[0] == ROW_TILE
    assert all(w.shape[0] % (FFN_CAST_STEPS * BF16_ROWS) == 0 for w in weights32)
    t_a, t_b = h_a.shape[0] // n_batch_a, h_b.shape[0] // n_batch_b
    w_chunk = lambda w: pl.BlockSpec((w.shape[0] // FFN_CAST_STEPS, w.shape[1]),
                                     lambda i: (jnp.minimum(i, FFN_CAST_STEPS - 1), 0))
    return pl.pallas_call(
        functools.partial(_ffn_kernel, n_batch_a=n_batch_a, n_sub=n_sub, n_steps_a=n_steps_a),
        grid=(FFN_CAST_STEPS + n_steps_a + 1,),
        in_specs=[pl.BlockSpec((rows_a, D_MODEL), lambda i: (jnp.clip(i - FFN_CAST_STEPS, 0, n_steps_a - 1), 0)),
                  _resident(h_b.shape), _resident(norm_ffn.shape),
                  w_chunk(w_gate), w_chunk(w_up), w_chunk(w_down), _resident(norm_final.shape)],
        out_specs=(pl.BlockSpec(memory_space=pl.ANY),
                   pl.BlockSpec((n_batch_b, t_b, D_MODEL), lambda i: (0, 0, 0))),
        out_shape=(jax.ShapeDtypeStruct((n_batch_a, t_a, D_MODEL), _F32),
                   jax.ShapeDtypeStruct((n_batch_b, t_b, D_MODEL), _F32)),
        scratch_shapes=[pltpu.VMEM(w_gate.shape, _BF16), pltpu.VMEM(w_up.shape, _BF16), pltpu.VMEM(w_down.shape, _BF16),
                        pltpu.VMEM((ROW_TILE, D_FF), _F32), pltpu.VMEM((ROW_TILE, D_FF), _F32),
                        pltpu.VMEM((2, rows_a // n_batch_a, n_batch_a, D_MODEL), _F32),
                        pltpu.SemaphoreType.DMA((2,))],
        compiler_params=pltpu.CompilerParams(dimension_semantics=("arbitrary",),
                                             vmem_limit_bytes=VMEM_CAP_BYTES),
        name="ffn",
    )(h_a, h_b, norm_ffn, w_gate, w_up, w_down, norm_final)


def kernel(x_prompt, x_sample, state_pool, state_ssm_re, state_ssm_im, norm_mix, w_in, pool_w, pool_scale, ssm_a_re, ssm_a_im, ssm_log_dt, ssm_b_re, ssm_b_im, ssm_c_re, ssm_c_im, ssm_d, glu_w, glu_b, w_branch_pool, w_branch_ssm, w_out, norm_ffn, ffn_w_gate, ffn_w_up, ffn_w_down, norm_final):
    assert w_in.shape[0] == 1, "single-layer trunk"
    (ab_re, ab_im, b_mat, c_mat, pool_mat, w_in_bf, glu_w_bf, w_bp_bf, w_bs_bf, w_out_bf) = _prep(
        ssm_a_re[0], ssm_a_im[0], ssm_log_dt, ssm_b_re[0], ssm_b_im[0], ssm_c_re[0], ssm_c_im[0], pool_w[0],
        (w_in[0], glu_w[0], w_branch_pool[0], w_branch_ssm[0], w_out[0]))
    mixer_w = (norm_mix[0].reshape(1, D_MODEL), w_in_bf, pool_mat,
               pool_scale[0].reshape(1, POOL_WIDTH), ab_re, ab_im, b_mat, c_mat, ssm_d[0].reshape(1, SSM_WIDTH),
               glu_w_bf, glu_b[0].reshape(1, SSM_WIDTH), w_bp_bf, w_bs_bf, w_out_bf)
    ffn_norms = (norm_ffn[0].reshape(1, D_MODEL), norm_final.reshape(1, D_MODEL))

    def mix(x, state, start_pos):
        b = x.shape[0]
        batch_minor = b % LANES == 0
        if state is not None:
            flat = ((lambda a: a.transpose(1, 2, 0).reshape(N_STATES, b)) if batch_minor
                    else (lambda a: a.reshape(b, N_STATES)))
            state = (state[0].transpose(1, 0, 2), flat(state[1]), flat(state[2]))
        h_rows, new_hist, s_re, s_im = _mixer(x, state, mixer_w, start_pos=start_pos, state_batch_minor=batch_minor)
        unflat = ((lambda a: a.reshape(SSM_GROUPS, SSM_STATE, b).transpose(2, 0, 1)[None]) if batch_minor
                  else (lambda a: a.reshape(1, b, SSM_GROUPS, SSM_STATE)))
        return h_rows, new_hist.transpose(1, 0, 2)[None], unflat(s_re), unflat(s_im)

    h_p, pool_p, re_p, im_p = mix(x_prompt, None, 0)
    h_s, pool_s, re_s, im_s = mix(x_sample, (state_pool[0], state_ssm_re[0], state_ssm_im[0]), PAST_LEN)
    y_p, y_s = _ffn(h_p, h_s, ffn_norms, (ffn_w_gate[0], ffn_w_up[0], ffn_w_down[0]),
                    n_batch_a=x_prompt.shape[0], n_batch_b=x_sample.shape[0])
    return (y_p, y_s, pool_p, re_p, im_p, pool_s, re_s, im_s)
```

```python
import functools
import math

import jax
import jax.numpy as jnp
from jax import lax
from jax.experimental import pallas as pl
from jax.experimental.pallas import tpu as pltpu

D_MODEL = 1024
PAST_LEN = 16384
POOL_WIDTH = 512
POOL_WINDOWS = (2, 4, 8, 16)
POOL_GROUP_DIM = 128
POOL_HIST = 15
HIST_STEPS = POOL_HIST + 1
SSM_WIDTH = 512
SSM_GROUPS = 32
SSM_GROUP_DIM = 16
SSM_STATE = 64
N_STATES = SSM_GROUPS * SSM_STATE
IN_WIDTH = POOL_WIDTH + SSM_WIDTH + 2 * D_MODEL
D_FF = 2816
RMS_EPS = 1e-6

SUBLANES = 8
BF16_ROWS = 16
LANES = 128
MXU_DIM = 256
CHUNK_LANES = 2 * LANES
N_CHUNKS = N_STATES // LANES
HALF_CHUNKS = N_CHUNKS // 2
HALF_LANES = HALF_CHUNKS * CHUNK_LANES
HALF_ROWS = MXU_DIM
ROW_TILE = 512
PREP_STEPS = 4
N_STRUCTURED_IN = 8
N_STRUCTURED_OUT = 5
FFN_SUB_TILES = 2
FFN_CAST_STEPS = 8
VMEM_LIMIT_BYTES = 56 * 1024 * 1024
VMEM_CAP_BYTES = 60 * 1024 * 1024

_F32 = jnp.float32
_BF16 = jnp.bfloat16


def _rmsnorm(x, g):
    return x * lax.rsqrt(jnp.mean(x * x, axis=-1, keepdims=True) + RMS_EPS) * g


def _dot(a, b):
    return jnp.dot(a, b, preferred_element_type=_F32)


def _structured_weights(a_re_ref, a_im_ref, log_dt_ref, b_re_ref, b_im_ref, c_re_ref, c_im_ref, pool_w_ref,
                        ab_re_ref, ab_im_ref, b_mat_ref, c_mat_ref, pool_mat_ref):
    g_n, h_n, p_n = SSM_GROUPS, SSM_GROUP_DIM, SSM_STATE
    twice = lambda v: jnp.concatenate([v, v], axis=-1)

    diag = lax.broadcasted_iota(jnp.int32, (g_n, g_n), 0) == lax.broadcasted_iota(jnp.int32, (g_n, g_n), 1)
    log_dt = jnp.sum(jnp.where(diag, jnp.broadcast_to(log_dt_ref[...], (g_n, g_n)), 0.0), axis=1, keepdims=True)
    dt = jnp.exp(log_dt)
    a_re = a_re_ref[...]
    a_im = a_im_ref[...]
    mag = jnp.exp(a_re * dt)
    ab_re = mag * jnp.cos(a_im * dt)
    ab_im = mag * jnp.sin(a_im * dt)
    num_re = ab_re - 1.0
    num_im = ab_im
    den = a_re * a_re + a_im * a_im
    coef_re = ((num_re * a_re + num_im * a_im) / den)[:, None, :]
    coef_im = ((num_im * a_re - num_re * a_im) / den)[:, None, :]
    ab_re_ref[...] = twice(ab_re)
    ab_im_ref[...] = twice(ab_im)

    b_re = b_re_ref[...]
    b_im = b_im_ref[...]
    bp = [twice(v).reshape(g_n * h_n, LANES)
          for v in (coef_re * b_re - coef_im * b_im, coef_re * b_im + coef_im * b_re)]
    row = lax.broadcasted_iota(jnp.int32, (HALF_ROWS, LANES), 0)
    lane = lax.broadcasted_iota(jnp.int32, (HALF_ROWS, LANES), 1)
    own_lanes = (lane >> 6) == ((row >> 4) & 1)
    for half in range(2):
        for part in range(2):
            mine = jnp.where(own_lanes, bp[part][half * HALF_ROWS:(half + 1) * HALF_ROWS, :], 0.0)
            for cl in range(HALF_CHUNKS):
                blk = jnp.where((row >> 5) == cl, mine, 0.0)
                lo = cl * CHUNK_LANES + part * LANES
                b_mat_ref[half, :, lo:lo + LANES] = blk.astype(_BF16)

    row_c = lax.broadcasted_iota(jnp.int32, (LANES, HALF_ROWS), 0)
    col_c = lax.broadcasted_iota(jnp.int32, (LANES, HALF_ROWS), 1)
    own_cols = (row_c >> 6) == ((col_c >> 4) & 1)
    for part, c_ref in enumerate((c_re_ref, c_im_ref)):
        ct = c_ref[...].reshape(g_n * h_n, p_n).T
        ct = jnp.concatenate([ct, ct], axis=0)
        if part == 1:
            ct = -ct
        for half in range(2):
            mine = jnp.where(own_cols, ct[:, half * HALF_ROWS:(half + 1) * HALF_ROWS], 0.0)
            for cl in range(HALF_CHUNKS):
                blk = jnp.where((col_c >> 5) == cl, mine, 0.0)
                lo = cl * CHUNK_LANES + part * LANES
                c_mat_ref[half, lo:lo + LANES, :] = blk.astype(_BF16)

    pool_mat_ref[...] = jnp.zeros(pool_mat_ref.shape, _BF16)
    for g in range(len(POOL_WINDOWS)):
        lo = (g % 2) * POOL_GROUP_DIM
        pool_mat_ref[g // 2, lo:lo + POOL_GROUP_DIM, lo:lo + POOL_GROUP_DIM] = pool_w_ref[g].astype(_BF16)


def _prep_kernel(*refs):
    n_dense = (len(refs) - N_STRUCTURED_IN - N_STRUCTURED_OUT) // 2
    dense_in = refs[N_STRUCTURED_IN:N_STRUCTURED_IN + n_dense]
    structured_out = refs[N_STRUCTURED_IN + n_dense:N_STRUCTURED_IN + n_dense + N_STRUCTURED_OUT]
    dense_out = refs[N_STRUCTURED_IN + n_dense + N_STRUCTURED_OUT:]

    @pl.when(pl.program_id(0) == 0)
    def _():
        _structured_weights(*refs[:N_STRUCTURED_IN], *structured_out)

    for src, dst in zip(dense_in, dense_out):
        dst[...] = src[...].astype(_BF16)


def _prep(a_re, a_im, log_dt, b_re, b_im, c_re, c_im, pool_w, dense):
    structured = (a_re, a_im, log_dt, b_re.transpose(0, 2, 1), b_im.transpose(0, 2, 1), c_re, c_im, pool_w)
    structured_out = (jax.ShapeDtypeStruct((SSM_GROUPS, LANES), _F32), jax.ShapeDtypeStruct((SSM_GROUPS, LANES), _F32),
                      jax.ShapeDtypeStruct((2, HALF_ROWS, HALF_LANES), _BF16),
                      jax.ShapeDtypeStruct((2, HALF_LANES, HALF_ROWS), _BF16),
                      jax.ShapeDtypeStruct((len(POOL_WINDOWS) // 2, MXU_DIM, MXU_DIM), _BF16))
    assert len(structured) == N_STRUCTURED_IN and len(structured_out) == N_STRUCTURED_OUT
    assert all(w.shape[0] % (PREP_STEPS * BF16_ROWS) == 0 for w in dense)
    whole = lambda a: pl.BlockSpec(a.shape, lambda i, n=len(a.shape): (0,) * n)
    chunk = lambda w: pl.BlockSpec((w.shape[0] // PREP_STEPS, w.shape[1]), lambda i: (i, 0))
    return pl.pallas_call(
        _prep_kernel,
        grid=(PREP_STEPS,),
        in_specs=[whole(a) for a in structured] + [chunk(w) for w in dense],
        out_specs=[whole(a) for a in structured_out] + [chunk(w) for w in dense],
        out_shape=structured_out + tuple(jax.ShapeDtypeStruct(w.shape, _BF16) for w in dense),
        compiler_params=pltpu.CompilerParams(dimension_semantics=("arbitrary",)),
        name="prep",
    )(*structured, *dense)


def _scan_half(xs_ref, s_re_ref, s_im_ref, ab_re_ref, ab_im_ref, half, n_batch, n_steps):
    lane = lax.broadcasted_iota(jnp.int32, (1, LANES), 1)

    def coeff(ref, c):
        row = jnp.where(lane < SSM_STATE, ref[2 * c:2 * c + 1, :], ref[2 * c + 1:2 * c + 2, :])
        return jnp.broadcast_to(row, (SUBLANES, LANES))

    def recur(r0, row_of_step, cls):
        chunks = [half * HALF_CHUNKS + cl for cl in cls]
        ar = [coeff(ab_re_ref, c) for c in chunks]
        ai = [coeff(ab_im_ref, c) for c in chunks]
        sr = [s_re_ref[pl.ds(r0, SUBLANES), c * LANES:(c + 1) * LANES] for c in chunks]
        si = [s_im_ref[pl.ds(r0, SUBLANES), c * LANES:(c + 1) * LANES] for c in chunks]
        for t in range(n_steps):
            rows = pl.ds(row_of_step(t), SUBLANES)
            for k, cl in enumerate(cls):
                lo = cl * CHUNK_LANES
                nr = ar[k] * sr[k] - ai[k] * si[k] + xs_ref[rows, lo:lo + LANES]
                ni = ar[k] * si[k] + ai[k] * sr[k] + xs_ref[rows, lo + LANES:lo + CHUNK_LANES]
                xs_ref[rows, lo:lo + LANES] = nr
                xs_ref[rows, lo + LANES:lo + CHUNK_LANES] = ni
                sr[k], si[k] = nr, ni
        for k, c in enumerate(chunks):
            s_re_ref[pl.ds(r0, SUBLANES), c * LANES:(c + 1) * LANES] = sr[k]
            s_im_ref[pl.ds(r0, SUBLANES), c * LANES:(c + 1) * LANES] = si[k]

    for r0 in range(0, n_batch, SUBLANES):
        recur(r0, lambda t, r0=r0: t * n_batch + r0, list(range(HALF_CHUNKS)))


def _mixer_kernel(*refs, n_batch, n_steps, n_tiles, start_pos, zero_state, state_batch_minor):
    n_state_in = 0 if zero_state else 3
    x_ref = refs[0]
    hist0_ref, s0_re_ref, s0_im_ref = refs[1:1 + n_state_in] if n_state_in else (None, None, None)
    (norm_ref, w_in_ref, pool_w_ref, pool_scale_ref, ab_re_ref, ab_im_ref, b_ref, c_ref, d_ref, glu_w_ref,
     glu_b_ref, w_bp_ref, w_bs_ref, w_out_ref,
     h_ref, hist_out_ref, s_re_out_ref, s_im_out_ref,
     *scratch) = refs[1 + n_state_in:]
    i = pl.program_id(0)
    rows = n_batch * n_steps
    hist_rows = HIST_STEPS * n_batch

    scratch = list(scratch)
    hist_scratch = [scratch.pop(0)] if n_tiles > 1 else []
    x_stage_ref, proj_ref, ext_ref, xs_ref = scratch[:4]
    s_re_ref, s_im_ref = scratch[4:6] if state_batch_minor else (s_re_out_ref, s_im_out_ref)
    dma_in = n_tiles > 1
    x_sem = scratch[-1] if dma_in else None
    hist_ref = hist_scratch[0] if hist_scratch else ext_ref.at[0:hist_rows]
    @pl.when(i == 0)
    def _():
        if zero_state:
            hist_ref[...] = jnp.zeros((hist_rows, POOL_WIDTH), _F32)
            s_re_ref[...] = jnp.zeros(s_re_ref.shape, _F32)
            s_im_ref[...] = jnp.zeros(s_im_ref.shape, _F32)
        else:
            hist_ref[0:n_batch, :] = jnp.zeros((n_batch, POOL_WIDTH), _F32)
            hist_ref[n_batch:hist_rows, :] = hist0_ref[...].reshape(POOL_HIST * n_batch, POOL_WIDTH)
            s_re_ref[...] = s0_re_ref[...].T if state_batch_minor else s0_re_ref[...]
            s_im_ref[...] = s0_im_ref[...].T if state_batch_minor else s0_im_ref[...]

    if dma_in:
        slot = lax.rem(i, 2)

        def copies(slot_, step):
            t0 = pl.multiple_of(step * n_steps, n_steps)
            return [pltpu.make_async_copy(x_ref.at[b, pl.ds(t0, n_steps), :], x_stage_ref.at[slot_, :, b, :],
                                          x_sem.at[slot_]) for b in range(n_batch)]

        @pl.when(i == 0)
        def _():
            for c in copies(0, 0):
                c.start()
        for c in copies(slot, i):
            c.wait()

        @pl.when(i + 1 < n_tiles)
        def _():
            for c in copies(1 - slot, i + 1):
                c.start()
        x_rows = lambda: x_stage_ref[slot].reshape(rows, D_MODEL)
    else:
        x_stage_ref[...] = jnp.swapaxes(x_ref[...], 0, 1).reshape(rows, D_MODEL)
        x_rows = lambda: x_stage_ref[...]
    xn = _rmsnorm(x_rows(), norm_ref[...]).astype(_BF16)
    u_width = POOL_WIDTH + SSM_WIDTH
    proj_ref[...] = _dot(xn, w_in_ref[...])

    if hist_scratch:
        ext_ref[0:hist_rows, :] = hist_ref[...]
    ext_ref[hist_rows:, :] = proj_ref[:, 0:POOL_WIDTH]
    if hist_scratch:
        hist_ref[...] = ext_ref[rows:rows + hist_rows, :]
    diffs = []
    for g, w in enumerate(POOL_WINDOWS):
        lo = g * POOL_GROUP_DIM
        u_g = ext_ref[hist_rows:, lo:lo + POOL_GROUP_DIM]
        acc = u_g
        for j in range(1, w):
            r0 = (HIST_STEPS - j) * n_batch
            acc = acc + ext_ref[r0:r0 + rows, lo:lo + POOL_GROUP_DIM]
        if start_pos + 1 >= w:
            pooled = acc * (1.0 / w)
        else:
            head = min(rows, hist_rows)
            row = lax.broadcasted_iota(jnp.int32, (head, POOL_GROUP_DIM), 0)
            step = lax.shift_right_logical(row, int(math.log2(n_batch)))
            pos = start_pos + i * n_steps + step
            pooled = acc[:head] / jnp.minimum(w, pos + 1).astype(_F32)
            if head < rows:
                pooled = jnp.concatenate([pooled, acc[head:] * (1.0 / w)], axis=0)
        diffs.append(pooled - u_g)
    mixed = jnp.concatenate(
        [_dot(jnp.concatenate(diffs[0:2], axis=1).astype(_BF16), pool_w_ref[0]),
         _dot(jnp.concatenate(diffs[2:4], axis=1).astype(_BF16), pool_w_ref[1])], axis=1)
    a_out = (mixed * pool_scale_ref[...]).astype(_BF16)

    u_ssm = proj_ref[:, POOL_WIDTH:u_width]
    u_bf = u_ssm.astype(_BF16)
    for half in range(2):
        xs_ref[half] = _dot(u_bf[:, half * HALF_ROWS:(half + 1) * HALF_ROWS], b_ref[half])
    for half in range(2):
        _scan_half(xs_ref.at[half], s_re_ref, s_im_ref, ab_re_ref, ab_im_ref, half, n_batch, n_steps)
    ys = [_dot(xs_ref[half].astype(_BF16), c_ref[half]) for half in range(2)]
    y_ssm = jnp.concatenate(ys, axis=1) + d_ref[...] * u_ssm
    z = jax.nn.gelu(y_ssm)
    glu = jax.nn.sigmoid(_dot(z.astype(_BF16), glu_w_ref[...]) + glu_b_ref[...])
    b_out = (z * glu).astype(_BF16)

    merged = (jax.nn.sigmoid(proj_ref[:, u_width:u_width + D_MODEL]) * _dot(a_out, w_bp_ref[...])
              + jax.nn.sigmoid(proj_ref[:, u_width + D_MODEL:u_width + 2 * D_MODEL]) * _dot(b_out, w_bs_ref[...]))
    h_ref[...] = x_rows() + _dot(merged.astype(_BF16), w_out_ref[...])

    @pl.when(i == n_tiles - 1)
    def _():
        newest = hist_ref[n_batch:hist_rows, :] if hist_scratch else ext_ref[rows + n_batch:, :]
        hist_out_ref[...] = newest.reshape(POOL_HIST, n_batch, POOL_WIDTH)
        if state_batch_minor:
            s_re_out_ref[...] = s_re_ref[...].T
            s_im_out_ref[...] = s_im_ref[...].T


def _resident(shape):
    zeros = (0,) * len(shape)
    return pl.BlockSpec(shape, lambda i: zeros, pipeline_mode=pl.Buffered(1))


def _tiling(n_batch, n_steps_total, rows_per_step=ROW_TILE):
    n_steps = min(n_steps_total, rows_per_step // n_batch)
    n_tiles = n_steps_total // n_steps
    assert n_tiles * n_steps == n_steps_total and n_batch % SUBLANES == 0
    assert n_batch & (n_batch - 1) == 0
    return n_steps, n_tiles


def _mixer(x, state, weights, *, start_pos, state_batch_minor):
    n_batch, n_steps_total, _ = x.shape
    n_steps, n_tiles = _tiling(n_batch, n_steps_total)
    rows = n_steps * n_batch
    hist_rows = HIST_STEPS * n_batch
    state = () if state is None else tuple(state)
    state_shape = (N_STATES, n_batch) if state_batch_minor else (n_batch, N_STATES)
    kern = functools.partial(_mixer_kernel, n_batch=n_batch, n_steps=n_steps, n_tiles=n_tiles,
                             start_pos=start_pos, zero_state=not state, state_batch_minor=state_batch_minor)
    state_block = pl.BlockSpec(state_shape, lambda i: (0, 0))
    hist_window_bytes = (2 if state else 1) * POOL_HIST * n_batch * POOL_WIDTH * 4
    vmem_limit = min(VMEM_CAP_BYTES, VMEM_LIMIT_BYTES + hist_window_bytes - hist_window_bytes % (1 << 20))
    return pl.pallas_call(
        kern,
        grid=(n_tiles,),
        in_specs=[pl.BlockSpec(memory_space=pl.ANY) if n_tiles > 1
                  else pl.BlockSpec((n_batch, n_steps, D_MODEL), lambda i: (0, i, 0))]
        + [_resident(a.shape) for a in state + tuple(weights)],
        out_specs=(pl.BlockSpec((rows, D_MODEL), lambda i: (i, 0)),
                   pl.BlockSpec((POOL_HIST, n_batch, POOL_WIDTH), lambda i: (0, 0, 0)),
                   state_block, state_block),
        out_shape=(jax.ShapeDtypeStruct((n_steps_total * n_batch, D_MODEL), _F32),
                   jax.ShapeDtypeStruct((POOL_HIST, n_batch, POOL_WIDTH), _F32),
                   jax.ShapeDtypeStruct(state_shape, _F32),
                   jax.ShapeDtypeStruct(state_shape, _F32)),
        scratch_shapes=([pltpu.VMEM((hist_rows, POOL_WIDTH), _F32)] if n_tiles > 1 else [])
        + [pltpu.VMEM((2, n_steps, n_batch, D_MODEL) if n_tiles > 1 else (rows, D_MODEL), _F32),
           pltpu.VMEM((rows, IN_WIDTH), _F32),
           pltpu.VMEM((hist_rows + rows, POOL_WIDTH), _F32),
           pltpu.VMEM((2, rows, HALF_LANES), _F32)]
        + ([pltpu.VMEM((n_batch, N_STATES), _F32)] * 2 if state_batch_minor else [])
        + ([pltpu.SemaphoreType.DMA((2,))] if n_tiles > 1 else []),
        compiler_params=pltpu.CompilerParams(dimension_semantics=("arbitrary",), vmem_limit_bytes=vmem_limit),
        name="mixer",
    )(x, *state, *weights)


def _ffn_kernel(h_a_ref, h_b_ref, norm_ffn_ref, w_gate32_ref, w_up32_ref, w_down32_ref, norm_final_ref,
                y_a_ref, y_b_ref,
                w_gate_ref, w_up_ref, w_down_ref, gate_ref, up_ref, y_buf, sem, *, n_batch_a, n_sub, n_steps_a):
    i = pl.program_id(0)
    j = i - FFN_CAST_STEPS
    sub_rows = h_a_ref.shape[0] // n_sub
    sub_steps = sub_rows // n_batch_a
    steps = n_sub * sub_steps

    def tile(h):
        hn = _rmsnorm(h, norm_ffn_ref[...]).astype(_BF16)
        gate_ref[...] = _dot(hn, w_gate_ref[...])
        up_ref[...] = _dot(hn, w_up_ref[...])
        gate = gate_ref[...]
        f = (gate * jax.nn.sigmoid(gate) * up_ref[...]).astype(_BF16)
        return _rmsnorm(h + _dot(f, w_down_ref[...]), norm_final_ref[...])

    def copies(slot, step):
        t0 = step * steps if isinstance(step, int) else pl.multiple_of(step * steps, steps)
        return [pltpu.make_async_copy(y_buf.at[slot, :, b, :], y_a_ref.at[b, pl.ds(t0, steps), :], sem.at[slot])
                for b in range(n_batch_a)]

    @pl.when(i < FFN_CAST_STEPS)
    def _():
        for src, dst in ((w_gate32_ref, w_gate_ref), (w_up32_ref, w_up_ref), (w_down32_ref, w_down_ref)):
            r = src.shape[0]
            dst[pl.ds(pl.multiple_of(i * r, r), r), :] = src[...].astype(_BF16)

    @pl.when((j >= 0) & (j < n_steps_a))
    def _():
        slot = lax.rem(j, 2)
        for s in range(n_sub):
            y = tile(h_a_ref[s * sub_rows:(s + 1) * sub_rows, :])
            y_buf[slot, s * sub_steps:(s + 1) * sub_steps] = y.reshape(sub_steps, n_batch_a, D_MODEL)

        @pl.when(j > 0)
        def _():
            for c in copies(1 - slot, j - 1):
                c.wait()
        for c in copies(slot, j):
            c.start()

    @pl.when(j == n_steps_a)
    def _():
        n_batch_b, steps_b, _ = y_b_ref.shape
        y = tile(h_b_ref[...]).reshape(steps_b, n_batch_b, D_MODEL)
        y_b_ref[...] = jnp.swapaxes(y, 0, 1)
        for c in copies((n_steps_a - 1) % 2, n_steps_a - 1):
            c.wait()


def _ffn(h_a, h_b, norms, weights32, *, n_batch_a, n_batch_b):
    norm_ffn, norm_final = norms
    w_gate, w_up, w_down = weights32
    n_sub = FFN_SUB_TILES
    rows_a = ROW_TILE * n_sub
    n_steps_a = h_a.shape[0] // rows_a
    assert n_steps_a * rows_a == h_a.shape[0] and rows_a % n_batch_a == 0 and h_b.shape[0] == ROW_TILE
    assert all(w.shape[0] % (FFN_CAST_STEPS * BF16_ROWS) == 0 for w in weights32)
    t_a, t_b = h_a.shape[0] // n_batch_a, h_b.shape[0] // n_batch_b
    w_chunk = lambda w: pl.BlockSpec((w.shape[0] // FFN_CAST_STEPS, w.shape[1]),
                                     lambda i: (jnp.minimum(i, FFN_CAST_STEPS - 1), 0))
    return pl.pallas_call(
        functools.partial(_ffn_kernel, n_batch_a=n_batch_a, n_sub=n_sub, n_steps_a=n_steps_a),
        grid=(FFN_CAST_STEPS + n_steps_a + 1,),
        in_specs=[pl.BlockSpec((rows_a, D_MODEL), lambda i: (jnp.clip(i - FFN_CAST_STEPS, 0, n_steps_a - 1), 0)),
                  _resident(h_b.shape), _resident(norm_ffn.shape),
                  w_chunk(w_gate), w_chunk(w_up), w_chunk(w_down), _resident(norm_final.shape)],
        out_specs=(pl.BlockSpec(memory_space=pl.ANY),
                   pl.BlockSpec((n_batch_b, t_b, D_MODEL), lambda i: (0, 0, 0))),
        out_shape=(jax.ShapeDtypeStruct((n_batch_a, t_a, D_MODEL), _F32),
                   jax.ShapeDtypeStruct((n_batch_b, t_b, D_MODEL), _F32)),
        scratch_shapes=[pltpu.VMEM(w_gate.shape, _BF16), pltpu.VMEM(w_up.shape, _BF16), pltpu.VMEM(w_down.shape, _BF16),
                        pltpu.VMEM((ROW_TILE, D_FF), _F32), pltpu.VMEM((ROW_TILE, D_FF), _F32),
                        pltpu.VMEM((2, rows_a // n_batch_a, n_batch_a, D_MODEL), _F32),
                        pltpu.SemaphoreType.DMA((2,))],
        compiler_params=pltpu.CompilerParams(dimension_semantics=("arbitrary",),
                                             vmem_limit_bytes=VMEM_CAP_BYTES),
        name="ffn",
    )(h_a, h_b, norm_ffn, w_gate, w_up, w_down, norm_final)


def kernel(x_prompt, x_sample, state_pool, state_ssm_re, state_ssm_im, norm_mix, w_in, pool_w, pool_scale, ssm_a_re, ssm_a_im, ssm_log_dt, ssm_b_re, ssm_b_im, ssm_c_re, ssm_c_im, ssm_d, glu_w, glu_b, w_branch_pool, w_branch_ssm, w_out, norm_ffn, ffn_w_gate, ffn_w_up, ffn_w_down, norm_final):
    assert w_in.shape[0] == 1, "single-layer trunk"
    (ab_re, ab_im, b_mat, c_mat, pool_mat, w_in_bf, glu_w_bf, w_bp_bf, w_bs_bf, w_out_bf) = _prep(
        ssm_a_re[0], ssm_a_im[0], ssm_log_dt, ssm_b_re[0], ssm_b_im[0], ssm_c_re[0], ssm_c_im[0], pool_w[0],
        (w_in[0], glu_w[0], w_branch_pool[0], w_branch_ssm[0], w_out[0]))
    mixer_w = (norm_mix[0].reshape(1, D_MODEL), w_in_bf, pool_mat,
               pool_scale[0].reshape(1, POOL_WIDTH), ab_re, ab_im, b_mat, c_mat, ssm_d[0].reshape(1, SSM_WIDTH),
               glu_w_bf, glu_b[0].reshape(1, SSM_WIDTH), w_bp_bf, w_bs_bf, w_out_bf)
    ffn_norms = (norm_ffn[0].reshape(1, D_MODEL), norm_final.reshape(1, D_MODEL))

    def mix(x, state, start_pos):
        b = x.shape[0]
        batch_minor = b % LANES == 0
        if state is not None:
            flat = ((lambda a: a.transpose(1, 2, 0).reshape(N_STATES, b)) if batch_minor
                    else (lambda a: a.reshape(b, N_STATES)))
            state = (state[0].transpose(1, 0, 2), flat(state[1]), flat(state[2]))
        h_rows, new_hist, s_re, s_im = _mixer(x, state, mixer_w, start_pos=start_pos, state_batch_minor=batch_minor)
        unflat = ((lambda a: a.reshape(SSM_GROUPS, SSM_STATE, b).transpose(2, 0, 1)[None]) if batch_minor
                  else (lambda a: a.reshape(1, b, SSM_GROUPS, SSM_STATE)))
        return h_rows, new_hist.transpose(1, 0, 2)[None], unflat(s_re), unflat(s_im)

    h_p, pool_p, re_p, im_p = mix(x_prompt, None, 0)
    h_s, pool_s, re_s, im_s = mix(x_sample, (state_pool[0], state_ssm_re[0], state_ssm_im[0]), PAST_LEN)
    y_p, y_s = _ffn(h_p, h_s, ffn_norms, (ffn_w_gate[0], ffn_w_up[0], ffn_w_down[0]),
                    n_batch_a=x_prompt.shape[0], n_batch_b=x_sample.shape[0])
    return (y_p, y_s, pool_p, re_p, im_p, pool_s, re_s, im_s)
```

```python
import functools
import math

import jax
import jax.numpy as jnp
from jax import lax
from jax.experimental import pallas as pl
from jax.experimental.pallas import tpu as pltpu

D_MODEL = 1024
PAST_LEN = 16384
POOL_WIDTH = 512
POOL_WINDOWS = (2, 4, 8, 16)
POOL_GROUP_DIM = 128
POOL_HIST = 15
HIST_STEPS = POOL_HIST + 1
SSM_WIDTH = 512
SSM_GROUPS = 32
SSM_GROUP_DIM = 16
SSM_STATE = 64
N_STATES = SSM_GROUPS * SSM_STATE
IN_WIDTH = POOL_WIDTH + SSM_WIDTH + 2 * D_MODEL
D_FF = 2816
RMS_EPS = 1e-6

SUBLANES = 8
BF16_ROWS = 16
LANES = 128
MXU_DIM = 256
CHUNK_LANES = 2 * LANES
N_CHUNKS = N_STATES // LANES
HALF_CHUNKS = N_CHUNKS // 2
HALF_LANES = HALF_CHUNKS * CHUNK_LANES
HALF_ROWS = MXU_DIM
ROW_TILE = 512
PREP_STEPS = 4
N_STRUCTURED_IN = 8
N_STRUCTURED_OUT = 5
N_MIXER_WEIGHTS = 14
MIXER_SUB_TILES = 2
FFN_SUB_TILES = 2
VMEM_LIMIT_BYTES = 56 * 1024 * 1024
VMEM_CAP_BYTES = 60 * 1024 * 1024

_F32 = jnp.float32
_BF16 = jnp.bfloat16


def _rmsnorm(x, g):
    return x * lax.rsqrt(jnp.mean(x * x, axis=-1, keepdims=True) + RMS_EPS) * g


def _dot(a, b):
    return jnp.dot(a, b, preferred_element_type=_F32)


def _structured_weights(a_re_ref, a_im_ref, log_dt_ref, b_re_ref, b_im_ref, c_re_ref, c_im_ref, pool_w_ref,
                        ab_re_ref, ab_im_ref, b_mat_ref, c_mat_ref, pool_mat_ref):
    g_n, h_n, p_n = SSM_GROUPS, SSM_GROUP_DIM, SSM_STATE
    twice = lambda v: jnp.concatenate([v, v], axis=-1)

    diag = lax.broadcasted_iota(jnp.int32, (g_n, g_n), 0) == lax.broadcasted_iota(jnp.int32, (g_n, g_n), 1)
    log_dt = jnp.sum(jnp.where(diag, jnp.broadcast_to(log_dt_ref[...], (g_n, g_n)), 0.0), axis=1, keepdims=True)
    dt = jnp.exp(log_dt)
    a_re = a_re_ref[...]
    a_im = a_im_ref[...]
    mag = jnp.exp(a_re * dt)
    ab_re = mag * jnp.cos(a_im * dt)
    ab_im = mag * jnp.sin(a_im * dt)
    num_re = ab_re - 1.0
    num_im = ab_im
    den = a_re * a_re + a_im * a_im
    coef_re = ((num_re * a_re + num_im * a_im) / den)[:, None, :]
    coef_im = ((num_im * a_re - num_re * a_im) / den)[:, None, :]
    ab_re_ref[...] = twice(ab_re)
    ab_im_ref[...] = twice(ab_im)

    b_re = b_re_ref[...]
    b_im = b_im_ref[...]
    bp = [twice(v).reshape(g_n * h_n, LANES)
          for v in (coef_re * b_re - coef_im * b_im, coef_re * b_im + coef_im * b_re)]
    row = lax.broadcasted_iota(jnp.int32, (HALF_ROWS, LANES), 0)
    lane = lax.broadcasted_iota(jnp.int32, (HALF_ROWS, LANES), 1)
    own_lanes = (lane >> 6) == ((row >> 4) & 1)
    for half in range(2):
        for part in range(2):
            mine = jnp.where(own_lanes, bp[part][half * HALF_ROWS:(half + 1) * HALF_ROWS, :], 0.0)
            for cl in range(HALF_CHUNKS):
                blk = jnp.where((row >> 5) == cl, mine, 0.0)
                lo = cl * CHUNK_LANES + part * LANES
                b_mat_ref[half, :, lo:lo + LANES] = blk.astype(_BF16)

    row_c = lax.broadcasted_iota(jnp.int32, (LANES, HALF_ROWS), 0)
    col_c = lax.broadcasted_iota(jnp.int32, (LANES, HALF_ROWS), 1)
    own_cols = (row_c >> 6) == ((col_c >> 4) & 1)
    for part, c_ref in enumerate((c_re_ref, c_im_ref)):
        ct = c_ref[...].reshape(g_n * h_n, p_n).T
        ct = jnp.concatenate([ct, ct], axis=0)
        if part == 1:
            ct = -ct
        for half in range(2):
            mine = jnp.where(own_cols, ct[:, half * HALF_ROWS:(half + 1) * HALF_ROWS], 0.0)
            for cl in range(HALF_CHUNKS):
                blk = jnp.where((col_c >> 5) == cl, mine, 0.0)
                lo = cl * CHUNK_LANES + part * LANES
                c_mat_ref[half, lo:lo + LANES, :] = blk.astype(_BF16)

    pool_mat_ref[...] = jnp.zeros(pool_mat_ref.shape, _BF16)
    for g in range(len(POOL_WINDOWS)):
        lo = (g % 2) * POOL_GROUP_DIM
        pool_mat_ref[g // 2, lo:lo + POOL_GROUP_DIM, lo:lo + POOL_GROUP_DIM] = pool_w_ref[g].astype(_BF16)


def _prep_kernel(*refs):
    n_dense = (len(refs) - N_STRUCTURED_IN - N_STRUCTURED_OUT) // 2
    dense_in = refs[N_STRUCTURED_IN:N_STRUCTURED_IN + n_dense]
    structured_out = refs[N_STRUCTURED_IN + n_dense:N_STRUCTURED_IN + n_dense + N_STRUCTURED_OUT]
    dense_out = refs[N_STRUCTURED_IN + n_dense + N_STRUCTURED_OUT:]

    @pl.when(pl.program_id(0) == 0)
    def _():
        _structured_weights(*refs[:N_STRUCTURED_IN], *structured_out)

    for src, dst in zip(dense_in, dense_out):
        dst[...] = src[...].astype(_BF16)


def _prep(a_re, a_im, log_dt, b_re, b_im, c_re, c_im, pool_w, dense):
    structured = (a_re, a_im, log_dt, b_re.transpose(0, 2, 1), b_im.transpose(0, 2, 1), c_re, c_im, pool_w)
    structured_out = (jax.ShapeDtypeStruct((SSM_GROUPS, LANES), _F32), jax.ShapeDtypeStruct((SSM_GROUPS, LANES), _F32),
                      jax.ShapeDtypeStruct((2, HALF_ROWS, HALF_LANES), _BF16),
                      jax.ShapeDtypeStruct((2, HALF_LANES, HALF_ROWS), _BF16),
                      jax.ShapeDtypeStruct((len(POOL_WINDOWS) // 2, MXU_DIM, MXU_DIM), _BF16))
    assert len(structured) == N_STRUCTURED_IN and len(structured_out) == N_STRUCTURED_OUT
    assert all(w.shape[0] % (PREP_STEPS * BF16_ROWS) == 0 for w in dense)
    whole = lambda a: pl.BlockSpec(a.shape, lambda i, n=len(a.shape): (0,) * n)
    chunk = lambda w: pl.BlockSpec((w.shape[0] // PREP_STEPS, w.shape[1]), lambda i: (i, 0))
    return pl.pallas_call(
        _prep_kernel,
        grid=(PREP_STEPS,),
        in_specs=[whole(a) for a in structured] + [chunk(w) for w in dense],
        out_specs=[whole(a) for a in structured_out] + [chunk(w) for w in dense],
        out_shape=structured_out + tuple(jax.ShapeDtypeStruct(w.shape, _BF16) for w in dense),
        compiler_params=pltpu.CompilerParams(dimension_semantics=("arbitrary",)),
        name="prep",
    )(*structured, *dense)


def _scan_half(xs_ref, s_re_ref, s_im_ref, ab_re_ref, ab_im_ref, half, n_batch, n_steps):
    lane = lax.broadcasted_iota(jnp.int32, (1, LANES), 1)

    def coeff(ref, c):
        row = jnp.where(lane < SSM_STATE, ref[2 * c:2 * c + 1, :], ref[2 * c + 1:2 * c + 2, :])
        return jnp.broadcast_to(row, (SUBLANES, LANES))

    def recur(r0, row_of_step, cls):
        chunks = [half * HALF_CHUNKS + cl for cl in cls]
        ar = [coeff(ab_re_ref, c) for c in chunks]
        ai = [coeff(ab_im_ref, c) for c in chunks]
        sr = [s_re_ref[pl.ds(r0, SUBLANES), c * LANES:(c + 1) * LANES] for c in chunks]
        si = [s_im_ref[pl.ds(r0, SUBLANES), c * LANES:(c + 1) * LANES] for c in chunks]
        for t in range(n_steps):
            rows = pl.ds(row_of_step(t), SUBLANES)
            for k, cl in enumerate(cls):
                lo = cl * CHUNK_LANES
                nr = ar[k] * sr[k] - ai[k] * si[k] + xs_ref[rows, lo:lo + LANES]
                ni = ar[k] * si[k] + ai[k] * sr[k] + xs_ref[rows, lo + LANES:lo + CHUNK_LANES]
                xs_ref[rows, lo:lo + LANES] = nr
                xs_ref[rows, lo + LANES:lo + CHUNK_LANES] = ni
                sr[k], si[k] = nr, ni
        for k, c in enumerate(chunks):
            s_re_ref[pl.ds(r0, SUBLANES), c * LANES:(c + 1) * LANES] = sr[k]
            s_im_ref[pl.ds(r0, SUBLANES), c * LANES:(c + 1) * LANES] = si[k]

    for r0 in range(0, n_batch, SUBLANES):
        recur(r0, lambda t, r0=r0: t * n_batch + r0, list(range(HALF_CHUNKS)))


def _mixer_kernel(*refs, n_batch, n_steps, n_sub, n_tiles, start_pos, zero_state, state_batch_minor, n_cast):
    n_state_in = 0 if zero_state else 3
    x_ref = refs[0]
    hist0_ref, s0_re_ref, s0_im_ref = refs[1:1 + n_state_in] if n_state_in else (None, None, None)
    (norm_ref, w_in_ref, pool_w_ref, pool_scale_ref, ab_re_ref, ab_im_ref, b_ref, c_ref, d_ref, glu_w_ref,
     glu_b_ref, w_bp_ref, w_bs_ref, w_out_ref) = refs[1 + n_state_in:1 + n_state_in + N_MIXER_WEIGHTS]
    rest = refs[1 + n_state_in + N_MIXER_WEIGHTS:]
    cast_in, rest = rest[:n_cast], rest[n_cast:]
    (h_ref, hist_out_ref, s_re_out_ref, s_im_out_ref), rest = rest[:4], rest[4:]
    cast_out, scratch = rest[:n_cast], rest[n_cast:]
    i = pl.program_id(0)
    rows = n_batch * n_steps
    sub_steps = n_steps // n_sub
    sub_rows = n_batch * sub_steps
    hist_rows = HIST_STEPS * n_batch

    scratch = list(scratch)
    hist_scratch = [scratch.pop(0)] if n_tiles > 1 else []
    x_stage_ref, proj_ref, ext_ref, xs_ref = scratch[:4]
    s_re_ref, s_im_ref = scratch[4:6] if state_batch_minor else (s_re_out_ref, s_im_out_ref)
    dma_in = n_tiles > 1
    x_sem = scratch[-1] if dma_in else None
    hist_ref = hist_scratch[0] if hist_scratch else ext_ref.at[0:hist_rows]
    @pl.when(i == 0)
    def _():
        if zero_state:
            hist_ref[...] = jnp.zeros((hist_rows, POOL_WIDTH), _F32)
            s_re_ref[...] = jnp.zeros(s_re_ref.shape, _F32)
            s_im_ref[...] = jnp.zeros(s_im_ref.shape, _F32)
        else:
            hist_ref[0:n_batch, :] = jnp.zeros((n_batch, POOL_WIDTH), _F32)
            hist_ref[n_batch:hist_rows, :] = hist0_ref[...].reshape(POOL_HIST * n_batch, POOL_WIDTH)
            s_re_ref[...] = s0_re_ref[...].T if state_batch_minor else s0_re_ref[...]
            s_im_ref[...] = s0_im_ref[...].T if state_batch_minor else s0_im_ref[...]

    if dma_in:
        slot = lax.rem(i, 2)

        def copies(slot_, step):
            t0 = pl.multiple_of(step * n_steps, n_steps)
            return [pltpu.make_async_copy(x_ref.at[b, pl.ds(t0, n_steps), :], x_stage_ref.at[slot_, :, b, :],
                                          x_sem.at[slot_]) for b in range(n_batch)]

        @pl.when(i == 0)
        def _():
            for c in copies(0, 0):
                c.start()
        for c in copies(slot, i):
            c.wait()

        @pl.when(i + 1 < n_tiles)
        def _():
            for c in copies(1 - slot, i + 1):
                c.start()
        x_rows = lambda sub: x_stage_ref[slot, sub * sub_steps:(sub + 1) * sub_steps].reshape(sub_rows, D_MODEL)
    else:
        x_stage_ref[...] = jnp.swapaxes(x_ref[...], 0, 1).reshape(rows, D_MODEL)
        x_rows = lambda sub: x_stage_ref[sub * sub_rows:(sub + 1) * sub_rows, :]
    u_width = POOL_WIDTH + SSM_WIDTH
    if hist_scratch:
        ext_ref[0:hist_rows, :] = hist_ref[...]

    def tile(sub):
        base = hist_rows + sub * sub_rows
        xn = _rmsnorm(x_rows(sub), norm_ref[...]).astype(_BF16)
        proj_ref[...] = _dot(xn, w_in_ref[...])

        ext_ref[base:base + sub_rows, :] = proj_ref[:, 0:POOL_WIDTH]
        diffs = []
        for g, w in enumerate(POOL_WINDOWS):
            lo = g * POOL_GROUP_DIM
            u_g = ext_ref[base:base + sub_rows, lo:lo + POOL_GROUP_DIM]
            acc = u_g
            for j in range(1, w):
                r0 = base - j * n_batch
                acc = acc + ext_ref[r0:r0 + sub_rows, lo:lo + POOL_GROUP_DIM]
            if start_pos + sub * sub_steps + 1 >= w:
                pooled = acc * (1.0 / w)
            else:
                head = min(sub_rows, hist_rows)
                row = lax.broadcasted_iota(jnp.int32, (head, POOL_GROUP_DIM), 0)
                step = lax.shift_right_logical(row, int(math.log2(n_batch)))
                pos = start_pos + i * n_steps + sub * sub_steps + step
                pooled = acc[:head] / jnp.minimum(w, pos + 1).astype(_F32)
                if head < sub_rows:
                    pooled = jnp.concatenate([pooled, acc[head:] * (1.0 / w)], axis=0)
            diffs.append(pooled - u_g)
        mixed = jnp.concatenate(
            [_dot(jnp.concatenate(diffs[0:2], axis=1).astype(_BF16), pool_w_ref[0]),
             _dot(jnp.concatenate(diffs[2:4], axis=1).astype(_BF16), pool_w_ref[1])], axis=1)
        a_out = (mixed * pool_scale_ref[...]).astype(_BF16)

        u_ssm = proj_ref[:, POOL_WIDTH:u_width]
        u_bf = u_ssm.astype(_BF16)
        for half in range(2):
            xs_ref[half] = _dot(u_bf[:, half * HALF_ROWS:(half + 1) * HALF_ROWS], b_ref[half])
        for half in range(2):
            _scan_half(xs_ref.at[half], s_re_ref, s_im_ref, ab_re_ref, ab_im_ref, half, n_batch, sub_steps)
        ys = [_dot(xs_ref[half].astype(_BF16), c_ref[half]) for half in range(2)]
        y_ssm = jnp.concatenate(ys, axis=1) + d_ref[...] * u_ssm
        z = jax.nn.gelu(y_ssm)
        glu = jax.nn.sigmoid(_dot(z.astype(_BF16), glu_w_ref[...]) + glu_b_ref[...])
        b_out = (z * glu).astype(_BF16)

        merged = (jax.nn.sigmoid(proj_ref[:, u_width:u_width + D_MODEL]) * _dot(a_out, w_bp_ref[...])
                  + jax.nn.sigmoid(proj_ref[:, u_width + D_MODEL:u_width + 2 * D_MODEL]) * _dot(b_out, w_bs_ref[...]))
        h_ref[sub * sub_rows:(sub + 1) * sub_rows, :] = x_rows(sub) + _dot(merged.astype(_BF16), w_out_ref[...])

    for sub in range(n_sub):
        tile(sub)
    if hist_scratch:
        hist_ref[...] = ext_ref[rows:rows + hist_rows, :]

    for src, dst in zip(cast_in, cast_out):
        dst[...] = src[...].astype(_BF16)

    @pl.when(i == n_tiles - 1)
    def _():
        newest = hist_ref[n_batch:hist_rows, :] if hist_scratch else ext_ref[rows + n_batch:, :]
        hist_out_ref[...] = newest.reshape(POOL_HIST, n_batch, POOL_WIDTH)
        if state_batch_minor:
            s_re_out_ref[...] = s_re_ref[...].T
            s_im_out_ref[...] = s_im_ref[...].T


def _resident(shape):
    zeros = (0,) * len(shape)
    return pl.BlockSpec(shape, lambda i: zeros, pipeline_mode=pl.Buffered(1))


def _tiling(n_batch, n_steps_total, rows_per_step=ROW_TILE):
    n_steps = min(n_steps_total, rows_per_step // n_batch)
    n_tiles = n_steps_total // n_steps
    assert n_tiles * n_steps == n_steps_total and n_batch % SUBLANES == 0
    assert n_batch & (n_batch - 1) == 0
    return n_steps, n_tiles


def _mixer(x, state, weights, *, start_pos, state_batch_minor, cast=()):
    n_batch, n_steps_total, _ = x.shape
    n_sub = max(1, min(MIXER_SUB_TILES, n_steps_total * n_batch // ROW_TILE))
    n_steps, n_tiles = _tiling(n_batch, n_steps_total, ROW_TILE * n_sub)
    rows = n_steps * n_batch
    sub_rows = rows // n_sub
    hist_rows = HIST_STEPS * n_batch
    state = () if state is None else tuple(state)
    state_shape = (N_STATES, n_batch) if state_batch_minor else (n_batch, N_STATES)
    assert len(weights) == N_MIXER_WEIGHTS
    kern = functools.partial(_mixer_kernel, n_batch=n_batch, n_steps=n_steps, n_sub=n_sub, n_tiles=n_tiles,
                             start_pos=start_pos,
                             zero_state=not state, state_batch_minor=state_batch_minor, n_cast=len(cast))

    def cast_chunks(w):
        n = max(k for k in range(1, n_tiles + 1) if w.shape[0] % (k * BF16_ROWS) == 0)
        return pl.BlockSpec((w.shape[0] // n, w.shape[1]), lambda i: (jnp.minimum(i, n - 1), 0))
    state_block = pl.BlockSpec(state_shape, lambda i: (0, 0))
    hist_window_bytes = (2 if state else 1) * POOL_HIST * n_batch * POOL_WIDTH * 4
    vmem_limit = min(VMEM_CAP_BYTES, VMEM_LIMIT_BYTES + hist_window_bytes - hist_window_bytes % (1 << 20))
    return pl.pallas_call(
        kern,
        grid=(n_tiles,),
        in_specs=[pl.BlockSpec(memory_space=pl.ANY) if n_tiles > 1
                  else pl.BlockSpec((n_batch, n_steps, D_MODEL), lambda i: (0, i, 0))]
        + [_resident(a.shape) for a in state + tuple(weights)] + [cast_chunks(w) for w in cast],
        out_specs=[pl.BlockSpec((rows, D_MODEL), lambda i: (i, 0)),
                   pl.BlockSpec((POOL_HIST, n_batch, POOL_WIDTH), lambda i: (0, 0, 0)),
                   state_block, state_block] + [cast_chunks(w) for w in cast],
        out_shape=[jax.ShapeDtypeStruct((n_steps_total * n_batch, D_MODEL), _F32),
                   jax.ShapeDtypeStruct((POOL_HIST, n_batch, POOL_WIDTH), _F32),
                   jax.ShapeDtypeStruct(state_shape, _F32),
                   jax.ShapeDtypeStruct(state_shape, _F32)] + [jax.ShapeDtypeStruct(w.shape, _BF16) for w in cast],
        scratch_shapes=([pltpu.VMEM((hist_rows, POOL_WIDTH), _F32)] if n_tiles > 1 else [])
        + [pltpu.VMEM((2, n_steps, n_batch, D_MODEL) if n_tiles > 1 else (rows, D_MODEL), _F32),
           pltpu.VMEM((sub_rows, IN_WIDTH), _F32),
           pltpu.VMEM((hist_rows + rows, POOL_WIDTH), _F32),
           pltpu.VMEM((2, sub_rows, HALF_LANES), _F32)]
        + ([pltpu.VMEM((n_batch, N_STATES), _F32)] * 2 if state_batch_minor else [])
        + ([pltpu.SemaphoreType.DMA((2,))] if n_tiles > 1 else []),
        compiler_params=pltpu.CompilerParams(dimension_semantics=("arbitrary",), vmem_limit_bytes=vmem_limit),
        name="mixer",
    )(x, *state, *weights, *cast)


def _ffn_kernel(h_a_ref, h_b_ref, norm_ffn_ref, w_gate_ref, w_up_ref, w_down_ref, norm_final_ref,
                y_a_ref, y_b_ref, gate_ref, up_ref, y_buf, sem, *, n_batch_a, n_sub, n_steps_a):
    j = pl.program_id(0)
    sub_rows = h_a_ref.shape[0] // n_sub
    sub_steps = sub_rows // n_batch_a
    steps = n_sub * sub_steps

    def tile(h):
        hn = _rmsnorm(h, norm_ffn_ref[...]).astype(_BF16)
        gate_ref[...] = _dot(hn, w_gate_ref[...])
        up_ref[...] = _dot(hn, w_up_ref[...])
        gate = gate_ref[...]
        f = (gate * jax.nn.sigmoid(gate) * up_ref[...]).astype(_BF16)
        return _rmsnorm(h + _dot(f, w_down_ref[...]), norm_final_ref[...])

    def copies(slot, step):
        t0 = step * steps if isinstance(step, int) else pl.multiple_of(step * steps, steps)
        return [pltpu.make_async_copy(y_buf.at[slot, :, b, :], y_a_ref.at[b, pl.ds(t0, steps), :], sem.at[slot])
                for b in range(n_batch_a)]

    @pl.when(j < n_steps_a)
    def _():
        slot = lax.rem(j, 2)
        for s in range(n_sub):
            y = tile(h_a_ref[s * sub_rows:(s + 1) * sub_rows, :])
            y_buf[slot, s * sub_steps:(s + 1) * sub_steps] = y.reshape(sub_steps, n_batch_a, D_MODEL)

        @pl.when(j > 0)
        def _():
            for c in copies(1 - slot, j - 1):
                c.wait()
        for c in copies(slot, j):
            c.start()

    @pl.when(j == n_steps_a)
    def _():
        n_batch_b, steps_b, _ = y_b_ref.shape
        y = tile(h_b_ref[...]).reshape(steps_b, n_batch_b, D_MODEL)
        y_b_ref[...] = jnp.swapaxes(y, 0, 1)
        for c in copies((n_steps_a - 1) % 2, n_steps_a - 1):
            c.wait()


def _ffn(h_a, h_b, norms, weights, *, n_batch_a, n_batch_b):
    norm_ffn, norm_final = norms
    w_gate, w_up, w_down = weights
    n_sub = FFN_SUB_TILES
    rows_a = ROW_TILE * n_sub
    n_steps_a = h_a.shape[0] // rows_a
    assert n_steps_a * rows_a == h_a.shape[0] and rows_a % n_batch_a == 0 and h_b.shape[0] == ROW_TILE
    t_a, t_b = h_a.shape[0] // n_batch_a, h_b.shape[0] // n_batch_b
    return pl.pallas_call(
        functools.partial(_ffn_kernel, n_batch_a=n_batch_a, n_sub=n_sub, n_steps_a=n_steps_a),
        grid=(n_steps_a + 1,),
        in_specs=[pl.BlockSpec((rows_a, D_MODEL), lambda i: (jnp.minimum(i, n_steps_a - 1), 0)),
                  _resident(h_b.shape), _resident(norm_ffn.shape),
                  _resident(w_gate.shape), _resident(w_up.shape), _resident(w_down.shape),
                  _resident(norm_final.shape)],
        out_specs=(pl.BlockSpec(memory_space=pl.ANY),
                   pl.BlockSpec((n_batch_b, t_b, D_MODEL), lambda i: (0, 0, 0))),
        out_shape=(jax.ShapeDtypeStruct((n_batch_a, t_a, D_MODEL), _F32),
                   jax.ShapeDtypeStruct((n_batch_b, t_b, D_MODEL), _F32)),
        scratch_shapes=[pltpu.VMEM((ROW_TILE, D_FF), _F32), pltpu.VMEM((ROW_TILE, D_FF), _F32),
                        pltpu.VMEM((2, rows_a // n_batch_a, n_batch_a, D_MODEL), _F32),
                        pltpu.SemaphoreType.DMA((2,))],
        compiler_params=pltpu.CompilerParams(dimension_semantics=("arbitrary",),
                                             vmem_limit_bytes=VMEM_LIMIT_BYTES),
        name="ffn",
    )(h_a, h_b, norm_ffn, w_gate, w_up, w_down, norm_final)


def kernel(x_prompt, x_sample, state_pool, state_ssm_re, state_ssm_im, norm_mix, w_in, pool_w, pool_scale, ssm_a_re, ssm_a_im, ssm_log_dt, ssm_b_re, ssm_b_im, ssm_c_re, ssm_c_im, ssm_d, glu_w, glu_b, w_branch_pool, w_branch_ssm, w_out, norm_ffn, ffn_w_gate, ffn_w_up, ffn_w_down, norm_final):
    assert w_in.shape[0] == 1, "single-layer trunk"
    (ab_re, ab_im, b_mat, c_mat, pool_mat, w_in_bf, glu_w_bf, w_bp_bf, w_bs_bf, w_out_bf) = _prep(
        ssm_a_re[0], ssm_a_im[0], ssm_log_dt, ssm_b_re[0], ssm_b_im[0], ssm_c_re[0], ssm_c_im[0], pool_w[0],
        (w_in[0], glu_w[0], w_branch_pool[0], w_branch_ssm[0], w_out[0]))
    mixer_w = (norm_mix[0].reshape(1, D_MODEL), w_in_bf, pool_mat,
               pool_scale[0].reshape(1, POOL_WIDTH), ab_re, ab_im, b_mat, c_mat, ssm_d[0].reshape(1, SSM_WIDTH),
               glu_w_bf, glu_b[0].reshape(1, SSM_WIDTH), w_bp_bf, w_bs_bf, w_out_bf)
    ffn_norms = (norm_ffn[0].reshape(1, D_MODEL), norm_final.reshape(1, D_MODEL))

    def mix(x, state, start_pos, cast=()):
        b = x.shape[0]
        batch_minor = b % LANES == 0
        if state is not None:
            flat = ((lambda a: a.transpose(1, 2, 0).reshape(N_STATES, b)) if batch_minor
                    else (lambda a: a.reshape(b, N_STATES)))
            state = (state[0].transpose(1, 0, 2), flat(state[1]), flat(state[2]))
        h_rows, new_hist, s_re, s_im, *cast_bf = _mixer(x, state, mixer_w, start_pos=start_pos,
                                                       state_batch_minor=batch_minor, cast=cast)
        unflat = ((lambda a: a.reshape(SSM_GROUPS, SSM_STATE, b).transpose(2, 0, 1)[None]) if batch_minor
                  else (lambda a: a.reshape(1, b, SSM_GROUPS, SSM_STATE)))
        return h_rows, new_hist.transpose(1, 0, 2)[None], unflat(s_re), unflat(s_im), cast_bf

    h_p, pool_p, re_p, im_p, ffn_w = mix(x_prompt, None, 0, cast=(ffn_w_gate[0], ffn_w_up[0], ffn_w_down[0]))
    h_s, pool_s, re_s, im_s, _ = mix(x_sample, (state_pool[0], state_ssm_re[0], state_ssm_im[0]), PAST_LEN)
    y_p, y_s = _ffn(h_p, h_s, ffn_norms, ffn_w,
                    n_batch_a=x_prompt.shape[0], n_batch_b=x_sample.shape[0])
    return (y_p, y_s, pool_p, re_p, im_p, pool_s, re_s, im_s)
```

```python
import functools
import math

import jax
import jax.numpy as jnp
from jax import lax
from jax.experimental import pallas as pl
from jax.experimental.pallas import tpu as pltpu

D_MODEL = 1024
PAST_LEN = 16384
POOL_WIDTH = 512
POOL_WINDOWS = (2, 4, 8, 16)
POOL_GROUP_DIM = 128
POOL_HIST = 15
HIST_STEPS = POOL_HIST + 1
SSM_WIDTH = 512
SSM_GROUPS = 32
SSM_GROUP_DIM = 16
SSM_STATE = 64
N_STATES = SSM_GROUPS * SSM_STATE
IN_WIDTH = POOL_WIDTH + SSM_WIDTH + 2 * D_MODEL
D_FF = 2816
RMS_EPS = 1e-6

SUBLANES = 8
BF16_ROWS = 16
LANES = 128
MXU_DIM = 256
CHUNK_LANES = 2 * LANES
N_CHUNKS = N_STATES // LANES
HALF_CHUNKS = N_CHUNKS // 2
HALF_LANES = HALF_CHUNKS * CHUNK_LANES
HALF_ROWS = MXU_DIM
ROW_TILE = 512
PREP_STEPS = 4
N_STRUCTURED_IN = 8
N_STRUCTURED_OUT = 5
N_MIXER_WEIGHTS = 14
MIXER_SUB_TILES = 2
FFN_SUB_TILES = 2
VMEM_LIMIT_BYTES = 56 * 1024 * 1024
VMEM_CAP_BYTES = 60 * 1024 * 1024

_F32 = jnp.float32
_BF16 = jnp.bfloat16


def _rmsnorm(x, g):
    return x * lax.rsqrt(jnp.mean(x * x, axis=-1, keepdims=True) + RMS_EPS) * g


def _dot(a, b):
    return jnp.dot(a, b, preferred_element_type=_F32)


def _structured_weights(a_re_ref, a_im_ref, log_dt_ref, b_re_ref, b_im_ref, c_re_ref, c_im_ref, pool_w_ref,
                        ab_re_ref, ab_im_ref, b_mat_ref, c_mat_ref, pool_mat_ref):
    g_n, h_n, p_n = SSM_GROUPS, SSM_GROUP_DIM, SSM_STATE
    twice = lambda v: jnp.concatenate([v, v], axis=-1)

    diag = lax.broadcasted_iota(jnp.int32, (g_n, g_n), 0) == lax.broadcasted_iota(jnp.int32, (g_n, g_n), 1)
    log_dt = jnp.sum(jnp.where(diag, jnp.broadcast_to(log_dt_ref[...], (g_n, g_n)), 0.0), axis=1, keepdims=True)
    dt = jnp.exp(log_dt)
    a_re = a_re_ref[...]
    a_im = a_im_ref[...]
    mag = jnp.exp(a_re * dt)
    ab_re = mag * jnp.cos(a_im * dt)
    ab_im = mag * jnp.sin(a_im * dt)
    num_re = ab_re - 1.0
    num_im = ab_im
    den = a_re * a_re + a_im * a_im
    coef_re = ((num_re * a_re + num_im * a_im) / den)[:, None, :]
    coef_im = ((num_im * a_re - num_re * a_im) / den)[:, None, :]
    ab_re_ref[...] = twice(ab_re)
    ab_im_ref[...] = twice(ab_im)

    b_re = b_re_ref[...]
    b_im = b_im_ref[...]
    bp = [twice(v).reshape(g_n * h_n, LANES)
          for v in (coef_re * b_re - coef_im * b_im, coef_re * b_im + coef_im * b_re)]
    row = lax.broadcasted_iota(jnp.int32, (HALF_ROWS, LANES), 0)
    lane = lax.broadcasted_iota(jnp.int32, (HALF_ROWS, LANES), 1)
    own_lanes = (lane >> 6) == ((row >> 4) & 1)
    for half in range(2):
        for part in range(2):
            mine = jnp.where(own_lanes, bp[part][half * HALF_ROWS:(half + 1) * HALF_ROWS, :], 0.0)
            for cl in range(HALF_CHUNKS):
                blk = jnp.where((row >> 5) == cl, mine, 0.0)
                lo = cl * CHUNK_LANES + part * LANES
                b_mat_ref[half, :, lo:lo + LANES] = blk.astype(_BF16)

    row_c = lax.broadcasted_iota(jnp.int32, (LANES, HALF_ROWS), 0)
    col_c = lax.broadcasted_iota(jnp.int32, (LANES, HALF_ROWS), 1)
    own_cols = (row_c >> 6) == ((col_c >> 4) & 1)
    for part, c_ref in enumerate((c_re_ref, c_im_ref)):
        ct = c_ref[...].reshape(g_n * h_n, p_n).T
        ct = jnp.concatenate([ct, ct], axis=0)
        if part == 1:
            ct = -ct
        for half in range(2):
            mine = jnp.where(own_cols, ct[:, half * HALF_ROWS:(half + 1) * HALF_ROWS], 0.0)
            for cl in range(HALF_CHUNKS):
                blk = jnp.where((col_c >> 5) == cl, mine, 0.0)
                lo = cl * CHUNK_LANES + part * LANES
                c_mat_ref[half, lo:lo + LANES, :] = blk.astype(_BF16)

    pool_mat_ref[...] = jnp.zeros(pool_mat_ref.shape, _BF16)
    for g in range(len(POOL_WINDOWS)):
        lo = (g % 2) * POOL_GROUP_DIM
        pool_mat_ref[g // 2, lo:lo + POOL_GROUP_DIM, lo:lo + POOL_GROUP_DIM] = pool_w_ref[g].astype(_BF16)


def _prep_kernel(*refs):
    n_dense = (len(refs) - N_STRUCTURED_IN - N_STRUCTURED_OUT) // 2
    dense_in = refs[N_STRUCTURED_IN:N_STRUCTURED_IN + n_dense]
    structured_out = refs[N_STRUCTURED_IN + n_dense:N_STRUCTURED_IN + n_dense + N_STRUCTURED_OUT]
    dense_out = refs[N_STRUCTURED_IN + n_dense + N_STRUCTURED_OUT:]

    @pl.when(pl.program_id(0) == 0)
    def _():
        _structured_weights(*refs[:N_STRUCTURED_IN], *structured_out)

    for src, dst in zip(dense_in, dense_out):
        dst[...] = src[...].astype(_BF16)


def _prep(a_re, a_im, log_dt, b_re, b_im, c_re, c_im, pool_w, dense):
    structured = (a_re, a_im, log_dt, b_re.transpose(0, 2, 1), b_im.transpose(0, 2, 1), c_re, c_im, pool_w)
    structured_out = (jax.ShapeDtypeStruct((SSM_GROUPS, LANES), _F32), jax.ShapeDtypeStruct((SSM_GROUPS, LANES), _F32),
                      jax.ShapeDtypeStruct((2, HALF_ROWS, HALF_LANES), _BF16),
                      jax.ShapeDtypeStruct((2, HALF_LANES, HALF_ROWS), _BF16),
                      jax.ShapeDtypeStruct((len(POOL_WINDOWS) // 2, MXU_DIM, MXU_DIM), _BF16))
    assert len(structured) == N_STRUCTURED_IN and len(structured_out) == N_STRUCTURED_OUT
    assert all(w.shape[0] % (PREP_STEPS * BF16_ROWS) == 0 for w in dense)
    whole = lambda a: pl.BlockSpec(a.shape, lambda i, n=len(a.shape): (0,) * n)
    chunk = lambda w: pl.BlockSpec((w.shape[0] // PREP_STEPS, w.shape[1]), lambda i: (i, 0))
    return pl.pallas_call(
        _prep_kernel,
        grid=(PREP_STEPS,),
        in_specs=[whole(a) for a in structured] + [chunk(w) for w in dense],
        out_specs=[whole(a) for a in structured_out] + [chunk(w) for w in dense],
        out_shape=structured_out + tuple(jax.ShapeDtypeStruct(w.shape, _BF16) for w in dense),
        compiler_params=pltpu.CompilerParams(dimension_semantics=("arbitrary",)),
        name="prep",
    )(*structured, *dense)


def _scan_half(xs_ref, s_re_ref, s_im_ref, ab_re_ref, ab_im_ref, half, n_batch, n_steps):
    lane = lax.broadcasted_iota(jnp.int32, (1, LANES), 1)

    def coeff(ref, c):
        row = jnp.where(lane < SSM_STATE, ref[2 * c:2 * c + 1, :], ref[2 * c + 1:2 * c + 2, :])
        return jnp.broadcast_to(row, (SUBLANES, LANES))

    def recur(r0, row_of_step, cls):
        chunks = [half * HALF_CHUNKS + cl for cl in cls]
        ar = [coeff(ab_re_ref, c) for c in chunks]
        ai = [coeff(ab_im_ref, c) for c in chunks]
        sr = [s_re_ref[pl.ds(r0, SUBLANES), c * LANES:(c + 1) * LANES] for c in chunks]
        si = [s_im_ref[pl.ds(r0, SUBLANES), c * LANES:(c + 1) * LANES] for c in chunks]
        for t in range(n_steps):
            rows = pl.ds(row_of_step(t), SUBLANES)
            for k, cl in enumerate(cls):
                lo = cl * CHUNK_LANES
                nr = ar[k] * sr[k] - ai[k] * si[k] + xs_ref[rows, lo:lo + LANES]
                ni = ar[k] * si[k] + ai[k] * sr[k] + xs_ref[rows, lo + LANES:lo + CHUNK_LANES]
                xs_ref[rows, lo:lo + LANES] = nr
                xs_ref[rows, lo + LANES:lo + CHUNK_LANES] = ni
                sr[k], si[k] = nr, ni
        for k, c in enumerate(chunks):
            s_re_ref[pl.ds(r0, SUBLANES), c * LANES:(c + 1) * LANES] = sr[k]
            s_im_ref[pl.ds(r0, SUBLANES), c * LANES:(c + 1) * LANES] = si[k]

    for r0 in range(0, n_batch, SUBLANES):
        recur(r0, lambda t, r0=r0: t * n_batch + r0, list(range(HALF_CHUNKS)))


def _staggered(b, n_batch, n_steps):
    off = b * (n_steps // n_batch)
    return [(lo, hi) for lo, hi in ((off, n_steps), (0, off)) if hi > lo]


def _mixer_kernel(*refs, n_batch, n_steps, n_sub, n_tiles, start_pos, zero_state, state_batch_minor, n_cast):
    n_state_in = 0 if zero_state else 3
    x_ref = refs[0]
    hist0_ref, s0_re_ref, s0_im_ref = refs[1:1 + n_state_in] if n_state_in else (None, None, None)
    (norm_ref, w_in_ref, pool_w_ref, pool_scale_ref, ab_re_ref, ab_im_ref, b_ref, c_ref, d_ref, glu_w_ref,
     glu_b_ref, w_bp_ref, w_bs_ref, w_out_ref) = refs[1 + n_state_in:1 + n_state_in + N_MIXER_WEIGHTS]
    rest = refs[1 + n_state_in + N_MIXER_WEIGHTS:]
    cast_in, rest = rest[:n_cast], rest[n_cast:]
    (h_ref, hist_out_ref, s_re_out_ref, s_im_out_ref), rest = rest[:4], rest[4:]
    cast_out, scratch = rest[:n_cast], rest[n_cast:]
    i = pl.program_id(0)
    rows = n_batch * n_steps
    sub_steps = n_steps // n_sub
    sub_rows = n_batch * sub_steps
    hist_rows = HIST_STEPS * n_batch

    scratch = list(scratch)
    hist_scratch = [scratch.pop(0)] if n_tiles > 1 else []
    x_stage_ref, proj_ref, ext_ref, xs_ref = scratch[:4]
    s_re_ref, s_im_ref = scratch[4:6] if state_batch_minor else (s_re_out_ref, s_im_out_ref)
    dma_in = n_tiles > 1
    x_sem = scratch[-1] if dma_in else None
    hist_ref = hist_scratch[0] if hist_scratch else ext_ref.at[0:hist_rows]
    @pl.when(i == 0)
    def _():
        if zero_state:
            hist_ref[...] = jnp.zeros((hist_rows, POOL_WIDTH), _F32)
            s_re_ref[...] = jnp.zeros(s_re_ref.shape, _F32)
            s_im_ref[...] = jnp.zeros(s_im_ref.shape, _F32)
        else:
            hist_ref[0:n_batch, :] = jnp.zeros((n_batch, POOL_WIDTH), _F32)
            hist_ref[n_batch:hist_rows, :] = hist0_ref[...].reshape(POOL_HIST * n_batch, POOL_WIDTH)
            s_re_ref[...] = s0_re_ref[...].T if state_batch_minor else s0_re_ref[...]
            s_im_ref[...] = s0_im_ref[...].T if state_batch_minor else s0_im_ref[...]

    if dma_in:
        slot = lax.rem(i, 2)

        def copies(slot_, step):
            t0 = pl.multiple_of(step * n_steps, n_steps)
            return [pltpu.make_async_copy(x_ref.at[b, pl.ds(t0 + lo, hi - lo), :], x_stage_ref.at[slot_, lo:hi, b, :],
                                          x_sem.at[slot_])
                    for b in range(n_batch) for lo, hi in _staggered(b, n_batch, n_steps)]

        @pl.when(i == 0)
        def _():
            for c in copies(0, 0):
                c.start()
        for c in copies(slot, i):
            c.wait()

        @pl.when(i + 1 < n_tiles)
        def _():
            for c in copies(1 - slot, i + 1):
                c.start()
        x_rows = lambda sub: x_stage_ref[slot, sub * sub_steps:(sub + 1) * sub_steps].reshape(sub_rows, D_MODEL)
    else:
        x_stage_ref[...] = jnp.swapaxes(x_ref[...], 0, 1).reshape(rows, D_MODEL)
        x_rows = lambda sub: x_stage_ref[sub * sub_rows:(sub + 1) * sub_rows, :]
    u_width = POOL_WIDTH + SSM_WIDTH
    if hist_scratch:
        ext_ref[0:hist_rows, :] = hist_ref[...]

    def tile(sub):
        base = hist_rows + sub * sub_rows
        xn = _rmsnorm(x_rows(sub), norm_ref[...]).astype(_BF16)
        proj_ref[...] = _dot(xn, w_in_ref[...])

        ext_ref[base:base + sub_rows, :] = proj_ref[:, 0:POOL_WIDTH]
        diffs = []
        for g, w in enumerate(POOL_WINDOWS):
            lo = g * POOL_GROUP_DIM
            u_g = ext_ref[base:base + sub_rows, lo:lo + POOL_GROUP_DIM]
            acc = u_g
            for j in range(1, w):
                r0 = base - j * n_batch
                acc = acc + ext_ref[r0:r0 + sub_rows, lo:lo + POOL_GROUP_DIM]
            if start_pos + sub * sub_steps + 1 >= w:
                pooled = acc * (1.0 / w)
            else:
                head = min(sub_rows, hist_rows)
                row = lax.broadcasted_iota(jnp.int32, (head, POOL_GROUP_DIM), 0)
                step = lax.shift_right_logical(row, int(math.log2(n_batch)))
                pos = start_pos + i * n_steps + sub * sub_steps + step
                pooled = acc[:head] / jnp.minimum(w, pos + 1).astype(_F32)
                if head < sub_rows:
                    pooled = jnp.concatenate([pooled, acc[head:] * (1.0 / w)], axis=0)
            diffs.append(pooled - u_g)
        mixed = jnp.concatenate(
            [_dot(jnp.concatenate(diffs[0:2], axis=1).astype(_BF16), pool_w_ref[0]),
             _dot(jnp.concatenate(diffs[2:4], axis=1).astype(_BF16), pool_w_ref[1])], axis=1)
        a_out = (mixed * pool_scale_ref[...]).astype(_BF16)

        u_ssm = proj_ref[:, POOL_WIDTH:u_width]
        u_bf = u_ssm.astype(_BF16)
        for half in range(2):
            xs_ref[half] = _dot(u_bf[:, half * HALF_ROWS:(half + 1) * HALF_ROWS], b_ref[half])
        for half in range(2):
            _scan_half(xs_ref.at[half], s_re_ref, s_im_ref, ab_re_ref, ab_im_ref, half, n_batch, sub_steps)
        ys = [_dot(xs_ref[half].astype(_BF16), c_ref[half]) for half in range(2)]
        y_ssm = jnp.concatenate(ys, axis=1) + d_ref[...] * u_ssm
        z = jax.nn.gelu(y_ssm)
        glu = jax.nn.sigmoid(_dot(z.astype(_BF16), glu_w_ref[...]) + glu_b_ref[...])
        b_out = (z * glu).astype(_BF16)

        merged = (jax.nn.sigmoid(proj_ref[:, u_width:u_width + D_MODEL]) * _dot(a_out, w_bp_ref[...])
                  + jax.nn.sigmoid(proj_ref[:, u_width + D_MODEL:u_width + 2 * D_MODEL]) * _dot(b_out, w_bs_ref[...]))
        h_ref[sub * sub_rows:(sub + 1) * sub_rows, :] = x_rows(sub) + _dot(merged.astype(_BF16), w_out_ref[...])

    for sub in range(n_sub):
        tile(sub)
    if hist_scratch:
        hist_ref[...] = ext_ref[rows:rows + hist_rows, :]

    for src, dst in zip(cast_in, cast_out):
        dst[...] = src[...].astype(_BF16)

    @pl.when(i == n_tiles - 1)
    def _():
        newest = hist_ref[n_batch:hist_rows, :] if hist_scratch else ext_ref[rows + n_batch:, :]
        hist_out_ref[...] = newest.reshape(POOL_HIST, n_batch, POOL_WIDTH)
        if state_batch_minor:
            s_re_out_ref[...] = s_re_ref[...].T
            s_im_out_ref[...] = s_im_ref[...].T


def _resident(shape):
    zeros = (0,) * len(shape)
    return pl.BlockSpec(shape, lambda i: zeros, pipeline_mode=pl.Buffered(1))


def _tiling(n_batch, n_steps_total, rows_per_step=ROW_TILE):
    n_steps = min(n_steps_total, rows_per_step // n_batch)
    n_tiles = n_steps_total // n_steps
    assert n_tiles * n_steps == n_steps_total and n_batch % SUBLANES == 0
    assert n_batch & (n_batch - 1) == 0
    return n_steps, n_tiles


def _mixer(x, state, weights, *, start_pos, state_batch_minor, cast=()):
    n_batch, n_steps_total, _ = x.shape
    n_sub = max(1, min(MIXER_SUB_TILES, n_steps_total * n_batch // ROW_TILE))
    n_steps, n_tiles = _tiling(n_batch, n_steps_total, ROW_TILE * n_sub)
    rows = n_steps * n_batch
    sub_rows = rows // n_sub
    hist_rows = HIST_STEPS * n_batch
    state = () if state is None else tuple(state)
    state_shape = (N_STATES, n_batch) if state_batch_minor else (n_batch, N_STATES)
    assert len(weights) == N_MIXER_WEIGHTS
    kern = functools.partial(_mixer_kernel, n_batch=n_batch, n_steps=n_steps, n_sub=n_sub, n_tiles=n_tiles,
                             start_pos=start_pos,
                             zero_state=not state, state_batch_minor=state_batch_minor, n_cast=len(cast))

    def cast_chunks(w):
        n = max(k for k in range(1, n_tiles + 1) if w.shape[0] % (k * BF16_ROWS) == 0)
        return pl.BlockSpec((w.shape[0] // n, w.shape[1]), lambda i: (jnp.minimum(i, n - 1), 0))
    state_block = pl.BlockSpec(state_shape, lambda i: (0, 0))
    hist_window_bytes = (2 if state else 1) * POOL_HIST * n_batch * POOL_WIDTH * 4
    vmem_limit = min(VMEM_CAP_BYTES, VMEM_LIMIT_BYTES + hist_window_bytes - hist_window_bytes % (1 << 20))
    return pl.pallas_call(
        kern,
        grid=(n_tiles,),
        in_specs=[pl.BlockSpec(memory_space=pl.ANY) if n_tiles > 1
                  else pl.BlockSpec((n_batch, n_steps, D_MODEL), lambda i: (0, i, 0))]
        + [_resident(a.shape) for a in state + tuple(weights)] + [cast_chunks(w) for w in cast],
        out_specs=[pl.BlockSpec((rows, D_MODEL), lambda i: (i, 0)),
                   pl.BlockSpec((POOL_HIST, n_batch, POOL_WIDTH), lambda i: (0, 0, 0)),
                   state_block, state_block] + [cast_chunks(w) for w in cast],
        out_shape=[jax.ShapeDtypeStruct((n_steps_total * n_batch, D_MODEL), _F32),
                   jax.ShapeDtypeStruct((POOL_HIST, n_batch, POOL_WIDTH), _F32),
                   jax.ShapeDtypeStruct(state_shape, _F32),
                   jax.ShapeDtypeStruct(state_shape, _F32)] + [jax.ShapeDtypeStruct(w.shape, _BF16) for w in cast],
        scratch_shapes=([pltpu.VMEM((hist_rows, POOL_WIDTH), _F32)] if n_tiles > 1 else [])
        + [pltpu.VMEM((2, n_steps, n_batch, D_MODEL) if n_tiles > 1 else (rows, D_MODEL), _F32),
           pltpu.VMEM((sub_rows, IN_WIDTH), _F32),
           pltpu.VMEM((hist_rows + rows, POOL_WIDTH), _F32),
           pltpu.VMEM((2, sub_rows, HALF_LANES), _F32)]
        + ([pltpu.VMEM((n_batch, N_STATES), _F32)] * 2 if state_batch_minor else [])
        + ([pltpu.SemaphoreType.DMA((2,))] if n_tiles > 1 else []),
        compiler_params=pltpu.CompilerParams(dimension_semantics=("arbitrary",), vmem_limit_bytes=vmem_limit),
        name="mixer",
    )(x, *state, *weights, *cast)


def _ffn_kernel(h_a_ref, h_b_ref, norm_ffn_ref, w_gate_ref, w_up_ref, w_down_ref, norm_final_ref,
                y_a_ref, y_b_ref, gate_ref, up_ref, y_buf, sem, *, n_batch_a, n_sub, n_steps_a):
    j = pl.program_id(0)
    sub_rows = h_a_ref.shape[0] // n_sub
    sub_steps = sub_rows // n_batch_a
    steps = n_sub * sub_steps

    def tile(h):
        hn = _rmsnorm(h, norm_ffn_ref[...]).astype(_BF16)
        gate_ref[...] = _dot(hn, w_gate_ref[...])
        up_ref[...] = _dot(hn, w_up_ref[...])
        gate = gate_ref[...]
        f = (gate * jax.nn.sigmoid(gate) * up_ref[...]).astype(_BF16)
        return _rmsnorm(h + _dot(f, w_down_ref[...]), norm_final_ref[...])

    def copies(slot, step):
        t0 = step * steps if isinstance(step, int) else pl.multiple_of(step * steps, steps)
        return [pltpu.make_async_copy(y_buf.at[slot, lo:hi, b, :], y_a_ref.at[b, pl.ds(t0 + lo, hi - lo), :],
                                      sem.at[slot])
                for b in range(n_batch_a) for lo, hi in _staggered(b, n_batch_a, steps)]

    @pl.when(j < n_steps_a)
    def _():
        slot = lax.rem(j, 2)
        for s in range(n_sub):
            y = tile(h_a_ref[s * sub_rows:(s + 1) * sub_rows, :])
            y_buf[slot, s * sub_steps:(s + 1) * sub_steps] = y.reshape(sub_steps, n_batch_a, D_MODEL)

        @pl.when(j > 0)
        def _():
            for c in copies(1 - slot, j - 1):
                c.wait()
        for c in copies(slot, j):
            c.start()

    @pl.when(j == n_steps_a)
    def _():
        n_batch_b, steps_b, _ = y_b_ref.shape
        y = tile(h_b_ref[...]).reshape(steps_b, n_batch_b, D_MODEL)
        y_b_ref[...] = jnp.swapaxes(y, 0, 1)
        for c in copies((n_steps_a - 1) % 2, n_steps_a - 1):
            c.wait()


def _ffn(h_a, h_b, norms, weights, *, n_batch_a, n_batch_b):
    norm_ffn, norm_final = norms
    w_gate, w_up, w_down = weights
    n_sub = FFN_SUB_TILES
    rows_a = ROW_TILE * n_sub
    n_steps_a = h_a.shape[0] // rows_a
    assert n_steps_a * rows_a == h_a.shape[0] and rows_a % n_batch_a == 0 and h_b.shape[0] == ROW_TILE
    t_a, t_b = h_a.shape[0] // n_batch_a, h_b.shape[0] // n_batch_b
    return pl.pallas_call(
        functools.partial(_ffn_kernel, n_batch_a=n_batch_a, n_sub=n_sub, n_steps_a=n_steps_a),
        grid=(n_steps_a + 1,),
        in_specs=[pl.BlockSpec((rows_a, D_MODEL), lambda i: (jnp.minimum(i, n_steps_a - 1), 0)),
                  _resident(h_b.shape), _resident(norm_ffn.shape),
                  _resident(w_gate.shape), _resident(w_up.shape), _resident(w_down.shape),
                  _resident(norm_final.shape)],
        out_specs=(pl.BlockSpec(memory_space=pl.ANY),
                   pl.BlockSpec((n_batch_b, t_b, D_MODEL), lambda i: (0, 0, 0))),
        out_shape=(jax.ShapeDtypeStruct((n_batch_a, t_a, D_MODEL), _F32),
                   jax.ShapeDtypeStruct((n_batch_b, t_b, D_MODEL), _F32)),
        scratch_shapes=[pltpu.VMEM((ROW_TILE, D_FF), _F32), pltpu.VMEM((ROW_TILE, D_FF), _F32),
                        pltpu.VMEM((2, rows_a // n_batch_a, n_batch_a, D_MODEL), _F32),
                        pltpu.SemaphoreType.DMA((2,))],
        compiler_params=pltpu.CompilerParams(dimension_semantics=("arbitrary",),
                                             vmem_limit_bytes=VMEM_LIMIT_BYTES),
        name="ffn",
    )(h_a, h_b, norm_ffn, w_gate, w_up, w_down, norm_final)


def kernel(x_prompt, x_sample, state_pool, state_ssm_re, state_ssm_im, norm_mix, w_in, pool_w, pool_scale, ssm_a_re, ssm_a_im, ssm_log_dt, ssm_b_re, ssm_b_im, ssm_c_re, ssm_c_im, ssm_d, glu_w, glu_b, w_branch_pool, w_branch_ssm, w_out, norm_ffn, ffn_w_gate, ffn_w_up, ffn_w_down, norm_final):
    assert w_in.shape[0] == 1, "single-layer trunk"
    (ab_re, ab_im, b_mat, c_mat, pool_mat, w_in_bf, glu_w_bf, w_bp_bf, w_bs_bf, w_out_bf) = _prep(
        ssm_a_re[0], ssm_a_im[0], ssm_log_dt, ssm_b_re[0], ssm_b_im[0], ssm_c_re[0], ssm_c_im[0], pool_w[0],
        (w_in[0], glu_w[0], w_branch_pool[0], w_branch_ssm[0], w_out[0]))
    mixer_w = (norm_mix[0].reshape(1, D_MODEL), w_in_bf, pool_mat,
               pool_scale[0].reshape(1, POOL_WIDTH), ab_re, ab_im, b_mat, c_mat, ssm_d[0].reshape(1, SSM_WIDTH),
               glu_w_bf, glu_b[0].reshape(1, SSM_WIDTH), w_bp_bf, w_bs_bf, w_out_bf)
    ffn_norms = (norm_ffn[0].reshape(1, D_MODEL), norm_final.reshape(1, D_MODEL))

    def mix(x, state, start_pos, cast=()):
        b = x.shape[0]
        batch_minor = b % LANES == 0
        if state is not None:
            flat = ((lambda a: a.transpose(1, 2, 0).reshape(N_STATES, b)) if batch_minor
                    else (lambda a: a.reshape(b, N_STATES)))
            state = (state[0].transpose(1, 0, 2), flat(state[1]), flat(state[2]))
        h_rows, new_hist, s_re, s_im, *cast_bf = _mixer(x, state, mixer_w, start_pos=start_pos,
                                                       state_batch_minor=batch_minor, cast=cast)
        unflat = ((lambda a: a.reshape(SSM_GROUPS, SSM_STATE, b).transpose(2, 0, 1)[None]) if batch_minor
                  else (lambda a: a.reshape(1, b, SSM_GROUPS, SSM_STATE)))
        return h_rows, new_hist.transpose(1, 0, 2)[None], unflat(s_re), unflat(s_im), cast_bf

    h_p, pool_p, re_p, im_p, ffn_w = mix(x_prompt, None, 0, cast=(ffn_w_gate[0], ffn_w_up[0], ffn_w_down[0]))
    h_s, pool_s, re_s, im_s, _ = mix(x_sample, (state_pool[0], state_ssm_re[0], state_ssm_im[0]), PAST_LEN)
    y_p, y_s = _ffn(h_p, h_s, ffn_norms, ffn_w,
                    n_batch_a=x_prompt.shape[0], n_batch_b=x_sample.shape[0])
    return (y_p, y_s, pool_p, re_p, im_p, pool_s, re_s, im_s)
```

```python
import functools
import math

import jax
import jax.numpy as jnp
from jax import lax
from jax.experimental import pallas as pl
from jax.experimental.pallas import tpu as pltpu

D_MODEL = 1024
PAST_LEN = 16384
POOL_WIDTH = 512
POOL_WINDOWS = (2, 4, 8, 16)
POOL_GROUP_DIM = 128
POOL_HIST = 15
HIST_STEPS = POOL_HIST + 1
SSM_WIDTH = 512
SSM_GROUPS = 32
SSM_GROUP_DIM = 16
SSM_STATE = 64
N_STATES = SSM_GROUPS * SSM_STATE
IN_WIDTH = POOL_WIDTH + SSM_WIDTH + 2 * D_MODEL
D_FF = 2816
RMS_EPS = 1e-6

SUBLANES = 8
BF16_ROWS = 16
LANES = 128
MXU_DIM = 256
CHUNK_LANES = 2 * LANES
N_CHUNKS = N_STATES // LANES
HALF_CHUNKS = N_CHUNKS // 2
HALF_LANES = HALF_CHUNKS * CHUNK_LANES
HALF_ROWS = MXU_DIM
ROW_TILE = 512
PREP_STEPS = 4
N_STRUCTURED_IN = 8
N_STRUCTURED_OUT = 5
N_MIXER_WEIGHTS = 14
MIXER_SUB_TILES = 2
FFN_SUB_TILES = 2
VMEM_LIMIT_BYTES = 56 * 1024 * 1024
VMEM_CAP_BYTES = 60 * 1024 * 1024

_F32 = jnp.float32
_BF16 = jnp.bfloat16


def _rmsnorm(x, g):
    return x * lax.rsqrt(jnp.mean(x * x, axis=-1, keepdims=True) + RMS_EPS) * g


def _dot(a, b):
    return jnp.dot(a, b, preferred_element_type=_F32)


def _structured_weights(a_re_ref, a_im_ref, log_dt_ref, b_re_ref, b_im_ref, c_re_ref, c_im_ref, pool_w_ref,
                        ab_re_ref, ab_im_ref, b_mat_ref, c_mat_ref, pool_mat_ref):
    g_n, h_n, p_n = SSM_GROUPS, SSM_GROUP_DIM, SSM_STATE
    twice = lambda v: jnp.concatenate([v, v], axis=-1)

    diag = lax.broadcasted_iota(jnp.int32, (g_n, g_n), 0) == lax.broadcasted_iota(jnp.int32, (g_n, g_n), 1)
    log_dt = jnp.sum(jnp.where(diag, jnp.broadcast_to(log_dt_ref[...], (g_n, g_n)), 0.0), axis=1, keepdims=True)
    dt = jnp.exp(log_dt)
    a_re = a_re_ref[...]
    a_im = a_im_ref[...]
    mag = jnp.exp(a_re * dt)
    ab_re = mag * jnp.cos(a_im * dt)
    ab_im = mag * jnp.sin(a_im * dt)
    num_re = ab_re - 1.0
    num_im = ab_im
    den = a_re * a_re + a_im * a_im
    coef_re = ((num_re * a_re + num_im * a_im) / den)[:, None, :]
    coef_im = ((num_im * a_re - num_re * a_im) / den)[:, None, :]
    ab_re_ref[...] = twice(ab_re)
    ab_im_ref[...] = twice(ab_im)

    b_re = b_re_ref[...]
    b_im = b_im_ref[...]
    bp = [twice(v).reshape(g_n * h_n, LANES)
          for v in (coef_re * b_re - coef_im * b_im, coef_re * b_im + coef_im * b_re)]
    row = lax.broadcasted_iota(jnp.int32, (HALF_ROWS, LANES), 0)
    lane = lax.broadcasted_iota(jnp.int32, (HALF_ROWS, LANES), 1)
    own_lanes = (lane >> 6) == ((row >> 4) & 1)
    for half in range(2):
        for part in range(2):
            mine = jnp.where(own_lanes, bp[part][half * HALF_ROWS:(half + 1) * HALF_ROWS, :], 0.0)
            for cl in range(HALF_CHUNKS):
                blk = jnp.where((row >> 5) == cl, mine, 0.0)
                lo = cl * CHUNK_LANES + part * LANES
                b_mat_ref[half, :, lo:lo + LANES] = blk.astype(_BF16)

    row_c = lax.broadcasted_iota(jnp.int32, (LANES, HALF_ROWS), 0)
    col_c = lax.broadcasted_iota(jnp.int32, (LANES, HALF_ROWS), 1)
    own_cols = (row_c >> 6) == ((col_c >> 4) & 1)
    for part, c_ref in enumerate((c_re_ref, c_im_ref)):
        ct = c_ref[...].reshape(g_n * h_n, p_n).T
        ct = jnp.concatenate([ct, ct], axis=0)
        if part == 1:
            ct = -ct
        for half in range(2):
            mine = jnp.where(own_cols, ct[:, half * HALF_ROWS:(half + 1) * HALF_ROWS], 0.0)
            for cl in range(HALF_CHUNKS):
                blk = jnp.where((col_c >> 5) == cl, mine, 0.0)
                lo = cl * CHUNK_LANES + part * LANES
                c_mat_ref[half, lo:lo + LANES, :] = blk.astype(_BF16)

    pool_mat_ref[...] = jnp.zeros(pool_mat_ref.shape, _BF16)
    for g in range(len(POOL_WINDOWS)):
        lo = (g % 2) * POOL_GROUP_DIM
        pool_mat_ref[g // 2, lo:lo + POOL_GROUP_DIM, lo:lo + POOL_GROUP_DIM] = pool_w_ref[g].astype(_BF16)


def _prep_kernel(*refs):
    n_dense = (len(refs) - N_STRUCTURED_IN - N_STRUCTURED_OUT) // 2
    dense_in = refs[N_STRUCTURED_IN:N_STRUCTURED_IN + n_dense]
    structured_out = refs[N_STRUCTURED_IN + n_dense:N_STRUCTURED_IN + n_dense + N_STRUCTURED_OUT]
    dense_out = refs[N_STRUCTURED_IN + n_dense + N_STRUCTURED_OUT:]

    @pl.when(pl.program_id(0) == 0)
    def _():
        _structured_weights(*refs[:N_STRUCTURED_IN], *structured_out)

    for src, dst in zip(dense_in, dense_out):
        dst[...] = src[...].astype(_BF16)


def _prep(a_re, a_im, log_dt, b_re, b_im, c_re, c_im, pool_w, dense):
    structured = (a_re, a_im, log_dt, b_re.transpose(0, 2, 1), b_im.transpose(0, 2, 1), c_re, c_im, pool_w)
    structured_out = (jax.ShapeDtypeStruct((SSM_GROUPS, LANES), _F32), jax.ShapeDtypeStruct((SSM_GROUPS, LANES), _F32),
                      jax.ShapeDtypeStruct((2, HALF_ROWS, HALF_LANES), _BF16),
                      jax.ShapeDtypeStruct((2, HALF_LANES, HALF_ROWS), _BF16),
                      jax.ShapeDtypeStruct((len(POOL_WINDOWS) // 2, MXU_DIM, MXU_DIM), _BF16))
    assert len(structured) == N_STRUCTURED_IN and len(structured_out) == N_STRUCTURED_OUT
    assert all(w.shape[0] % (PREP_STEPS * BF16_ROWS) == 0 for w in dense)
    whole = lambda a: pl.BlockSpec(a.shape, lambda i, n=len(a.shape): (0,) * n)
    chunk = lambda w: pl.BlockSpec((w.shape[0] // PREP_STEPS, w.shape[1]), lambda i: (i, 0))
    return pl.pallas_call(
        _prep_kernel,
        grid=(PREP_STEPS,),
        in_specs=[whole(a) for a in structured] + [chunk(w) for w in dense],
        out_specs=[whole(a) for a in structured_out] + [chunk(w) for w in dense],
        out_shape=structured_out + tuple(jax.ShapeDtypeStruct(w.shape, _BF16) for w in dense),
        compiler_params=pltpu.CompilerParams(dimension_semantics=("arbitrary",)),
        name="prep",
    )(*structured, *dense)


def _scan_half(xs_ref, s_re_ref, s_im_ref, ab_re_ref, ab_im_ref, half, n_batch, n_steps):
    lane = lax.broadcasted_iota(jnp.int32, (1, LANES), 1)

    def coeff(ref, c):
        row = jnp.where(lane < SSM_STATE, ref[2 * c:2 * c + 1, :], ref[2 * c + 1:2 * c + 2, :])
        return jnp.broadcast_to(row, (SUBLANES, LANES))

    def recur(r0, row_of_step, cls):
        chunks = [half * HALF_CHUNKS + cl for cl in cls]
        ar = [coeff(ab_re_ref, c) for c in chunks]
        ai = [coeff(ab_im_ref, c) for c in chunks]
        sr = [s_re_ref[pl.ds(r0, SUBLANES), c * LANES:(c + 1) * LANES] for c in chunks]
        si = [s_im_ref[pl.ds(r0, SUBLANES), c * LANES:(c + 1) * LANES] for c in chunks]
        for t in range(n_steps):
            rows = pl.ds(row_of_step(t), SUBLANES)
            for k, cl in enumerate(cls):
                lo = cl * CHUNK_LANES
                nr = ar[k] * sr[k] - ai[k] * si[k] + xs_ref[rows, lo:lo + LANES]
                ni = ar[k] * si[k] + ai[k] * sr[k] + xs_ref[rows, lo + LANES:lo + CHUNK_LANES]
                xs_ref[rows, lo:lo + LANES] = nr
                xs_ref[rows, lo + LANES:lo + CHUNK_LANES] = ni
                sr[k], si[k] = nr, ni
        for k, c in enumerate(chunks):
            s_re_ref[pl.ds(r0, SUBLANES), c * LANES:(c + 1) * LANES] = sr[k]
            s_im_ref[pl.ds(r0, SUBLANES), c * LANES:(c + 1) * LANES] = si[k]

    for r0 in range(0, n_batch, SUBLANES):
        recur(r0, lambda t, r0=r0: t * n_batch + r0, list(range(HALF_CHUNKS)))


_MIXER_WEIGHT_NAMES = ("norm", "w_in", "pool_w", "pool_scale", "ab_re", "ab_im", "b", "c", "d", "glu_w", "glu_b",
                       "w_bp", "w_bs", "w_out")
_LATE_OPERANDS = ("b", "s0_re", "s0_im", "c", "glu_w", "w_bp", "w_bs", "w_out")


def _mixer_kernel(*refs, n_batch, n_steps, n_sub, n_tiles, start_pos, zero_state, state_batch_minor, n_cast):
    n_state_in = 0 if zero_state else 3
    x_ref = refs[0]
    hist0_ref, s0_re_ref, s0_im_ref = refs[1:1 + n_state_in] if n_state_in else (None, None, None)
    (norm_ref, w_in_ref, pool_w_ref, pool_scale_ref, ab_re_ref, ab_im_ref, b_ref, c_ref, d_ref, glu_w_ref,
     glu_b_ref, w_bp_ref, w_bs_ref, w_out_ref) = refs[1 + n_state_in:1 + n_state_in + N_MIXER_WEIGHTS]
    rest = refs[1 + n_state_in + N_MIXER_WEIGHTS:]
    cast_in, rest = rest[:n_cast], rest[n_cast:]
    (h_ref, hist_out_ref, s_re_out_ref, s_im_out_ref), rest = rest[:4], rest[4:]
    cast_out, scratch = rest[:n_cast], rest[n_cast:]
    i = pl.program_id(0)
    rows = n_batch * n_steps
    sub_steps = n_steps // n_sub
    sub_rows = n_batch * sub_steps
    hist_rows = HIST_STEPS * n_batch

    scratch = list(scratch)
    hist_scratch = [scratch.pop(0)] if n_tiles > 1 else []
    x_stage_ref, proj_ref, ext_ref, xs_ref = scratch[:4]
    s_re_ref, s_im_ref = scratch[4:6] if state_batch_minor else (s_re_out_ref, s_im_out_ref)
    dma_in = n_tiles > 1
    x_sem = scratch[-1] if dma_in else None
    hist_ref = hist_scratch[0] if hist_scratch else ext_ref.at[0:hist_rows]

    named = dict(zip(("hist0", "s0_re", "s0_im")[:n_state_in], refs[1:1 + n_state_in]))
    named.update(zip(_MIXER_WEIGHT_NAMES, refs[1 + n_state_in:1 + n_state_in + N_MIXER_WEIGHTS]))
    late = {}
    if not dma_in:
        late_names = [name for name in _LATE_OPERANDS if name in named]
        late_bufs, late_sem = scratch[-1 - len(late_names):-1], scratch[-1]
        for k, (name, buf) in enumerate(zip(late_names, late_bufs)):
            late[name] = [pltpu.make_async_copy(named[name], buf, late_sem.at[k]), buf]
            late[name][0].start()

    def operand(name):
        if name not in late:
            return named[name]
        copy, buf = late[name]
        if copy is not None:
            copy.wait()
            late[name][0] = None
        return buf

    @pl.when(i == 0)
    def _():
        if zero_state:
            hist_ref[...] = jnp.zeros((hist_rows, POOL_WIDTH), _F32)
            s_re_ref[...] = jnp.zeros(s_re_ref.shape, _F32)
            s_im_ref[...] = jnp.zeros(s_im_ref.shape, _F32)
        else:
            hist_ref[0:n_batch, :] = jnp.zeros((n_batch, POOL_WIDTH), _F32)
            hist_ref[n_batch:hist_rows, :] = hist0_ref[...].reshape(POOL_HIST * n_batch, POOL_WIDTH)
            if "s0_re" not in late:
                s_re_ref[...] = s0_re_ref[...].T if state_batch_minor else s0_re_ref[...]
                s_im_ref[...] = s0_im_ref[...].T if state_batch_minor else s0_im_ref[...]

    if dma_in:
        slot = lax.rem(i, 2)

        def copies(slot_, step):
            t0 = pl.multiple_of(step * n_steps, n_steps)
            return [pltpu.make_async_copy(x_ref.at[b, pl.ds(t0, n_steps), :], x_stage_ref.at[slot_, :, b, :],
                                          x_sem.at[slot_]) for b in range(n_batch)]

        @pl.when(i == 0)
        def _():
            for c in copies(0, 0):
                c.start()
        for c in copies(slot, i):
            c.wait()

        @pl.when(i + 1 < n_tiles)
        def _():
            for c in copies(1 - slot, i + 1):
                c.start()
        x_rows = lambda sub: x_stage_ref[slot, sub * sub_steps:(sub + 1) * sub_steps].reshape(sub_rows, D_MODEL)
    else:
        x_stage_ref[...] = jnp.swapaxes(x_ref[...], 0, 1).reshape(rows, D_MODEL)
        x_rows = lambda sub: x_stage_ref[sub * sub_rows:(sub + 1) * sub_rows, :]
    u_width = POOL_WIDTH + SSM_WIDTH
    if hist_scratch:
        ext_ref[0:hist_rows, :] = hist_ref[...]

    def tile(sub):
        base = hist_rows + sub * sub_rows
        xn = _rmsnorm(x_rows(sub), norm_ref[...]).astype(_BF16)
        proj_ref[...] = _dot(xn, w_in_ref[...])

        ext_ref[base:base + sub_rows, :] = proj_ref[:, 0:POOL_WIDTH]
        diffs = []
        for g, w in enumerate(POOL_WINDOWS):
            lo = g * POOL_GROUP_DIM
            u_g = ext_ref[base:base + sub_rows, lo:lo + POOL_GROUP_DIM]
            acc = u_g
            for j in range(1, w):
                r0 = base - j * n_batch
                acc = acc + ext_ref[r0:r0 + sub_rows, lo:lo + POOL_GROUP_DIM]
            if start_pos + sub * sub_steps + 1 >= w:
                pooled = acc * (1.0 / w)
            else:
                head = min(sub_rows, hist_rows)
                row = lax.broadcasted_iota(jnp.int32, (head, POOL_GROUP_DIM), 0)
                step = lax.shift_right_logical(row, int(math.log2(n_batch)))
                pos = start_pos + i * n_steps + sub * sub_steps + step
                pooled = acc[:head] / jnp.minimum(w, pos + 1).astype(_F32)
                if head < sub_rows:
                    pooled = jnp.concatenate([pooled, acc[head:] * (1.0 / w)], axis=0)
            diffs.append(pooled - u_g)
        mixed = jnp.concatenate(
            [_dot(jnp.concatenate(diffs[0:2], axis=1).astype(_BF16), pool_w_ref[0]),
             _dot(jnp.concatenate(diffs[2:4], axis=1).astype(_BF16), pool_w_ref[1])], axis=1)
        a_out = (mixed * pool_scale_ref[...]).astype(_BF16)

        u_ssm = proj_ref[:, POOL_WIDTH:u_width]
        u_bf = u_ssm.astype(_BF16)
        for name in late:
            operand(name)
        for half in range(2):
            xs_ref[half] = _dot(u_bf[:, half * HALF_ROWS:(half + 1) * HALF_ROWS], operand("b")[half])
        if sub == 0 and "s0_re" in late:
            for s_ref, name in ((s_re_ref, "s0_re"), (s_im_ref, "s0_im")):
                s0 = operand(name)[...]
                s_ref[...] = s0.T if state_batch_minor else s0
        for half in range(2):
            _scan_half(xs_ref.at[half], s_re_ref, s_im_ref, ab_re_ref, ab_im_ref, half, n_batch, sub_steps)
        ys = [_dot(xs_ref[half].astype(_BF16), operand("c")[half]) for half in range(2)]
        y_ssm = jnp.concatenate(ys, axis=1) + d_ref[...] * u_ssm
        z = jax.nn.gelu(y_ssm)
        glu = jax.nn.sigmoid(_dot(z.astype(_BF16), operand("glu_w")[...]) + glu_b_ref[...])
        b_out = (z * glu).astype(_BF16)

        merged = (jax.nn.sigmoid(proj_ref[:, u_width:u_width + D_MODEL]) * _dot(a_out, operand("w_bp")[...])
                  + jax.nn.sigmoid(proj_ref[:, u_width + D_MODEL:u_width + 2 * D_MODEL])
                  * _dot(b_out, operand("w_bs")[...]))
        h_ref[sub * sub_rows:(sub + 1) * sub_rows, :] = (x_rows(sub)
                                                         + _dot(merged.astype(_BF16), operand("w_out")[...]))

    for sub in range(n_sub):
        tile(sub)
    if hist_scratch:
        hist_ref[...] = ext_ref[rows:rows + hist_rows, :]

    for src, dst in zip(cast_in, cast_out):
        dst[...] = src[...].astype(_BF16)

    @pl.when(i == n_tiles - 1)
    def _():
        newest = hist_ref[n_batch:hist_rows, :] if hist_scratch else ext_ref[rows + n_batch:, :]
        hist_out_ref[...] = newest.reshape(POOL_HIST, n_batch, POOL_WIDTH)
        if state_batch_minor:
            s_re_out_ref[...] = s_re_ref[...].T
            s_im_out_ref[...] = s_im_ref[...].T


def _resident(shape):
    zeros = (0,) * len(shape)
    return pl.BlockSpec(shape, lambda i: zeros, pipeline_mode=pl.Buffered(1))


def _tiling(n_batch, n_steps_total, rows_per_step=ROW_TILE):
    n_steps = min(n_steps_total, rows_per_step // n_batch)
    n_tiles = n_steps_total // n_steps
    assert n_tiles * n_steps == n_steps_total and n_batch % SUBLANES == 0
    assert n_batch & (n_batch - 1) == 0
    return n_steps, n_tiles


def _mixer(x, state, weights, *, start_pos, state_batch_minor, cast=()):
    n_batch, n_steps_total, _ = x.shape
    n_sub = max(1, min(MIXER_SUB_TILES, n_steps_total * n_batch // ROW_TILE))
    n_steps, n_tiles = _tiling(n_batch, n_steps_total, ROW_TILE * n_sub)
    rows = n_steps * n_batch
    sub_rows = rows // n_sub
    hist_rows = HIST_STEPS * n_batch
    state = () if state is None else tuple(state)
    state_shape = (N_STATES, n_batch) if state_batch_minor else (n_batch, N_STATES)
    assert len(weights) == N_MIXER_WEIGHTS
    kern = functools.partial(_mixer_kernel, n_batch=n_batch, n_steps=n_steps, n_sub=n_sub, n_tiles=n_tiles,
                             start_pos=start_pos,
                             zero_state=not state, state_batch_minor=state_batch_minor, n_cast=len(cast))

    named = dict(zip(("hist0", "s0_re", "s0_im"), state))
    named.update(zip(_MIXER_WEIGHT_NAMES, weights))
    late_names = [name for name in _LATE_OPERANDS if name in named] if n_tiles == 1 else []

    def cast_chunks(w):
        n = max(k for k in range(1, n_tiles + 1) if w.shape[0] % (k * BF16_ROWS) == 0)
        return pl.BlockSpec((w.shape[0] // n, w.shape[1]), lambda i: (jnp.minimum(i, n - 1), 0))
    state_block = pl.BlockSpec(state_shape, lambda i: (0, 0))
    hist_window_bytes = (2 if state else 1) * POOL_HIST * n_batch * POOL_WIDTH * 4
    vmem_limit = min(VMEM_CAP_BYTES, VMEM_LIMIT_BYTES + hist_window_bytes - hist_window_bytes % (1 << 20))
    return pl.pallas_call(
        kern,
        grid=(n_tiles,),
        in_specs=[pl.BlockSpec(memory_space=pl.ANY) if n_tiles > 1
                  else pl.BlockSpec((n_batch, n_steps, D_MODEL), lambda i: (0, i, 0))]
        + [pl.BlockSpec(memory_space=pl.ANY) if name in late_names else _resident(a.shape)
           for name, a in named.items()] + [cast_chunks(w) for w in cast],
        out_specs=[pl.BlockSpec((rows, D_MODEL), lambda i: (i, 0)),
                   pl.BlockSpec((POOL_HIST, n_batch, POOL_WIDTH), lambda i: (0, 0, 0)),
                   state_block, state_block] + [cast_chunks(w) for w in cast],
        out_shape=[jax.ShapeDtypeStruct((n_steps_total * n_batch, D_MODEL), _F32),
                   jax.ShapeDtypeStruct((POOL_HIST, n_batch, POOL_WIDTH), _F32),
                   jax.ShapeDtypeStruct(state_shape, _F32),
                   jax.ShapeDtypeStruct(state_shape, _F32)] + [jax.ShapeDtypeStruct(w.shape, _BF16) for w in cast],
        scratch_shapes=([pltpu.VMEM((hist_rows, POOL_WIDTH), _F32)] if n_tiles > 1 else [])
        + [pltpu.VMEM((2, n_steps, n_batch, D_MODEL) if n_tiles > 1 else (rows, D_MODEL), _F32),
           pltpu.VMEM((sub_rows, IN_WIDTH), _F32),
           pltpu.VMEM((hist_rows + rows, POOL_WIDTH), _F32),
           pltpu.VMEM((2, sub_rows, HALF_LANES), _F32)]
        + ([pltpu.VMEM((n_batch, N_STATES), _F32)] * 2 if state_batch_minor else [])
        + [pltpu.VMEM(named[name].shape, named[name].dtype) for name in late_names]
        + [pltpu.SemaphoreType.DMA((2 if n_tiles > 1 else len(late_names),))],
        compiler_params=pltpu.CompilerParams(dimension_semantics=("arbitrary",), vmem_limit_bytes=vmem_limit),
        name="mixer",
    )(x, *state, *weights, *cast)


def _ffn_kernel(h_a_ref, h_b_ref, norm_ffn_ref, w_gate_ref, w_up_ref, w_down_ref, norm_final_ref,
                y_a_ref, y_b_ref, gate_ref, up_ref, y_buf, sem, *, n_batch_a, n_sub, n_steps_a):
    j = pl.program_id(0)
    sub_rows = h_a_ref.shape[0] // n_sub
    sub_steps = sub_rows // n_batch_a
    steps = n_sub * sub_steps

    def tile(h):
        hn = _rmsnorm(h, norm_ffn_ref[...]).astype(_BF16)
        gate_ref[...] = _dot(hn, w_gate_ref[...])
        up_ref[...] = _dot(hn, w_up_ref[...])
        gate = gate_ref[...]
        f = (gate * jax.nn.sigmoid(gate) * up_ref[...]).astype(_BF16)
        return _rmsnorm(h + _dot(f, w_down_ref[...]), norm_final_ref[...])

    def copies(slot, step):
        t0 = step * steps if isinstance(step, int) else pl.multiple_of(step * steps, steps)
        return [pltpu.make_async_copy(y_buf.at[slot, :, b, :], y_a_ref.at[b, pl.ds(t0, steps), :], sem.at[slot])
                for b in range(n_batch_a)]

    @pl.when(j < n_steps_a)
    def _():
        slot = lax.rem(j, 2)
        for s in range(n_sub):
            y = tile(h_a_ref[s * sub_rows:(s + 1) * sub_rows, :])
            y_buf[slot, s * sub_steps:(s + 1) * sub_steps] = y.reshape(sub_steps, n_batch_a, D_MODEL)

        @pl.when(j > 0)
        def _():
            for c in copies(1 - slot, j - 1):
                c.wait()
        for c in copies(slot, j):
            c.start()

    @pl.when(j == n_steps_a)
    def _():
        n_batch_b, steps_b, _ = y_b_ref.shape
        y = tile(h_b_ref[...]).reshape(steps_b, n_batch_b, D_MODEL)
        y_b_ref[...] = jnp.swapaxes(y, 0, 1)
        for c in copies((n_steps_a - 1) % 2, n_steps_a - 1):
            c.wait()


def _ffn(h_a, h_b, norms, weights, *, n_batch_a, n_batch_b):
    norm_ffn, norm_final = norms
    w_gate, w_up, w_down = weights
    n_sub = FFN_SUB_TILES
    rows_a = ROW_TILE * n_sub
    n_steps_a = h_a.shape[0] // rows_a
    assert n_steps_a * rows_a == h_a.shape[0] and rows_a % n_batch_a == 0 and h_b.shape[0] == ROW_TILE
    t_a, t_b = h_a.shape[0] // n_batch_a, h_b.shape[0] // n_batch_b
    return pl.pallas_call(
        functools.partial(_ffn_kernel, n_batch_a=n_batch_a, n_sub=n_sub, n_steps_a=n_steps_a),
        grid=(n_steps_a + 1,),
        in_specs=[pl.BlockSpec((rows_a, D_MODEL), lambda i: (jnp.minimum(i, n_steps_a - 1), 0)),
                  _resident(h_b.shape), _resident(norm_ffn.shape),
                  _resident(w_gate.shape), _resident(w_up.shape), _resident(w_down.shape),
                  _resident(norm_final.shape)],
        out_specs=(pl.BlockSpec(memory_space=pl.ANY),
                   pl.BlockSpec((n_batch_b, t_b, D_MODEL), lambda i: (0, 0, 0))),
        out_shape=(jax.ShapeDtypeStruct((n_batch_a, t_a, D_MODEL), _F32),
                   jax.ShapeDtypeStruct((n_batch_b, t_b, D_MODEL), _F32)),
        scratch_shapes=[pltpu.VMEM((ROW_TILE, D_FF), _F32), pltpu.VMEM((ROW_TILE, D_FF), _F32),
                        pltpu.VMEM((2, rows_a // n_batch_a, n_batch_a, D_MODEL), _F32),
                        pltpu.SemaphoreType.DMA((2,))],
        compiler_params=pltpu.CompilerParams(dimension_semantics=("arbitrary",),
                                             vmem_limit_bytes=VMEM_LIMIT_BYTES),
        name="ffn",
    )(h_a, h_b, norm_ffn, w_gate, w_up, w_down, norm_final)


def kernel(x_prompt, x_sample, state_pool, state_ssm_re, state_ssm_im, norm_mix, w_in, pool_w, pool_scale, ssm_a_re, ssm_a_im, ssm_log_dt, ssm_b_re, ssm_b_im, ssm_c_re, ssm_c_im, ssm_d, glu_w, glu_b, w_branch_pool, w_branch_ssm, w_out, norm_ffn, ffn_w_gate, ffn_w_up, ffn_w_down, norm_final):
    assert w_in.shape[0] == 1, "single-layer trunk"
    (ab_re, ab_im, b_mat, c_mat, pool_mat, w_in_bf, glu_w_bf, w_bp_bf, w_bs_bf, w_out_bf) = _prep(
        ssm_a_re[0], ssm_a_im[0], ssm_log_dt, ssm_b_re[0], ssm_b_im[0], ssm_c_re[0], ssm_c_im[0], pool_w[0],
        (w_in[0], glu_w[0], w_branch_pool[0], w_branch_ssm[0], w_out[0]))
    mixer_w = (norm_mix[0].reshape(1, D_MODEL), w_in_bf, pool_mat,
               pool_scale[0].reshape(1, POOL_WIDTH), ab_re, ab_im, b_mat, c_mat, ssm_d[0].reshape(1, SSM_WIDTH),
               glu_w_bf, glu_b[0].reshape(1, SSM_WIDTH), w_bp_bf, w_bs_bf, w_out_bf)
    ffn_norms = (norm_ffn[0].reshape(1, D_MODEL), norm_final.reshape(1, D_MODEL))

    def mix(x, state, start_pos, cast=()):
        b = x.shape[0]
        batch_minor = b % LANES == 0
        if state is not None:
            flat = ((lambda a: a.transpose(1, 2, 0).reshape(N_STATES, b)) if batch_minor
                    else (lambda a: a.reshape(b, N_STATES)))
            state = (state[0].transpose(1, 0, 2), flat(state[1]), flat(state[2]))
        h_rows, new_hist, s_re, s_im, *cast_bf = _mixer(x, state, mixer_w, start_pos=start_pos,
                                                       state_batch_minor=batch_minor, cast=cast)
        unflat = ((lambda a: a.reshape(SSM_GROUPS, SSM_STATE, b).transpose(2, 0, 1)[None]) if batch_minor
                  else (lambda a: a.reshape(1, b, SSM_GROUPS, SSM_STATE)))
        return h_rows, new_hist.transpose(1, 0, 2)[None], unflat(s_re), unflat(s_im), cast_bf

    h_p, pool_p, re_p, im_p, ffn_w = mix(x_prompt, None, 0, cast=(ffn_w_gate[0], ffn_w_up[0], ffn_w_down[0]))
    h_s, pool_s, re_s, im_s, _ = mix(x_sample, (state_pool[0], state_ssm_re[0], state_ssm_im[0]), PAST_LEN)
    y_p, y_s = _ffn(h_p, h_s, ffn_norms, ffn_w,
                    n_batch_a=x_prompt.shape[0], n_batch_b=x_sample.shape[0])
    return (y_p, y_s, pool_p, re_p, im_p, pool_s, re_s, im_s)
```

```python
import functools
import math

import jax
import jax.numpy as jnp
from jax import lax
from jax.experimental import pallas as pl
from jax.experimental.pallas import tpu as pltpu

D_MODEL = 1024
PAST_LEN = 16384
POOL_WIDTH = 512
POOL_WINDOWS = (2, 4, 8, 16)
POOL_GROUP_DIM = 128
POOL_HIST = 15
HIST_STEPS = POOL_HIST + 1
SSM_WIDTH = 512
SSM_GROUPS = 32
SSM_GROUP_DIM = 16
SSM_STATE = 64
N_STATES = SSM_GROUPS * SSM_STATE
IN_WIDTH = POOL_WIDTH + SSM_WIDTH + 2 * D_MODEL
D_FF = 2816
RMS_EPS = 1e-6

SUBLANES = 8
BF16_ROWS = 16
LANES = 128
MXU_DIM = 256
CHUNK_LANES = 2 * LANES
N_CHUNKS = N_STATES // LANES
HALF_CHUNKS = N_CHUNKS // 2
HALF_LANES = HALF_CHUNKS * CHUNK_LANES
HALF_ROWS = MXU_DIM
ROW_TILE = 512
PREP_STEPS = 4
N_STRUCTURED_IN = 8
N_STRUCTURED_OUT = 5
N_MIXER_WEIGHTS = 14
MIXER_SUB_TILES = 2
FFN_SUB_TILES = 2
VMEM_LIMIT_BYTES = 56 * 1024 * 1024
VMEM_CAP_BYTES = 60 * 1024 * 1024

_F32 = jnp.float32
_BF16 = jnp.bfloat16


def _rmsnorm(x, g):
    return x * lax.rsqrt(jnp.mean(x * x, axis=-1, keepdims=True) + RMS_EPS) * g


def _dot(a, b):
    return jnp.dot(a, b, preferred_element_type=_F32)


def _structured_weights(a_re_ref, a_im_ref, log_dt_ref, b_re_ref, b_im_ref, c_re_ref, c_im_ref, pool_w_ref,
                        ab_re_ref, ab_im_ref, b_mat_ref, c_mat_ref, pool_mat_ref):
    g_n, h_n, p_n = SSM_GROUPS, SSM_GROUP_DIM, SSM_STATE
    twice = lambda v: jnp.concatenate([v, v], axis=-1)

    diag = lax.broadcasted_iota(jnp.int32, (g_n, g_n), 0) == lax.broadcasted_iota(jnp.int32, (g_n, g_n), 1)
    log_dt = jnp.sum(jnp.where(diag, jnp.broadcast_to(log_dt_ref[...], (g_n, g_n)), 0.0), axis=1, keepdims=True)
    dt = jnp.exp(log_dt)
    a_re = a_re_ref[...]
    a_im = a_im_ref[...]
    mag = jnp.exp(a_re * dt)
    ab_re = mag * jnp.cos(a_im * dt)
    ab_im = mag * jnp.sin(a_im * dt)
    num_re = ab_re - 1.0
    num_im = ab_im
    den = a_re * a_re + a_im * a_im
    coef_re = ((num_re * a_re + num_im * a_im) / den)[:, None, :]
    coef_im = ((num_im * a_re - num_re * a_im) / den)[:, None, :]
    ab_re_ref[...] = twice(ab_re)
    ab_im_ref[...] = twice(ab_im)

    b_re = b_re_ref[...]
    b_im = b_im_ref[...]
    bp = [twice(v).reshape(g_n * h_n, LANES)
          for v in (coef_re * b_re - coef_im * b_im, coef_re * b_im + coef_im * b_re)]
    row = lax.broadcasted_iota(jnp.int32, (HALF_ROWS, LANES), 0)
    lane = lax.broadcasted_iota(jnp.int32, (HALF_ROWS, LANES), 1)
    own_lanes = (lane >> 6) == ((row >> 4) & 1)
    for half in range(2):
        for part in range(2):
            mine = jnp.where(own_lanes, bp[part][half * HALF_ROWS:(half + 1) * HALF_ROWS, :], 0.0)
            for cl in range(HALF_CHUNKS):
                blk = jnp.where((row >> 5) == cl, mine, 0.0)
                lo = cl * CHUNK_LANES + part * LANES
                b_mat_ref[half, :, lo:lo + LANES] = blk.astype(_BF16)

    row_c = lax.broadcasted_iota(jnp.int32, (LANES, HALF_ROWS), 0)
    col_c = lax.broadcasted_iota(jnp.int32, (LANES, HALF_ROWS), 1)
    own_cols = (row_c >> 6) == ((col_c >> 4) & 1)
    for part, c_ref in enumerate((c_re_ref, c_im_ref)):
        ct = c_ref[...].reshape(g_n * h_n, p_n).T
        ct = jnp.concatenate([ct, ct], axis=0)
        if part == 1:
            ct = -ct
        for half in range(2):
            mine = jnp.where(own_cols, ct[:, half * HALF_ROWS:(half + 1) * HALF_ROWS], 0.0)
            for cl in range(HALF_CHUNKS):
                blk = jnp.where((col_c >> 5) == cl, mine, 0.0)
                lo = cl * CHUNK_LANES + part * LANES
                c_mat_ref[half, lo:lo + LANES, :] = blk.astype(_BF16)

    pool_mat_ref[...] = jnp.zeros(pool_mat_ref.shape, _BF16)
    for g in range(len(POOL_WINDOWS)):
        lo = (g % 2) * POOL_GROUP_DIM
        pool_mat_ref[g // 2, lo:lo + POOL_GROUP_DIM, lo:lo + POOL_GROUP_DIM] = pool_w_ref[g].astype(_BF16)


def _prep_kernel(*refs):
    n_dense = (len(refs) - N_STRUCTURED_IN - N_STRUCTURED_OUT) // 2
    dense_in = refs[N_STRUCTURED_IN:N_STRUCTURED_IN + n_dense]
    structured_out = refs[N_STRUCTURED_IN + n_dense:N_STRUCTURED_IN + n_dense + N_STRUCTURED_OUT]
    dense_out = refs[N_STRUCTURED_IN + n_dense + N_STRUCTURED_OUT:]

    @pl.when(pl.program_id(0) == 0)
    def _():
        _structured_weights(*refs[:N_STRUCTURED_IN], *structured_out)

    for src, dst in zip(dense_in, dense_out):
        dst[...] = src[...].astype(_BF16)


def _prep(a_re, a_im, log_dt, b_re, b_im, c_re, c_im, pool_w, dense):
    structured = (a_re, a_im, log_dt, b_re.transpose(0, 2, 1), b_im.transpose(0, 2, 1), c_re, c_im, pool_w)
    structured_out = (jax.ShapeDtypeStruct((SSM_GROUPS, LANES), _F32), jax.ShapeDtypeStruct((SSM_GROUPS, LANES), _F32),
                      jax.ShapeDtypeStruct((2, HALF_ROWS, HALF_LANES), _BF16),
                      jax.ShapeDtypeStruct((2, HALF_LANES, HALF_ROWS), _BF16),
                      jax.ShapeDtypeStruct((len(POOL_WINDOWS) // 2, MXU_DIM, MXU_DIM), _BF16))
    assert len(structured) == N_STRUCTURED_IN and len(structured_out) == N_STRUCTURED_OUT
    assert all(w.shape[0] % (PREP_STEPS * BF16_ROWS) == 0 for w in dense)
    whole = lambda a: pl.BlockSpec(a.shape, lambda i, n=len(a.shape): (0,) * n)
    chunk = lambda w: pl.BlockSpec((w.shape[0] // PREP_STEPS, w.shape[1]), lambda i: (i, 0))
    return pl.pallas_call(
        _prep_kernel,
        grid=(PREP_STEPS,),
        in_specs=[whole(a) for a in structured] + [chunk(w) for w in dense],
        out_specs=[whole(a) for a in structured_out] + [chunk(w) for w in dense],
        out_shape=structured_out + tuple(jax.ShapeDtypeStruct(w.shape, _BF16) for w in dense),
        compiler_params=pltpu.CompilerParams(dimension_semantics=("arbitrary",)),
        name="prep",
    )(*structured, *dense)


def _scan_half(xs_ref, s_re_ref, s_im_ref, ab_re_ref, ab_im_ref, half, n_batch, n_steps):
    lane = lax.broadcasted_iota(jnp.int32, (1, LANES), 1)

    def coeff(ref, c):
        row = jnp.where(lane < SSM_STATE, ref[2 * c:2 * c + 1, :], ref[2 * c + 1:2 * c + 2, :])
        return jnp.broadcast_to(row, (SUBLANES, LANES))

    def recur(r0, row_of_step, cls):
        chunks = [half * HALF_CHUNKS + cl for cl in cls]
        ar = [coeff(ab_re_ref, c) for c in chunks]
        ai = [coeff(ab_im_ref, c) for c in chunks]
        sr = [s_re_ref[pl.ds(r0, SUBLANES), c * LANES:(c + 1) * LANES] for c in chunks]
        si = [s_im_ref[pl.ds(r0, SUBLANES), c * LANES:(c + 1) * LANES] for c in chunks]
        for t in range(n_steps):
            rows = pl.ds(row_of_step(t), SUBLANES)
            for k, cl in enumerate(cls):
                lo = cl * CHUNK_LANES
                nr = ar[k] * sr[k] - ai[k] * si[k] + xs_ref[rows, lo:lo + LANES]
                ni = ar[k] * si[k] + ai[k] * sr[k] + xs_ref[rows, lo + LANES:lo + CHUNK_LANES]
                xs_ref[rows, lo:lo + LANES] = nr
                xs_ref[rows, lo + LANES:lo + CHUNK_LANES] = ni
                sr[k], si[k] = nr, ni
        for k, c in enumerate(chunks):
            s_re_ref[pl.ds(r0, SUBLANES), c * LANES:(c + 1) * LANES] = sr[k]
            s_im_ref[pl.ds(r0, SUBLANES), c * LANES:(c + 1) * LANES] = si[k]

    for r0 in range(0, n_batch, SUBLANES):
        recur(r0, lambda t, r0=r0: t * n_batch + r0, list(range(HALF_CHUNKS)))


_MIXER_WEIGHT_NAMES = ("norm", "w_in", "pool_w", "pool_scale", "ab_re", "ab_im", "b", "c", "d", "glu_w", "glu_b",
                       "w_bp", "w_bs", "w_out")
_LATE_OPERANDS = ("hist0", "b", "s0_re", "s0_im", "c", "glu_w", "w_bp", "w_bs", "w_out")


def _mixer_kernel(*refs, n_batch, n_steps, n_sub, n_tiles, start_pos, zero_state, state_batch_minor, n_cast):
    n_state_in = 0 if zero_state else 3
    x_ref = refs[0]
    hist0_ref, s0_re_ref, s0_im_ref = refs[1:1 + n_state_in] if n_state_in else (None, None, None)
    (norm_ref, w_in_ref, pool_w_ref, pool_scale_ref, ab_re_ref, ab_im_ref, b_ref, c_ref, d_ref, glu_w_ref,
     glu_b_ref, w_bp_ref, w_bs_ref, w_out_ref) = refs[1 + n_state_in:1 + n_state_in + N_MIXER_WEIGHTS]
    rest = refs[1 + n_state_in + N_MIXER_WEIGHTS:]
    cast_in, rest = rest[:n_cast], rest[n_cast:]
    (h_ref, hist_out_ref, s_re_out_ref, s_im_out_ref), rest = rest[:4], rest[4:]
    cast_out, scratch = rest[:n_cast], rest[n_cast:]
    i = pl.program_id(0)
    rows = n_batch * n_steps
    sub_steps = n_steps // n_sub
    sub_rows = n_batch * sub_steps
    hist_rows = HIST_STEPS * n_batch

    scratch = list(scratch)
    hist_scratch = [scratch.pop(0)] if n_tiles > 1 else []
    x_stage_ref, proj_ref, ext_ref, xs_ref = scratch[:4]
    s_re_ref, s_im_ref = scratch[4:6] if state_batch_minor else (s_re_out_ref, s_im_out_ref)
    dma_in = n_tiles > 1
    x_sem = scratch[-1] if dma_in else None
    hist_ref = hist_scratch[0] if hist_scratch else ext_ref.at[0:hist_rows]

    named = dict(zip(("hist0", "s0_re", "s0_im")[:n_state_in], refs[1:1 + n_state_in]))
    named.update(zip(_MIXER_WEIGHT_NAMES, refs[1 + n_state_in:1 + n_state_in + N_MIXER_WEIGHTS]))
    late = {}
    if not dma_in:
        late_names = [name for name in _LATE_OPERANDS if name in named]
        late_sem = scratch[-1]
        landing = iter(scratch[-1 - sum(name != "hist0" for name in late_names):-1])
        for k, name in enumerate(late_names):
            buf = ext_ref.at[n_batch:hist_rows] if name == "hist0" else next(landing)
            late[name] = [pltpu.make_async_copy(named[name], buf, late_sem.at[k]), buf]
            late[name][0].start()
        hist_out_copy = pltpu.make_async_copy(ext_ref.at[rows + n_batch:rows + hist_rows], hist_out_ref,
                                              late_sem.at[len(late_names)])

    def operand(name):
        if name not in late:
            return named[name]
        copy, buf = late[name]
        if copy is not None:
            copy.wait()
            late[name][0] = None
        return buf

    @pl.when(i == 0)
    def _():
        if zero_state:
            hist_ref[...] = jnp.zeros((hist_rows, POOL_WIDTH), _F32)
            s_re_ref[...] = jnp.zeros(s_re_ref.shape, _F32)
            s_im_ref[...] = jnp.zeros(s_im_ref.shape, _F32)
        else:
            hist_ref[0:n_batch, :] = jnp.zeros((n_batch, POOL_WIDTH), _F32)
            if "hist0" not in late:
                hist_ref[n_batch:hist_rows, :] = hist0_ref[...].reshape(POOL_HIST * n_batch, POOL_WIDTH)
            if "s0_re" not in late:
                s_re_ref[...] = s0_re_ref[...].T if state_batch_minor else s0_re_ref[...]
                s_im_ref[...] = s0_im_ref[...].T if state_batch_minor else s0_im_ref[...]

    if dma_in:
        slot = lax.rem(i, 2)

        def copies(slot_, step):
            t0 = pl.multiple_of(step * n_steps, n_steps)
            return [pltpu.make_async_copy(x_ref.at[b, pl.ds(t0, n_steps), :], x_stage_ref.at[slot_, :, b, :],
                                          x_sem.at[slot_]) for b in range(n_batch)]

        @pl.when(i == 0)
        def _():
            for c in copies(0, 0):
                c.start()
        for c in copies(slot, i):
            c.wait()

        @pl.when(i + 1 < n_tiles)
        def _():
            for c in copies(1 - slot, i + 1):
                c.start()
        x_rows = lambda sub: x_stage_ref[slot, sub * sub_steps:(sub + 1) * sub_steps].reshape(sub_rows, D_MODEL)
    else:
        x_stage_ref[...] = jnp.swapaxes(x_ref[...], 0, 1).reshape(rows, D_MODEL)
        x_rows = lambda sub: x_stage_ref[sub * sub_rows:(sub + 1) * sub_rows, :]
    u_width = POOL_WIDTH + SSM_WIDTH
    if hist_scratch:
        ext_ref[0:hist_rows, :] = hist_ref[...]

    def tile(sub):
        base = hist_rows + sub * sub_rows
        xn = _rmsnorm(x_rows(sub), norm_ref[...]).astype(_BF16)
        proj_ref[...] = _dot(xn, w_in_ref[...])

        ext_ref[base:base + sub_rows, :] = proj_ref[:, 0:POOL_WIDTH]
        if "hist0" in late:
            operand("hist0")
        if not dma_in and sub == n_sub - 1:
            hist_out_copy.start()
        diffs = []
        for g, w in enumerate(POOL_WINDOWS):
            lo = g * POOL_GROUP_DIM
            u_g = ext_ref[base:base + sub_rows, lo:lo + POOL_GROUP_DIM]
            acc = u_g
            for j in range(1, w):
                r0 = base - j * n_batch
                acc = acc + ext_ref[r0:r0 + sub_rows, lo:lo + POOL_GROUP_DIM]
            if start_pos + sub * sub_steps + 1 >= w:
                pooled = acc * (1.0 / w)
            else:
                head = min(sub_rows, hist_rows)
                row = lax.broadcasted_iota(jnp.int32, (head, POOL_GROUP_DIM), 0)
                step = lax.shift_right_logical(row, int(math.log2(n_batch)))
                pos = start_pos + i * n_steps + sub * sub_steps + step
                pooled = acc[:head] / jnp.minimum(w, pos + 1).astype(_F32)
                if head < sub_rows:
                    pooled = jnp.concatenate([pooled, acc[head:] * (1.0 / w)], axis=0)
            diffs.append(pooled - u_g)
        mixed = jnp.concatenate(
            [_dot(jnp.concatenate(diffs[0:2], axis=1).astype(_BF16), pool_w_ref[0]),
             _dot(jnp.concatenate(diffs[2:4], axis=1).astype(_BF16), pool_w_ref[1])], axis=1)
        a_out = (mixed * pool_scale_ref[...]).astype(_BF16)

        u_ssm = proj_ref[:, POOL_WIDTH:u_width]
        u_bf = u_ssm.astype(_BF16)
        for name in late:
            operand(name)
        for half in range(2):
            xs_ref[half] = _dot(u_bf[:, half * HALF_ROWS:(half + 1) * HALF_ROWS], operand("b")[half])
        if sub == 0 and "s0_re" in late:
            for s_ref, name in ((s_re_ref, "s0_re"), (s_im_ref, "s0_im")):
                s0 = operand(name)[...]
                s_ref[...] = s0.T if state_batch_minor else s0
        for half in range(2):
            _scan_half(xs_ref.at[half], s_re_ref, s_im_ref, ab_re_ref, ab_im_ref, half, n_batch, sub_steps)
        ys = [_dot(xs_ref[half].astype(_BF16), operand("c")[half]) for half in range(2)]
        y_ssm = jnp.concatenate(ys, axis=1) + d_ref[...] * u_ssm
        z = jax.nn.gelu(y_ssm)
        glu = jax.nn.sigmoid(_dot(z.astype(_BF16), operand("glu_w")[...]) + glu_b_ref[...])
        b_out = (z * glu).astype(_BF16)

        merged = (jax.nn.sigmoid(proj_ref[:, u_width:u_width + D_MODEL]) * _dot(a_out, operand("w_bp")[...])
                  + jax.nn.sigmoid(proj_ref[:, u_width + D_MODEL:u_width + 2 * D_MODEL])
                  * _dot(b_out, operand("w_bs")[...]))
        h_ref[sub * sub_rows:(sub + 1) * sub_rows, :] = (x_rows(sub)
                                                         + _dot(merged.astype(_BF16), operand("w_out")[...]))

    for sub in range(n_sub):
        tile(sub)
    if hist_scratch:
        hist_ref[...] = ext_ref[rows:rows + hist_rows, :]

    for src, dst in zip(cast_in, cast_out):
        dst[...] = src[...].astype(_BF16)

    @pl.when(i == n_tiles - 1)
    def _():
        if dma_in:
            hist_out_ref[...] = hist_ref[n_batch:hist_rows, :].reshape(POOL_HIST, n_batch, POOL_WIDTH)
        else:
            hist_out_copy.wait()
        if state_batch_minor:
            s_re_out_ref[...] = s_re_ref[...].T
            s_im_out_ref[...] = s_im_ref[...].T


def _resident(shape):
    zeros = (0,) * len(shape)
    return pl.BlockSpec(shape, lambda i: zeros, pipeline_mode=pl.Buffered(1))


def _tiling(n_batch, n_steps_total, rows_per_step=ROW_TILE):
    n_steps = min(n_steps_total, rows_per_step // n_batch)
    n_tiles = n_steps_total // n_steps
    assert n_tiles * n_steps == n_steps_total and n_batch % SUBLANES == 0
    assert n_batch & (n_batch - 1) == 0
    return n_steps, n_tiles


def _mixer(x, state, weights, *, start_pos, state_batch_minor, cast=()):
    n_batch, n_steps_total, _ = x.shape
    n_sub = max(1, min(MIXER_SUB_TILES, n_steps_total * n_batch // ROW_TILE))
    n_steps, n_tiles = _tiling(n_batch, n_steps_total, ROW_TILE * n_sub)
    rows = n_steps * n_batch
    sub_rows = rows // n_sub
    hist_rows = HIST_STEPS * n_batch
    state = () if state is None else tuple(state)
    hist_rows_shape = (POOL_HIST * n_batch, POOL_WIDTH)
    if state and n_tiles == 1:
        state = (state[0].reshape(hist_rows_shape),) + state[1:]
    state_shape = (N_STATES, n_batch) if state_batch_minor else (n_batch, N_STATES)
    assert len(weights) == N_MIXER_WEIGHTS
    kern = functools.partial(_mixer_kernel, n_batch=n_batch, n_steps=n_steps, n_sub=n_sub, n_tiles=n_tiles,
                             start_pos=start_pos,
                             zero_state=not state, state_batch_minor=state_batch_minor, n_cast=len(cast))

    named = dict(zip(("hist0", "s0_re", "s0_im"), state))
    named.update(zip(_MIXER_WEIGHT_NAMES, weights))
    late_names = [name for name in _LATE_OPERANDS if name in named] if n_tiles == 1 else []

    def cast_chunks(w):
        n = max(k for k in range(1, n_tiles + 1) if w.shape[0] % (k * BF16_ROWS) == 0)
        return pl.BlockSpec((w.shape[0] // n, w.shape[1]), lambda i: (jnp.minimum(i, n - 1), 0))
    state_block = pl.BlockSpec(state_shape, lambda i: (0, 0))
    hist_window_bytes = (2 if state else 1) * POOL_HIST * n_batch * POOL_WIDTH * 4
    vmem_limit = min(VMEM_CAP_BYTES, VMEM_LIMIT_BYTES + hist_window_bytes - hist_window_bytes % (1 << 20))
    outs = pl.pallas_call(
        kern,
        grid=(n_tiles,),
        in_specs=[pl.BlockSpec(memory_space=pl.ANY) if n_tiles > 1
                  else pl.BlockSpec((n_batch, n_steps, D_MODEL), lambda i: (0, i, 0))]
        + [pl.BlockSpec(memory_space=pl.ANY) if name in late_names else _resident(a.shape)
           for name, a in named.items()] + [cast_chunks(w) for w in cast],
        out_specs=[pl.BlockSpec((rows, D_MODEL), lambda i: (i, 0)),
                   pl.BlockSpec((POOL_HIST, n_batch, POOL_WIDTH), lambda i: (0, 0, 0)) if n_tiles > 1
                   else pl.BlockSpec(memory_space=pl.ANY),
                   state_block, state_block] + [cast_chunks(w) for w in cast],
        out_shape=[jax.ShapeDtypeStruct((n_steps_total * n_batch, D_MODEL), _F32),
                   jax.ShapeDtypeStruct((POOL_HIST, n_batch, POOL_WIDTH) if n_tiles > 1 else hist_rows_shape, _F32),
                   jax.ShapeDtypeStruct(state_shape, _F32),
                   jax.ShapeDtypeStruct(state_shape, _F32)] + [jax.ShapeDtypeStruct(w.shape, _BF16) for w in cast],
        scratch_shapes=([pltpu.VMEM((hist_rows, POOL_WIDTH), _F32)] if n_tiles > 1 else [])
        + [pltpu.VMEM((2, n_steps, n_batch, D_MODEL) if n_tiles > 1 else (rows, D_MODEL), _F32),
           pltpu.VMEM((sub_rows, IN_WIDTH), _F32),
           pltpu.VMEM((hist_rows + rows, POOL_WIDTH), _F32),
           pltpu.VMEM((2, sub_rows, HALF_LANES), _F32)]
        + ([pltpu.VMEM((n_batch, N_STATES), _F32)] * 2 if state_batch_minor else [])
        + [pltpu.VMEM(named[name].shape, named[name].dtype) for name in late_names if name != "hist0"]
        + [pltpu.SemaphoreType.DMA((2 if n_tiles > 1 else len(late_names) + 1,))],
        compiler_params=pltpu.CompilerParams(dimension_semantics=("arbitrary",), vmem_limit_bytes=vmem_limit),
        name="mixer",
    )(x, *state, *weights, *cast)
    return (outs[0], outs[1].reshape(POOL_HIST, n_batch, POOL_WIDTH), *outs[2:])


def _ffn_kernel(h_a_ref, h_b_ref, norm_ffn_ref, w_gate_ref, w_up_ref, w_down_ref, norm_final_ref,
                y_a_ref, y_b_ref, gate_ref, up_ref, y_buf, h_b_buf, sem, *, n_batch_a, n_sub, n_steps_a):
    j = pl.program_id(0)
    h_b_copy = pltpu.make_async_copy(h_b_ref, h_b_buf, sem.at[2])

    @pl.when(j == 0)
    def _():
        h_b_copy.start()
    sub_rows = h_a_ref.shape[0] // n_sub
    sub_steps = sub_rows // n_batch_a
    steps = n_sub * sub_steps

    def tile(h):
        hn = _rmsnorm(h, norm_ffn_ref[...]).astype(_BF16)
        gate_ref[...] = _dot(hn, w_gate_ref[...])
        up_ref[...] = _dot(hn, w_up_ref[...])
        gate = gate_ref[...]
        f = (gate * jax.nn.sigmoid(gate) * up_ref[...]).astype(_BF16)
        return _rmsnorm(h + _dot(f, w_down_ref[...]), norm_final_ref[...])

    def copies(slot, step):
        t0 = step * steps if isinstance(step, int) else pl.multiple_of(step * steps, steps)
        return [pltpu.make_async_copy(y_buf.at[slot, :, b, :], y_a_ref.at[b, pl.ds(t0, steps), :], sem.at[slot])
                for b in range(n_batch_a)]

    @pl.when(j < n_steps_a)
    def _():
        slot = lax.rem(j, 2)
        for s in range(n_sub):
            y = tile(h_a_ref[s * sub_rows:(s + 1) * sub_rows, :])
            y_buf[slot, s * sub_steps:(s + 1) * sub_steps] = y.reshape(sub_steps, n_batch_a, D_MODEL)

        @pl.when(j > 0)
        def _():
            for c in copies(1 - slot, j - 1):
                c.wait()
        for c in copies(slot, j):
            c.start()

    @pl.when(j == n_steps_a)
    def _():
        n_batch_b, steps_b, _ = y_b_ref.shape
        h_b_copy.wait()
        y = tile(h_b_buf[...]).reshape(steps_b, n_batch_b, D_MODEL)
        y_b_ref[...] = jnp.swapaxes(y, 0, 1)
        for c in copies((n_steps_a - 1) % 2, n_steps_a - 1):
            c.wait()


def _ffn(h_a, h_b, norms, weights, *, n_batch_a, n_batch_b):
    norm_ffn, norm_final = norms
    w_gate, w_up, w_down = weights
    n_sub = FFN_SUB_TILES
    rows_a = ROW_TILE * n_sub
    n_steps_a = h_a.shape[0] // rows_a
    assert n_steps_a * rows_a == h_a.shape[0] and rows_a % n_batch_a == 0 and h_b.shape[0] == ROW_TILE
    t_a, t_b = h_a.shape[0] // n_batch_a, h_b.shape[0] // n_batch_b
    return pl.pallas_call(
        functools.partial(_ffn_kernel, n_batch_a=n_batch_a, n_sub=n_sub, n_steps_a=n_steps_a),
        grid=(n_steps_a + 1,),
        in_specs=[pl.BlockSpec((rows_a, D_MODEL), lambda i: (jnp.minimum(i, n_steps_a - 1), 0)),
                  pl.BlockSpec(memory_space=pl.ANY), _resident(norm_ffn.shape),
                  _resident(w_gate.shape), _resident(w_up.shape), _resident(w_down.shape),
                  _resident(norm_final.shape)],
        out_specs=(pl.BlockSpec(memory_space=pl.ANY),
                   pl.BlockSpec((n_batch_b, t_b, D_MODEL), lambda i: (0, 0, 0))),
        out_shape=(jax.ShapeDtypeStruct((n_batch_a, t_a, D_MODEL), _F32),
                   jax.ShapeDtypeStruct((n_batch_b, t_b, D_MODEL), _F32)),
        scratch_shapes=[pltpu.VMEM((ROW_TILE, D_FF), _F32), pltpu.VMEM((ROW_TILE, D_FF), _F32),
                        pltpu.VMEM((2, rows_a // n_batch_a, n_batch_a, D_MODEL), _F32),
                        pltpu.VMEM(h_b.shape, _F32),
                        pltpu.SemaphoreType.DMA((3,))],
        compiler_params=pltpu.CompilerParams(dimension_semantics=("arbitrary",),
                                             vmem_limit_bytes=VMEM_LIMIT_BYTES),
        name="ffn",
    )(h_a, h_b, norm_ffn, w_gate, w_up, w_down, norm_final)


def kernel(x_prompt, x_sample, state_pool, state_ssm_re, state_ssm_im, norm_mix, w_in, pool_w, pool_scale, ssm_a_re, ssm_a_im, ssm_log_dt, ssm_b_re, ssm_b_im, ssm_c_re, ssm_c_im, ssm_d, glu_w, glu_b, w_branch_pool, w_branch_ssm, w_out, norm_ffn, ffn_w_gate, ffn_w_up, ffn_w_down, norm_final):
    assert w_in.shape[0] == 1, "single-layer trunk"
    (ab_re, ab_im, b_mat, c_mat, pool_mat, w_in_bf, glu_w_bf, w_bp_bf, w_bs_bf, w_out_bf) = _prep(
        ssm_a_re[0], ssm_a_im[0], ssm_log_dt, ssm_b_re[0], ssm_b_im[0], ssm_c_re[0], ssm_c_im[0], pool_w[0],
        (w_in[0], glu_w[0], w_branch_pool[0], w_branch_ssm[0], w_out[0]))
    mixer_w = (norm_mix[0].reshape(1, D_MODEL), w_in_bf, pool_mat,
               pool_scale[0].reshape(1, POOL_WIDTH), ab_re, ab_im, b_mat, c_mat, ssm_d[0].reshape(1, SSM_WIDTH),
               glu_w_bf, glu_b[0].reshape(1, SSM_WIDTH), w_bp_bf, w_bs_bf, w_out_bf)
    ffn_norms = (norm_ffn[0].reshape(1, D_MODEL), norm_final.reshape(1, D_MODEL))

    def mix(x, state, start_pos, cast=()):
        b = x.shape[0]
        batch_minor = b % LANES == 0
        if state is not None:
            flat = ((lambda a: a.transpose(1, 2, 0).reshape(N_STATES, b)) if batch_minor
                    else (lambda a: a.reshape(b, N_STATES)))
            state = (state[0].transpose(1, 0, 2), flat(state[1]), flat(state[2]))
        h_rows, new_hist, s_re, s_im, *cast_bf = _mixer(x, state, mixer_w, start_pos=start_pos,
                                                       state_batch_minor=batch_minor, cast=cast)
        unflat = ((lambda a: a.reshape(SSM_GROUPS, SSM_STATE, b).transpose(2, 0, 1)[None]) if batch_minor
                  else (lambda a: a.reshape(1, b, SSM_GROUPS, SSM_STATE)))
        return h_rows, new_hist.transpose(1, 0, 2)[None], unflat(s_re), unflat(s_im), cast_bf

    h_p, pool_p, re_p, im_p, ffn_w = mix(x_prompt, None, 0, cast=(ffn_w_gate[0], ffn_w_up[0], ffn_w_down[0]))
    h_s, pool_s, re_s, im_s, _ = mix(x_sample, (state_pool[0], state_ssm_re[0], state_ssm_im[0]), PAST_LEN)
    y_p, y_s = _ffn(h_p, h_s, ffn_norms, ffn_w,
                    n_batch_a=x_prompt.shape[0], n_batch_b=x_sample.shape[0])
    return (y_p, y_s, pool_p, re_p, im_p, pool_s, re_s, im_s)
```

```python
import functools
import math

import jax
import jax.numpy as jnp
from jax import lax
from jax.experimental import pallas as pl
from jax.experimental.pallas import tpu as pltpu

D_MODEL = 1024
PAST_LEN = 16384
POOL_WIDTH = 512
POOL_WINDOWS = (2, 4, 8, 16)
POOL_GROUP_DIM = 128
POOL_HIST = 15
HIST_STEPS = POOL_HIST + 1
SSM_WIDTH = 512
SSM_GROUPS = 32
SSM_GROUP_DIM = 16
SSM_STATE = 64
N_STATES = SSM_GROUPS * SSM_STATE
IN_WIDTH = POOL_WIDTH + SSM_WIDTH + 2 * D_MODEL
D_FF = 2816
RMS_EPS = 1e-6

SUBLANES = 8
BF16_ROWS = 16
LANES = 128
MXU_DIM = 256
CHUNK_LANES = 2 * LANES
N_CHUNKS = N_STATES // LANES
HALF_CHUNKS = N_CHUNKS // 2
HALF_LANES = HALF_CHUNKS * CHUNK_LANES
HALF_ROWS = MXU_DIM
ROW_TILE = 512
PREP_STEPS = 4
N_STRUCTURED_IN = 8
N_STRUCTURED_OUT = 5
N_MIXER_WEIGHTS = 14
MIXER_SUB_TILES = 2
FFN_SUB_TILES = 2
VMEM_LIMIT_BYTES = 56 * 1024 * 1024
VMEM_CAP_BYTES = 60 * 1024 * 1024

_F32 = jnp.float32
_BF16 = jnp.bfloat16


def _rmsnorm(x, g):
    return x * lax.rsqrt(jnp.mean(x * x, axis=-1, keepdims=True) + RMS_EPS) * g


def _dot(a, b):
    return jnp.dot(a, b, preferred_element_type=_F32)


def _structured_weights(a_re_ref, a_im_ref, log_dt_ref, b_re_ref, b_im_ref, c_re_ref, c_im_ref, pool_w_ref,
                        ab_re_ref, ab_im_ref, b_mat_ref, c_mat_ref, pool_mat_ref):
    g_n, h_n, p_n = SSM_GROUPS, SSM_GROUP_DIM, SSM_STATE
    twice = lambda v: jnp.concatenate([v, v], axis=-1)

    diag = lax.broadcasted_iota(jnp.int32, (g_n, g_n), 0) == lax.broadcasted_iota(jnp.int32, (g_n, g_n), 1)
    log_dt = jnp.sum(jnp.where(diag, jnp.broadcast_to(log_dt_ref[...], (g_n, g_n)), 0.0), axis=1, keepdims=True)
    dt = jnp.exp(log_dt)
    a_re = a_re_ref[...]
    a_im = a_im_ref[...]
    mag = jnp.exp(a_re * dt)
    ab_re = mag * jnp.cos(a_im * dt)
    ab_im = mag * jnp.sin(a_im * dt)
    num_re = ab_re - 1.0
    num_im = ab_im
    den = a_re * a_re + a_im * a_im
    coef_re = ((num_re * a_re + num_im * a_im) / den)[:, None, :]
    coef_im = ((num_im * a_re - num_re * a_im) / den)[:, None, :]
    ab_re_ref[...] = twice(ab_re)
    ab_im_ref[...] = twice(ab_im)

    b_re = b_re_ref[...]
    b_im = b_im_ref[...]
    bp = [twice(v).reshape(g_n * h_n, LANES)
          for v in (coef_re * b_re - coef_im * b_im, coef_re * b_im + coef_im * b_re)]
    row = lax.broadcasted_iota(jnp.int32, (HALF_ROWS, LANES), 0)
    lane = lax.broadcasted_iota(jnp.int32, (HALF_ROWS, LANES), 1)
    own_lanes = (lane >> 6) == ((row >> 4) & 1)
    for half in range(2):
        for part in range(2):
            mine = jnp.where(own_lanes, bp[part][half * HALF_ROWS:(half + 1) * HALF_ROWS, :], 0.0)
            for cl in range(HALF_CHUNKS):
                blk = jnp.where((row >> 5) == cl, mine, 0.0)
                lo = cl * CHUNK_LANES + part * LANES
                b_mat_ref[half, :, lo:lo + LANES] = blk.astype(_BF16)

    row_c = lax.broadcasted_iota(jnp.int32, (LANES, HALF_ROWS), 0)
    col_c = lax.broadcasted_iota(jnp.int32, (LANES, HALF_ROWS), 1)
    own_cols = (row_c >> 6) == ((col_c >> 4) & 1)
    for part, c_ref in enumerate((c_re_ref, c_im_ref)):
        ct = c_ref[...].reshape(g_n * h_n, p_n).T
        ct = jnp.concatenate([ct, ct], axis=0)
        if part == 1:
            ct = -ct
        for half in range(2):
            mine = jnp.where(own_cols, ct[:, half * HALF_ROWS:(half + 1) * HALF_ROWS], 0.0)
            for cl in range(HALF_CHUNKS):
                blk = jnp.where((col_c >> 5) == cl, mine, 0.0)
                lo = cl * CHUNK_LANES + part * LANES
                c_mat_ref[half, lo:lo + LANES, :] = blk.astype(_BF16)

    pool_mat_ref[...] = jnp.zeros(pool_mat_ref.shape, _BF16)
    for g in range(len(POOL_WINDOWS)):
        lo = (g % 2) * POOL_GROUP_DIM
        pool_mat_ref[g // 2, lo:lo + POOL_GROUP_DIM, lo:lo + POOL_GROUP_DIM] = pool_w_ref[g].astype(_BF16)


def _prep_kernel(*refs):
    n_dense = (len(refs) - N_STRUCTURED_IN - N_STRUCTURED_OUT) // 2
    dense_in = refs[N_STRUCTURED_IN:N_STRUCTURED_IN + n_dense]
    structured_out = refs[N_STRUCTURED_IN + n_dense:N_STRUCTURED_IN + n_dense + N_STRUCTURED_OUT]
    dense_out = refs[N_STRUCTURED_IN + n_dense + N_STRUCTURED_OUT:]

    @pl.when(pl.program_id(0) == 0)
    def _():
        _structured_weights(*refs[:N_STRUCTURED_IN], *structured_out)

    for src, dst in zip(dense_in, dense_out):
        dst[...] = src[...].astype(_BF16)


def _prep(a_re, a_im, log_dt, b_re, b_im, c_re, c_im, pool_w, dense):
    structured = (a_re, a_im, log_dt, b_re.transpose(0, 2, 1), b_im.transpose(0, 2, 1), c_re, c_im, pool_w)
    structured_out = (jax.ShapeDtypeStruct((SSM_GROUPS, LANES), _F32), jax.ShapeDtypeStruct((SSM_GROUPS, LANES), _F32),
                      jax.ShapeDtypeStruct((2, HALF_ROWS, HALF_LANES), _BF16),
                      jax.ShapeDtypeStruct((2, HALF_LANES, HALF_ROWS), _BF16),
                      jax.ShapeDtypeStruct((len(POOL_WINDOWS) // 2, MXU_DIM, MXU_DIM), _BF16))
    assert len(structured) == N_STRUCTURED_IN and len(structured_out) == N_STRUCTURED_OUT
    assert all(w.shape[0] % (PREP_STEPS * BF16_ROWS) == 0 for w in dense)
    whole = lambda a: pl.BlockSpec(a.shape, lambda i, n=len(a.shape): (0,) * n)
    chunk = lambda w: pl.BlockSpec((w.shape[0] // PREP_STEPS, w.shape[1]), lambda i: (i, 0))
    return pl.pallas_call(
        _prep_kernel,
        grid=(PREP_STEPS,),
        in_specs=[whole(a) for a in structured] + [chunk(w) for w in dense],
        out_specs=[whole(a) for a in structured_out] + [chunk(w) for w in dense],
        out_shape=structured_out + tuple(jax.ShapeDtypeStruct(w.shape, _BF16) for w in dense),
        compiler_params=pltpu.CompilerParams(dimension_semantics=("arbitrary",)),
        name="prep",
    )(*structured, *dense)


def _scan_half(xs_ref, s_re_ref, s_im_ref, ab_re_ref, ab_im_ref, half, n_batch, n_steps):
    lane = lax.broadcasted_iota(jnp.int32, (1, LANES), 1)

    def coeff(ref, c):
        row = jnp.where(lane < SSM_STATE, ref[2 * c:2 * c + 1, :], ref[2 * c + 1:2 * c + 2, :])
        return jnp.broadcast_to(row, (SUBLANES, LANES))

    def recur(r0, row_of_step, cls):
        chunks = [half * HALF_CHUNKS + cl for cl in cls]
        ar = [coeff(ab_re_ref, c) for c in chunks]
        ai = [coeff(ab_im_ref, c) for c in chunks]
        sr = [s_re_ref[pl.ds(r0, SUBLANES), c * LANES:(c + 1) * LANES] for c in chunks]
        si = [s_im_ref[pl.ds(r0, SUBLANES), c * LANES:(c + 1) * LANES] for c in chunks]
        for t in range(n_steps):
            rows = pl.ds(row_of_step(t), SUBLANES)
            for k, cl in enumerate(cls):
                lo = cl * CHUNK_LANES
                nr = ar[k] * sr[k] - ai[k] * si[k] + xs_ref[rows, lo:lo + LANES]
                ni = ar[k] * si[k] + ai[k] * sr[k] + xs_ref[rows, lo + LANES:lo + CHUNK_LANES]
                xs_ref[rows, lo:lo + LANES] = nr
                xs_ref[rows, lo + LANES:lo + CHUNK_LANES] = ni
                sr[k], si[k] = nr, ni
        for k, c in enumerate(chunks):
            s_re_ref[pl.ds(r0, SUBLANES), c * LANES:(c + 1) * LANES] = sr[k]
            s_im_ref[pl.ds(r0, SUBLANES), c * LANES:(c + 1) * LANES] = si[k]

    for r0 in range(0, n_batch, SUBLANES):
        recur(r0, lambda t, r0=r0: t * n_batch + r0, list(range(HALF_CHUNKS)))


_MIXER_WEIGHT_NAMES = ("norm", "w_in", "pool_w", "pool_scale", "ab_re", "ab_im", "b", "c", "d", "glu_w", "glu_b",
                       "w_bp", "w_bs", "w_out")
_LATE_OPERANDS = ("hist0", "b", "s0_re", "s0_im", "c", "glu_w", "w_bp", "w_bs", "w_out")


def _mixer_kernel(*refs, n_batch, n_steps, n_sub, n_tiles, start_pos, zero_state, state_batch_minor, n_cast):
    n_state_in = 0 if zero_state else 3
    x_ref = refs[0]
    hist0_ref, s0_re_ref, s0_im_ref = refs[1:1 + n_state_in] if n_state_in else (None, None, None)
    (norm_ref, w_in_ref, pool_w_ref, pool_scale_ref, ab_re_ref, ab_im_ref, b_ref, c_ref, d_ref, glu_w_ref,
     glu_b_ref, w_bp_ref, w_bs_ref, w_out_ref) = refs[1 + n_state_in:1 + n_state_in + N_MIXER_WEIGHTS]
    rest = refs[1 + n_state_in + N_MIXER_WEIGHTS:]
    cast_in, rest = rest[:n_cast], rest[n_cast:]
    (h_ref, hist_out_ref, s_re_out_ref, s_im_out_ref), rest = rest[:4], rest[4:]
    cast_out, scratch = rest[:n_cast], rest[n_cast:]
    i = pl.program_id(0)
    rows = n_batch * n_steps
    sub_steps = n_steps // n_sub
    sub_rows = n_batch * sub_steps
    hist_rows = HIST_STEPS * n_batch

    scratch = list(scratch)
    hist_scratch = [scratch.pop(0)] if n_tiles > 1 else []
    x_stage_ref, proj_ref, ext_ref, xs_ref = scratch[:4]
    s_re_ref, s_im_ref = scratch[4:6] if state_batch_minor else (s_re_out_ref, s_im_out_ref)
    dma_in = n_tiles > 1
    x_sem = scratch[-1] if dma_in else None
    hist_ref = hist_scratch[0] if hist_scratch else ext_ref.at[0:hist_rows]

    named = dict(zip(("hist0", "s0_re", "s0_im")[:n_state_in], refs[1:1 + n_state_in]))
    named.update(zip(_MIXER_WEIGHT_NAMES, refs[1 + n_state_in:1 + n_state_in + N_MIXER_WEIGHTS]))
    late = {}
    state_out = []
    if not dma_in:
        late_names = [name for name in _LATE_OPERANDS if name in named]
        late_sem = scratch[-1]
        landing = iter(scratch[-1 - sum(name != "hist0" for name in late_names):-1])
        for k, name in enumerate(late_names):
            buf = ext_ref.at[n_batch:hist_rows] if name == "hist0" else next(landing)
            late[name] = [pltpu.make_async_copy(named[name], buf, late_sem.at[k]), buf]
            late[name][0].start()
        hist_out_copy = pltpu.make_async_copy(ext_ref.at[rows + n_batch:rows + hist_rows], hist_out_ref,
                                              late_sem.at[len(late_names)])
        if state_batch_minor:
            state_out = [(s_ref, stage, pltpu.make_async_copy(stage, out_ref, late_sem.at[len(late_names) + 1 + k]))
                         for k, (s_ref, stage, out_ref) in enumerate(
                             zip((s_re_ref, s_im_ref), scratch[6:8], (s_re_out_ref, s_im_out_ref)))]

    def operand(name):
        if name not in late:
            return named[name]
        copy, buf = late[name]
        if copy is not None:
            copy.wait()
            late[name][0] = None
        return buf

    @pl.when(i == 0)
    def _():
        if zero_state:
            hist_ref[...] = jnp.zeros((hist_rows, POOL_WIDTH), _F32)
            s_re_ref[...] = jnp.zeros(s_re_ref.shape, _F32)
            s_im_ref[...] = jnp.zeros(s_im_ref.shape, _F32)
        else:
            hist_ref[0:n_batch, :] = jnp.zeros((n_batch, POOL_WIDTH), _F32)
            if "hist0" not in late:
                hist_ref[n_batch:hist_rows, :] = hist0_ref[...].reshape(POOL_HIST * n_batch, POOL_WIDTH)
            if "s0_re" not in late:
                s_re_ref[...] = s0_re_ref[...].T if state_batch_minor else s0_re_ref[...]
                s_im_ref[...] = s0_im_ref[...].T if state_batch_minor else s0_im_ref[...]

    if dma_in:
        slot = lax.rem(i, 2)

        def copies(slot_, step):
            t0 = pl.multiple_of(step * n_steps, n_steps)
            return [pltpu.make_async_copy(x_ref.at[b, pl.ds(t0, n_steps), :], x_stage_ref.at[slot_, :, b, :],
                                          x_sem.at[slot_]) for b in range(n_batch)]

        @pl.when(i == 0)
        def _():
            for c in copies(0, 0):
                c.start()
        for c in copies(slot, i):
            c.wait()

        @pl.when(i + 1 < n_tiles)
        def _():
            for c in copies(1 - slot, i + 1):
                c.start()
        x_rows = lambda sub: x_stage_ref[slot, sub * sub_steps:(sub + 1) * sub_steps].reshape(sub_rows, D_MODEL)
    else:
        x_stage_ref[...] = jnp.swapaxes(x_ref[...], 0, 1).reshape(rows, D_MODEL)
        x_rows = lambda sub: x_stage_ref[sub * sub_rows:(sub + 1) * sub_rows, :]
    u_width = POOL_WIDTH + SSM_WIDTH
    if hist_scratch:
        ext_ref[0:hist_rows, :] = hist_ref[...]

    def tile(sub):
        base = hist_rows + sub * sub_rows
        xn = _rmsnorm(x_rows(sub), norm_ref[...]).astype(_BF16)
        proj_ref[...] = _dot(xn, w_in_ref[...])

        ext_ref[base:base + sub_rows, :] = proj_ref[:, 0:POOL_WIDTH]
        if "hist0" in late:
            operand("hist0")
        if not dma_in and sub == n_sub - 1:
            hist_out_copy.start()
        diffs = []
        for g, w in enumerate(POOL_WINDOWS):
            lo = g * POOL_GROUP_DIM
            u_g = ext_ref[base:base + sub_rows, lo:lo + POOL_GROUP_DIM]
            acc = u_g
            for j in range(1, w):
                r0 = base - j * n_batch
                acc = acc + ext_ref[r0:r0 + sub_rows, lo:lo + POOL_GROUP_DIM]
            if start_pos + sub * sub_steps + 1 >= w:
                pooled = acc * (1.0 / w)
            else:
                head = min(sub_rows, hist_rows)
                row = lax.broadcasted_iota(jnp.int32, (head, POOL_GROUP_DIM), 0)
                step = lax.shift_right_logical(row, int(math.log2(n_batch)))
                pos = start_pos + i * n_steps + sub * sub_steps + step
                pooled = acc[:head] / jnp.minimum(w, pos + 1).astype(_F32)
                if head < sub_rows:
                    pooled = jnp.concatenate([pooled, acc[head:] * (1.0 / w)], axis=0)
            diffs.append(pooled - u_g)
        mixed = jnp.concatenate(
            [_dot(jnp.concatenate(diffs[0:2], axis=1).astype(_BF16), pool_w_ref[0]),
             _dot(jnp.concatenate(diffs[2:4], axis=1).astype(_BF16), pool_w_ref[1])], axis=1)
        a_out = (mixed * pool_scale_ref[...]).astype(_BF16)

        u_ssm = proj_ref[:, POOL_WIDTH:u_width]
        u_bf = u_ssm.astype(_BF16)
        for name in late:
            operand(name)
        for half in range(2):
            xs_ref[half] = _dot(u_bf[:, half * HALF_ROWS:(half + 1) * HALF_ROWS], operand("b")[half])
        if sub == 0 and "s0_re" in late:
            for s_ref, name in ((s_re_ref, "s0_re"), (s_im_ref, "s0_im")):
                s0 = operand(name)[...]
                s_ref[...] = s0.T if state_batch_minor else s0
        for half in range(2):
            _scan_half(xs_ref.at[half], s_re_ref, s_im_ref, ab_re_ref, ab_im_ref, half, n_batch, sub_steps)
        if sub == n_sub - 1:
            for s_ref, stage, copy in state_out:
                stage[...] = s_ref[...].T
                copy.start()
        ys = [_dot(xs_ref[half].astype(_BF16), operand("c")[half]) for half in range(2)]
        y_ssm = jnp.concatenate(ys, axis=1) + d_ref[...] * u_ssm
        z = jax.nn.gelu(y_ssm)
        glu = jax.nn.sigmoid(_dot(z.astype(_BF16), operand("glu_w")[...]) + glu_b_ref[...])
        b_out = (z * glu).astype(_BF16)

        merged = (jax.nn.sigmoid(proj_ref[:, u_width:u_width + D_MODEL]) * _dot(a_out, operand("w_bp")[...])
                  + jax.nn.sigmoid(proj_ref[:, u_width + D_MODEL:u_width + 2 * D_MODEL])
                  * _dot(b_out, operand("w_bs")[...]))
        h_ref[sub * sub_rows:(sub + 1) * sub_rows, :] = (x_rows(sub)
                                                         + _dot(merged.astype(_BF16), operand("w_out")[...]))

    for sub in range(n_sub):
        tile(sub)
    if hist_scratch:
        hist_ref[...] = ext_ref[rows:rows + hist_rows, :]

    for src, dst in zip(cast_in, cast_out):
        dst[...] = src[...].astype(_BF16)

    @pl.when(i == n_tiles - 1)
    def _():
        if dma_in:
            hist_out_ref[...] = hist_ref[n_batch:hist_rows, :].reshape(POOL_HIST, n_batch, POOL_WIDTH)
        else:
            hist_out_copy.wait()
        for *_, copy in state_out:
            copy.wait()
        if state_batch_minor and not state_out:
            s_re_out_ref[...] = s_re_ref[...].T
            s_im_out_ref[...] = s_im_ref[...].T


def _resident(shape):
    zeros = (0,) * len(shape)
    return pl.BlockSpec(shape, lambda i: zeros, pipeline_mode=pl.Buffered(1))


def _tiling(n_batch, n_steps_total, rows_per_step=ROW_TILE):
    n_steps = min(n_steps_total, rows_per_step // n_batch)
    n_tiles = n_steps_total // n_steps
    assert n_tiles * n_steps == n_steps_total and n_batch % SUBLANES == 0
    assert n_batch & (n_batch - 1) == 0
    return n_steps, n_tiles


def _mixer(x, state, weights, *, start_pos, state_batch_minor, cast=()):
    n_batch, n_steps_total, _ = x.shape
    n_sub = max(1, min(MIXER_SUB_TILES, n_steps_total * n_batch // ROW_TILE))
    n_steps, n_tiles = _tiling(n_batch, n_steps_total, ROW_TILE * n_sub)
    rows = n_steps * n_batch
    sub_rows = rows // n_sub
    hist_rows = HIST_STEPS * n_batch
    state = () if state is None else tuple(state)
    hist_rows_shape = (POOL_HIST * n_batch, POOL_WIDTH)
    if state and n_tiles == 1:
        state = (state[0].reshape(hist_rows_shape),) + state[1:]
    state_shape = (N_STATES, n_batch) if state_batch_minor else (n_batch, N_STATES)
    assert len(weights) == N_MIXER_WEIGHTS
    kern = functools.partial(_mixer_kernel, n_batch=n_batch, n_steps=n_steps, n_sub=n_sub, n_tiles=n_tiles,
                             start_pos=start_pos,
                             zero_state=not state, state_batch_minor=state_batch_minor, n_cast=len(cast))

    named = dict(zip(("hist0", "s0_re", "s0_im"), state))
    named.update(zip(_MIXER_WEIGHT_NAMES, weights))
    late_names = [name for name in _LATE_OPERANDS if name in named] if n_tiles == 1 else []

    def cast_chunks(w):
        n = max(k for k in range(1, n_tiles + 1) if w.shape[0] % (k * BF16_ROWS) == 0)
        return pl.BlockSpec((w.shape[0] // n, w.shape[1]), lambda i: (jnp.minimum(i, n - 1), 0))
    early_state = n_tiles == 1 and state_batch_minor
    state_block = pl.BlockSpec(memory_space=pl.ANY) if early_state else pl.BlockSpec(state_shape, lambda i: (0, 0))
    hist_window_bytes = (2 if state else 1) * POOL_HIST * n_batch * POOL_WIDTH * 4
    vmem_limit = min(VMEM_CAP_BYTES, VMEM_LIMIT_BYTES + hist_window_bytes - hist_window_bytes % (1 << 20))
    outs = pl.pallas_call(
        kern,
        grid=(n_tiles,),
        in_specs=[pl.BlockSpec(memory_space=pl.ANY) if n_tiles > 1
                  else pl.BlockSpec((n_batch, n_steps, D_MODEL), lambda i: (0, i, 0))]
        + [pl.BlockSpec(memory_space=pl.ANY) if name in late_names else _resident(a.shape)
           for name, a in named.items()] + [cast_chunks(w) for w in cast],
        out_specs=[pl.BlockSpec((rows, D_MODEL), lambda i: (i, 0)),
                   pl.BlockSpec((POOL_HIST, n_batch, POOL_WIDTH), lambda i: (0, 0, 0)) if n_tiles > 1
                   else pl.BlockSpec(memory_space=pl.ANY),
                   state_block, state_block] + [cast_chunks(w) for w in cast],
        out_shape=[jax.ShapeDtypeStruct((n_steps_total * n_batch, D_MODEL), _F32),
                   jax.ShapeDtypeStruct((POOL_HIST, n_batch, POOL_WIDTH) if n_tiles > 1 else hist_rows_shape, _F32),
                   jax.ShapeDtypeStruct(state_shape, _F32),
                   jax.ShapeDtypeStruct(state_shape, _F32)] + [jax.ShapeDtypeStruct(w.shape, _BF16) for w in cast],
        scratch_shapes=([pltpu.VMEM((hist_rows, POOL_WIDTH), _F32)] if n_tiles > 1 else [])
        + [pltpu.VMEM((2, n_steps, n_batch, D_MODEL) if n_tiles > 1 else (rows, D_MODEL), _F32),
           pltpu.VMEM((sub_rows, IN_WIDTH), _F32),
           pltpu.VMEM((hist_rows + rows, POOL_WIDTH), _F32),
           pltpu.VMEM((2, sub_rows, HALF_LANES), _F32)]
        + ([pltpu.VMEM((n_batch, N_STATES), _F32)] * 2 if state_batch_minor else [])
        + ([pltpu.VMEM(state_shape, _F32)] * 2 if early_state else [])
        + [pltpu.VMEM(named[name].shape, named[name].dtype) for name in late_names if name != "hist0"]
        + [pltpu.SemaphoreType.DMA((2 if n_tiles > 1 else len(late_names) + 3,))],
        compiler_params=pltpu.CompilerParams(dimension_semantics=("arbitrary",), vmem_limit_bytes=vmem_limit),
        name="mixer",
    )(x, *state, *weights, *cast)
    return (outs[0], outs[1].reshape(POOL_HIST, n_batch, POOL_WIDTH), *outs[2:])


def _ffn_kernel(h_a_ref, h_b_ref, norm_ffn_ref, w_gate_ref, w_up_ref, w_down_ref, norm_final_ref,
                y_a_ref, y_b_ref, gate_ref, up_ref, y_buf, h_b_buf, sem, *, n_batch_a, n_sub, n_steps_a):
    j = pl.program_id(0)
    h_b_copy = pltpu.make_async_copy(h_b_ref, h_b_buf, sem.at[2])

    @pl.when(j == 0)
    def _():
        h_b_copy.start()
    sub_rows = h_a_ref.shape[0] // n_sub
    sub_steps = sub_rows // n_batch_a
    steps = n_sub * sub_steps

    def tile(h):
        hn = _rmsnorm(h, norm_ffn_ref[...]).astype(_BF16)
        gate_ref[...] = _dot(hn, w_gate_ref[...])
        up_ref[...] = _dot(hn, w_up_ref[...])
        gate = gate_ref[...]
        f = (gate * jax.nn.sigmoid(gate) * up_ref[...]).astype(_BF16)
        return _rmsnorm(h + _dot(f, w_down_ref[...]), norm_final_ref[...])

    def copies(slot, step):
        t0 = step * steps if isinstance(step, int) else pl.multiple_of(step * steps, steps)
        return [pltpu.make_async_copy(y_buf.at[slot, :, b, :], y_a_ref.at[b, pl.ds(t0, steps), :], sem.at[slot])
                for b in range(n_batch_a)]

    @pl.when(j < n_steps_a)
    def _():
        slot = lax.rem(j, 2)
        for s in range(n_sub):
            y = tile(h_a_ref[s * sub_rows:(s + 1) * sub_rows, :])
            y_buf[slot, s * sub_steps:(s + 1) * sub_steps] = y.reshape(sub_steps, n_batch_a, D_MODEL)

        @pl.when(j > 0)
        def _():
            for c in copies(1 - slot, j - 1):
                c.wait()
        for c in copies(slot, j):
            c.start()

    @pl.when(j == n_steps_a)
    def _():
        n_batch_b, steps_b, _ = y_b_ref.shape
        h_b_copy.wait()
        y = tile(h_b_buf[...]).reshape(steps_b, n_batch_b, D_MODEL)
        y_b_ref[...] = jnp.swapaxes(y, 0, 1)
        for c in copies((n_steps_a - 1) % 2, n_steps_a - 1):
            c.wait()


def _ffn(h_a, h_b, norms, weights, *, n_batch_a, n_batch_b):
    norm_ffn, norm_final = norms
    w_gate, w_up, w_down = weights
    n_sub = FFN_SUB_TILES
    rows_a = ROW_TILE * n_sub
    n_steps_a = h_a.shape[0] // rows_a
    assert n_steps_a * rows_a == h_a.shape[0] and rows_a % n_batch_a == 0 and h_b.shape[0] == ROW_TILE
    t_a, t_b = h_a.shape[0] // n_batch_a, h_b.shape[0] // n_batch_b
    return pl.pallas_call(
        functools.partial(_ffn_kernel, n_batch_a=n_batch_a, n_sub=n_sub, n_steps_a=n_steps_a),
        grid=(n_steps_a + 1,),
        in_specs=[pl.BlockSpec((rows_a, D_MODEL), lambda i: (jnp.minimum(i, n_steps_a - 1), 0)),
                  pl.BlockSpec(memory_space=pl.ANY), _resident(norm_ffn.shape),
                  _resident(w_gate.shape), _resident(w_up.shape), _resident(w_down.shape),
                  _resident(norm_final.shape)],
        out_specs=(pl.BlockSpec(memory_space=pl.ANY),
                   pl.BlockSpec((n_batch_b, t_b, D_MODEL), lambda i: (0, 0, 0))),
        out_shape=(jax.ShapeDtypeStruct((n_batch_a, t_a, D_MODEL), _F32),
                   jax.ShapeDtypeStruct((n_batch_b, t_b, D_MODEL), _F32)),
        scratch_shapes=[pltpu.VMEM((ROW_TILE, D_FF), _F32), pltpu.VMEM((ROW_TILE, D_FF), _F32),
                        pltpu.VMEM((2, rows_a // n_batch_a, n_batch_a, D_MODEL), _F32),
                        pltpu.VMEM(h_b.shape, _F32),
                        pltpu.SemaphoreType.DMA((3,))],
        compiler_params=pltpu.CompilerParams(dimension_semantics=("arbitrary",),
                                             vmem_limit_bytes=VMEM_LIMIT_BYTES),
        name="ffn",
    )(h_a, h_b, norm_ffn, w_gate, w_up, w_down, norm_final)


def kernel(x_prompt, x_sample, state_pool, state_ssm_re, state_ssm_im, norm_mix, w_in, pool_w, pool_scale, ssm_a_re, ssm_a_im, ssm_log_dt, ssm_b_re, ssm_b_im, ssm_c_re, ssm_c_im, ssm_d, glu_w, glu_b, w_branch_pool, w_branch_ssm, w_out, norm_ffn, ffn_w_gate, ffn_w_up, ffn_w_down, norm_final):
    assert w_in.shape[0] == 1, "single-layer trunk"
    (ab_re, ab_im, b_mat, c_mat, pool_mat, w_in_bf, glu_w_bf, w_bp_bf, w_bs_bf, w_out_bf) = _prep(
        ssm_a_re[0], ssm_a_im[0], ssm_log_dt, ssm_b_re[0], ssm_b_im[0], ssm_c_re[0], ssm_c_im[0], pool_w[0],
        (w_in[0], glu_w[0], w_branch_pool[0], w_branch_ssm[0], w_out[0]))
    mixer_w = (norm_mix[0].reshape(1, D_MODEL), w_in_bf, pool_mat,
               pool_scale[0].reshape(1, POOL_WIDTH), ab_re, ab_im, b_mat, c_mat, ssm_d[0].reshape(1, SSM_WIDTH),
               glu_w_bf, glu_b[0].reshape(1, SSM_WIDTH), w_bp_bf, w_bs_bf, w_out_bf)
    ffn_norms = (norm_ffn[0].reshape(1, D_MODEL), norm_final.reshape(1, D_MODEL))

    def mix(x, state, start_pos, cast=()):
        b = x.shape[0]
        batch_minor = b % LANES == 0
        if state is not None:
            flat = ((lambda a: a.transpose(1, 2, 0).reshape(N_STATES, b)) if batch_minor
                    else (lambda a: a.reshape(b, N_STATES)))
            state = (state[0].transpose(1, 0, 2), flat(state[1]), flat(state[2]))
        h_rows, new_hist, s_re, s_im, *cast_bf = _mixer(x, state, mixer_w, start_pos=start_pos,
                                                       state_batch_minor=batch_minor, cast=cast)
        unflat = ((lambda a: a.reshape(SSM_GROUPS, SSM_STATE, b).transpose(2, 0, 1)[None]) if batch_minor
                  else (lambda a: a.reshape(1, b, SSM_GROUPS, SSM_STATE)))
        return h_rows, new_hist.transpose(1, 0, 2)[None], unflat(s_re), unflat(s_im), cast_bf

    h_p, pool_p, re_p, im_p, ffn_w = mix(x_prompt, None, 0, cast=(ffn_w_gate[0], ffn_w_up[0], ffn_w_down[0]))
    h_s, pool_s, re_s, im_s, _ = mix(x_sample, (state_pool[0], state_ssm_re[0], state_ssm_im[0]), PAST_LEN)
    y_p, y_s = _ffn(h_p, h_s, ffn_norms, ffn_w,
                    n_batch_a=x_prompt.shape[0], n_batch_b=x_sample.shape[0])
    return (y_p, y_s, pool_p, re_p, im_p, pool_s, re_s, im_s)
```

```python
import functools
import math

import jax
import jax.numpy as jnp
from jax import lax
from jax.experimental import pallas as pl
from jax.experimental.pallas import tpu as pltpu

D_MODEL = 1024
PAST_LEN = 16384
POOL_WIDTH = 512
POOL_WINDOWS = (2, 4, 8, 16)
POOL_GROUP_DIM = 128
POOL_HIST = 15
HIST_STEPS = POOL_HIST + 1
SSM_WIDTH = 512
SSM_GROUPS = 32
SSM_GROUP_DIM = 16
SSM_STATE = 64
N_STATES = SSM_GROUPS * SSM_STATE
IN_WIDTH = POOL_WIDTH + SSM_WIDTH + 2 * D_MODEL
D_FF = 2816
RMS_EPS = 1e-6

SUBLANES = 8
BF16_ROWS = 16
LANES = 128
MXU_DIM = 256
CHUNK_LANES = 2 * LANES
N_CHUNKS = N_STATES // LANES
HALF_CHUNKS = N_CHUNKS // 2
HALF_LANES = HALF_CHUNKS * CHUNK_LANES
HALF_ROWS = MXU_DIM
ROW_TILE = 512
PREP_STEPS = 4
N_STRUCTURED_IN = 8
N_STRUCTURED_OUT = 5
N_MIXER_WEIGHTS = 14
MIXER_SUB_TILES = 2
FFN_SUB_TILES = 2
VMEM_LIMIT_BYTES = 56 * 1024 * 1024
VMEM_CAP_BYTES = 60 * 1024 * 1024

_F32 = jnp.float32
_BF16 = jnp.bfloat16


def _rmsnorm(x, g):
    return x * lax.rsqrt(jnp.mean(x * x, axis=-1, keepdims=True) + RMS_EPS) * g


def _dot(a, b):
    return jnp.dot(a, b, preferred_element_type=_F32)


def _structured_weights(a_re_ref, a_im_ref, log_dt_ref, b_re_ref, b_im_ref, c_re_ref, c_im_ref, pool_w_ref,
                        ab_re_ref, ab_im_ref, b_mat_ref, c_mat_ref, pool_mat_ref):
    g_n, h_n, p_n = SSM_GROUPS, SSM_GROUP_DIM, SSM_STATE
    twice = lambda v: jnp.concatenate([v, v], axis=-1)

    diag = lax.broadcasted_iota(jnp.int32, (g_n, g_n), 0) == lax.broadcasted_iota(jnp.int32, (g_n, g_n), 1)
    log_dt = jnp.sum(jnp.where(diag, jnp.broadcast_to(log_dt_ref[...], (g_n, g_n)), 0.0), axis=1, keepdims=True)
    dt = jnp.exp(log_dt)
    a_re = a_re_ref[...]
    a_im = a_im_ref[...]
    mag = jnp.exp(a_re * dt)
    ab_re = mag * jnp.cos(a_im * dt)
    ab_im = mag * jnp.sin(a_im * dt)
    num_re = ab_re - 1.0
    num_im = ab_im
    den = a_re * a_re + a_im * a_im
    coef_re = ((num_re * a_re + num_im * a_im) / den)[:, None, :]
    coef_im = ((num_im * a_re - num_re * a_im) / den)[:, None, :]
    ab_re_ref[...] = twice(ab_re)
    ab_im_ref[...] = twice(ab_im)

    b_re = b_re_ref[...]
    b_im = b_im_ref[...]
    bp = [twice(v).reshape(g_n * h_n, LANES)
          for v in (coef_re * b_re - coef_im * b_im, coef_re * b_im + coef_im * b_re)]
    row = lax.broadcasted_iota(jnp.int32, (HALF_ROWS, LANES), 0)
    lane = lax.broadcasted_iota(jnp.int32, (HALF_ROWS, LANES), 1)
    own_lanes = (lane >> 6) == ((row >> 4) & 1)
    for half in range(2):
        for part in range(2):
            mine = jnp.where(own_lanes, bp[part][half * HALF_ROWS:(half + 1) * HALF_ROWS, :], 0.0)
            for cl in range(HALF_CHUNKS):
                blk = jnp.where((row >> 5) == cl, mine, 0.0)
                lo = cl * CHUNK_LANES + part * LANES
                b_mat_ref[half, :, lo:lo + LANES] = blk.astype(_BF16)

    row_c = lax.broadcasted_iota(jnp.int32, (LANES, HALF_ROWS), 0)
    col_c = lax.broadcasted_iota(jnp.int32, (LANES, HALF_ROWS), 1)
    own_cols = (row_c >> 6) == ((col_c >> 4) & 1)
    for part, c_ref in enumerate((c_re_ref, c_im_ref)):
        ct = c_ref[...].reshape(g_n * h_n, p_n).T
        ct = jnp.concatenate([ct, ct], axis=0)
        if part == 1:
            ct = -ct
        for half in range(2):
            mine = jnp.where(own_cols, ct[:, half * HALF_ROWS:(half + 1) * HALF_ROWS], 0.0)
            for cl in range(HALF_CHUNKS):
                blk = jnp.where((col_c >> 5) == cl, mine, 0.0)
                lo = cl * CHUNK_LANES + part * LANES
                c_mat_ref[half, lo:lo + LANES, :] = blk.astype(_BF16)

    pool_mat_ref[...] = jnp.zeros(pool_mat_ref.shape, _BF16)
    for g in range(len(POOL_WINDOWS)):
        lo = (g % 2) * POOL_GROUP_DIM
        pool_mat_ref[g // 2, lo:lo + POOL_GROUP_DIM, lo:lo + POOL_GROUP_DIM] = pool_w_ref[g].astype(_BF16)


def _prep_kernel(*refs):
    n_dense = (len(refs) - N_STRUCTURED_IN - N_STRUCTURED_OUT) // 2
    dense_in = refs[N_STRUCTURED_IN:N_STRUCTURED_IN + n_dense]
    structured_out = refs[N_STRUCTURED_IN + n_dense:N_STRUCTURED_IN + n_dense + N_STRUCTURED_OUT]
    dense_out = refs[N_STRUCTURED_IN + n_dense + N_STRUCTURED_OUT:]

    @pl.when(pl.program_id(0) == 0)
    def _():
        _structured_weights(*refs[:N_STRUCTURED_IN], *structured_out)

    for src, dst in zip(dense_in, dense_out):
        dst[...] = src[...].astype(_BF16)


def _prep(a_re, a_im, log_dt, b_re, b_im, c_re, c_im, pool_w, dense):
    structured = (a_re, a_im, log_dt, b_re.transpose(0, 2, 1), b_im.transpose(0, 2, 1), c_re, c_im, pool_w)
    structured_out = (jax.ShapeDtypeStruct((SSM_GROUPS, LANES), _F32), jax.ShapeDtypeStruct((SSM_GROUPS, LANES), _F32),
                      jax.ShapeDtypeStruct((2, HALF_ROWS, HALF_LANES), _BF16),
                      jax.ShapeDtypeStruct((2, HALF_LANES, HALF_ROWS), _BF16),
                      jax.ShapeDtypeStruct((len(POOL_WINDOWS) // 2, MXU_DIM, MXU_DIM), _BF16))
    assert len(structured) == N_STRUCTURED_IN and len(structured_out) == N_STRUCTURED_OUT
    assert all(w.shape[0] % (PREP_STEPS * BF16_ROWS) == 0 for w in dense)
    whole = lambda a: pl.BlockSpec(a.shape, lambda i, n=len(a.shape): (0,) * n)
    chunk = lambda w: pl.BlockSpec((w.shape[0] // PREP_STEPS, w.shape[1]), lambda i: (i, 0))
    return pl.pallas_call(
        _prep_kernel,
        grid=(PREP_STEPS,),
        in_specs=[whole(a) for a in structured] + [chunk(w) for w in dense],
        out_specs=[whole(a) for a in structured_out] + [chunk(w) for w in dense],
        out_shape=structured_out + tuple(jax.ShapeDtypeStruct(w.shape, _BF16) for w in dense),
        compiler_params=pltpu.CompilerParams(dimension_semantics=("arbitrary",)),
        name="prep",
    )(*structured, *dense)


def _scan_half(xs_ref, s_re_ref, s_im_ref, ab_re_ref, ab_im_ref, half, n_batch, n_steps):
    lane = lax.broadcasted_iota(jnp.int32, (1, LANES), 1)

    def coeff(ref, c):
        row = jnp.where(lane < SSM_STATE, ref[2 * c:2 * c + 1, :], ref[2 * c + 1:2 * c + 2, :])
        return jnp.broadcast_to(row, (SUBLANES, LANES))

    def recur(r0, row_of_step, cls):
        chunks = [half * HALF_CHUNKS + cl for cl in cls]
        ar = [coeff(ab_re_ref, c) for c in chunks]
        ai = [coeff(ab_im_ref, c) for c in chunks]
        sr = [s_re_ref[pl.ds(r0, SUBLANES), c * LANES:(c + 1) * LANES] for c in chunks]
        si = [s_im_ref[pl.ds(r0, SUBLANES), c * LANES:(c + 1) * LANES] for c in chunks]
        for t in range(n_steps):
            rows = pl.ds(row_of_step(t), SUBLANES)
            for k, cl in enumerate(cls):
                lo = cl * CHUNK_LANES
                nr = ar[k] * sr[k] - ai[k] * si[k] + xs_ref[rows, lo:lo + LANES]
                ni = ar[k] * si[k] + ai[k] * sr[k] + xs_ref[rows, lo + LANES:lo + CHUNK_LANES]
                xs_ref[rows, lo:lo + LANES] = nr
                xs_ref[rows, lo + LANES:lo + CHUNK_LANES] = ni
                sr[k], si[k] = nr, ni
        for k, c in enumerate(chunks):
            s_re_ref[pl.ds(r0, SUBLANES), c * LANES:(c + 1) * LANES] = sr[k]
            s_im_ref[pl.ds(r0, SUBLANES), c * LANES:(c + 1) * LANES] = si[k]

    for r0 in range(0, n_batch, SUBLANES):
        recur(r0, lambda t, r0=r0: t * n_batch + r0, list(range(HALF_CHUNKS)))


_MIXER_WEIGHT_NAMES = ("norm", "w_in", "pool_w", "pool_scale", "ab_re", "ab_im", "b", "c", "d", "glu_w", "glu_b",
                       "w_bp", "w_bs", "w_out")
_LATE_OPERANDS = ("hist0", "b", "s0_re", "s0_im", "c", "glu_w", "w_bp", "w_bs", "w_out")


def _mixer_kernel(*refs, n_batch, n_steps, n_sub, n_tiles, start_pos, zero_state, state_batch_minor, n_cast):
    n_state_in = 0 if zero_state else 3
    x_ref = refs[0]
    hist0_ref, s0_re_ref, s0_im_ref = refs[1:1 + n_state_in] if n_state_in else (None, None, None)
    (norm_ref, w_in_ref, pool_w_ref, pool_scale_ref, ab_re_ref, ab_im_ref, b_ref, c_ref, d_ref, glu_w_ref,
     glu_b_ref, w_bp_ref, w_bs_ref, w_out_ref) = refs[1 + n_state_in:1 + n_state_in + N_MIXER_WEIGHTS]
    rest = refs[1 + n_state_in + N_MIXER_WEIGHTS:]
    cast_in, rest = rest[:n_cast], rest[n_cast:]
    (h_ref, hist_out_ref, s_re_out_ref, s_im_out_ref), rest = rest[:4], rest[4:]
    cast_out, scratch = rest[:n_cast], rest[n_cast:]
    i = pl.program_id(0)
    rows = n_batch * n_steps
    sub_steps = n_steps // n_sub
    sub_rows = n_batch * sub_steps
    hist_rows = HIST_STEPS * n_batch

    scratch = list(scratch)
    hist_scratch = [scratch.pop(0)] if n_tiles > 1 else []
    x_stage_ref, proj_ref, ext_ref, xs_ref = scratch[:4]
    s_re_ref, s_im_ref = scratch[4:6] if state_batch_minor else (s_re_out_ref, s_im_out_ref)
    dma_in = n_tiles > 1
    x_sem = scratch[-1] if dma_in else None
    hist_ref = hist_scratch[0] if hist_scratch else ext_ref.at[0:hist_rows]

    named = dict(zip(("hist0", "s0_re", "s0_im")[:n_state_in], refs[1:1 + n_state_in]))
    named.update(zip(_MIXER_WEIGHT_NAMES, refs[1 + n_state_in:1 + n_state_in + N_MIXER_WEIGHTS]))
    late = {}
    if not dma_in:
        late_names = [name for name in _LATE_OPERANDS if name in named]
        late_sem = scratch[-1]
        landing = iter(scratch[-1 - sum(name != "hist0" for name in late_names):-1])
        for k, name in enumerate(late_names):
            buf = ext_ref.at[n_batch:hist_rows] if name == "hist0" else next(landing)
            late[name] = [pltpu.make_async_copy(named[name], buf, late_sem.at[k]), buf]
            late[name][0].start(priority=k % 2)
        hist_out_copy = pltpu.make_async_copy(ext_ref.at[rows + n_batch:rows + hist_rows], hist_out_ref,
                                              late_sem.at[len(late_names)])

    def operand(name):
        if name not in late:
            return named[name]
        copy, buf = late[name]
        if copy is not None:
            copy.wait()
            late[name][0] = None
        return buf

    @pl.when(i == 0)
    def _():
        if zero_state:
            hist_ref[...] = jnp.zeros((hist_rows, POOL_WIDTH), _F32)
            s_re_ref[...] = jnp.zeros(s_re_ref.shape, _F32)
            s_im_ref[...] = jnp.zeros(s_im_ref.shape, _F32)
        else:
            hist_ref[0:n_batch, :] = jnp.zeros((n_batch, POOL_WIDTH), _F32)
            if "hist0" not in late:
                hist_ref[n_batch:hist_rows, :] = hist0_ref[...].reshape(POOL_HIST * n_batch, POOL_WIDTH)
            if "s0_re" not in late:
                s_re_ref[...] = s0_re_ref[...].T if state_batch_minor else s0_re_ref[...]
                s_im_ref[...] = s0_im_ref[...].T if state_batch_minor else s0_im_ref[...]

    if dma_in:
        slot = lax.rem(i, 2)

        def copies(slot_, step):
            t0 = pl.multiple_of(step * n_steps, n_steps)
            return [pltpu.make_async_copy(x_ref.at[b, pl.ds(t0, n_steps), :], x_stage_ref.at[slot_, :, b, :],
                                          x_sem.at[slot_]) for b in range(n_batch)]

        @pl.when(i == 0)
        def _():
            for b, c in enumerate(copies(0, 0)):
                c.start(priority=b % 2)
        for c in copies(slot, i):
            c.wait()

        @pl.when(i + 1 < n_tiles)
        def _():
            for b, c in enumerate(copies(1 - slot, i + 1)):
                c.start(priority=b % 2)
        x_rows = lambda sub: x_stage_ref[slot, sub * sub_steps:(sub + 1) * sub_steps].reshape(sub_rows, D_MODEL)
    else:
        x_stage_ref[...] = jnp.swapaxes(x_ref[...], 0, 1).reshape(rows, D_MODEL)
        x_rows = lambda sub: x_stage_ref[sub * sub_rows:(sub + 1) * sub_rows, :]
    u_width = POOL_WIDTH + SSM_WIDTH
    if hist_scratch:
        ext_ref[0:hist_rows, :] = hist_ref[...]

    def tile(sub):
        base = hist_rows + sub * sub_rows
        xn = _rmsnorm(x_rows(sub), norm_ref[...]).astype(_BF16)
        proj_ref[...] = _dot(xn, w_in_ref[...])

        ext_ref[base:base + sub_rows, :] = proj_ref[:, 0:POOL_WIDTH]
        if "hist0" in late:
            operand("hist0")
        if not dma_in and sub == n_sub - 1:
            hist_out_copy.start()
        diffs = []
        for g, w in enumerate(POOL_WINDOWS):
            lo = g * POOL_GROUP_DIM
            u_g = ext_ref[base:base + sub_rows, lo:lo + POOL_GROUP_DIM]
            acc = u_g
            for j in range(1, w):
                r0 = base - j * n_batch
                acc = acc + ext_ref[r0:r0 + sub_rows, lo:lo + POOL_GROUP_DIM]
            if start_pos + sub * sub_steps + 1 >= w:
                pooled = acc * (1.0 / w)
            else:
                head = min(sub_rows, hist_rows)
                row = lax.broadcasted_iota(jnp.int32, (head, POOL_GROUP_DIM), 0)
                step = lax.shift_right_logical(row, int(math.log2(n_batch)))
                pos = start_pos + i * n_steps + sub * sub_steps + step
                pooled = acc[:head] / jnp.minimum(w, pos + 1).astype(_F32)
                if head < sub_rows:
                    pooled = jnp.concatenate([pooled, acc[head:] * (1.0 / w)], axis=0)
            diffs.append(pooled - u_g)
        mixed = jnp.concatenate(
            [_dot(jnp.concatenate(diffs[0:2], axis=1).astype(_BF16), pool_w_ref[0]),
             _dot(jnp.concatenate(diffs[2:4], axis=1).astype(_BF16), pool_w_ref[1])], axis=1)
        a_out = (mixed * pool_scale_ref[...]).astype(_BF16)

        u_ssm = proj_ref[:, POOL_WIDTH:u_width]
        u_bf = u_ssm.astype(_BF16)
        for name in late:
            operand(name)
        for half in range(2):
            xs_ref[half] = _dot(u_bf[:, half * HALF_ROWS:(half + 1) * HALF_ROWS], operand("b")[half])
        if sub == 0 and "s0_re" in late:
            for s_ref, name in ((s_re_ref, "s0_re"), (s_im_ref, "s0_im")):
                s0 = operand(name)[...]
                s_ref[...] = s0.T if state_batch_minor else s0
        for half in range(2):
            _scan_half(xs_ref.at[half], s_re_ref, s_im_ref, ab_re_ref, ab_im_ref, half, n_batch, sub_steps)
        ys = [_dot(xs_ref[half].astype(_BF16), operand("c")[half]) for half in range(2)]
        y_ssm = jnp.concatenate(ys, axis=1) + d_ref[...] * u_ssm
        z = jax.nn.gelu(y_ssm)
        glu = jax.nn.sigmoid(_dot(z.astype(_BF16), operand("glu_w")[...]) + glu_b_ref[...])
        b_out = (z * glu).astype(_BF16)

        merged = (jax.nn.sigmoid(proj_ref[:, u_width:u_width + D_MODEL]) * _dot(a_out, operand("w_bp")[...])
                  + jax.nn.sigmoid(proj_ref[:, u_width + D_MODEL:u_width + 2 * D_MODEL])
                  * _dot(b_out, operand("w_bs")[...]))
        h_ref[sub * sub_rows:(sub + 1) * sub_rows, :] = (x_rows(sub)
                                                         + _dot(merged.astype(_BF16), operand("w_out")[...]))

    for sub in range(n_sub):
        tile(sub)
    if hist_scratch:
        hist_ref[...] = ext_ref[rows:rows + hist_rows, :]

    for src, dst in zip(cast_in, cast_out):
        dst[...] = src[...].astype(_BF16)

    @pl.when(i == n_tiles - 1)
    def _():
        if dma_in:
            hist_out_ref[...] = hist_ref[n_batch:hist_rows, :].reshape(POOL_HIST, n_batch, POOL_WIDTH)
        else:
            hist_out_copy.wait()
        if state_batch_minor:
            s_re_out_ref[...] = s_re_ref[...].T
            s_im_out_ref[...] = s_im_ref[...].T


def _resident(shape):
    zeros = (0,) * len(shape)
    return pl.BlockSpec(shape, lambda i: zeros, pipeline_mode=pl.Buffered(1))


def _tiling(n_batch, n_steps_total, rows_per_step=ROW_TILE):
    n_steps = min(n_steps_total, rows_per_step // n_batch)
    n_tiles = n_steps_total // n_steps
    assert n_tiles * n_steps == n_steps_total and n_batch % SUBLANES == 0
    assert n_batch & (n_batch - 1) == 0
    return n_steps, n_tiles


def _mixer(x, state, weights, *, start_pos, state_batch_minor, cast=()):
    n_batch, n_steps_total, _ = x.shape
    n_sub = max(1, min(MIXER_SUB_TILES, n_steps_total * n_batch // ROW_TILE))
    n_steps, n_tiles = _tiling(n_batch, n_steps_total, ROW_TILE * n_sub)
    rows = n_steps * n_batch
    sub_rows = rows // n_sub
    hist_rows = HIST_STEPS * n_batch
    state = () if state is None else tuple(state)
    hist_rows_shape = (POOL_HIST * n_batch, POOL_WIDTH)
    if state and n_tiles == 1:
        state = (state[0].reshape(hist_rows_shape),) + state[1:]
    state_shape = (N_STATES, n_batch) if state_batch_minor else (n_batch, N_STATES)
    assert len(weights) == N_MIXER_WEIGHTS
    kern = functools.partial(_mixer_kernel, n_batch=n_batch, n_steps=n_steps, n_sub=n_sub, n_tiles=n_tiles,
                             start_pos=start_pos,
                             zero_state=not state, state_batch_minor=state_batch_minor, n_cast=len(cast))

    named = dict(zip(("hist0", "s0_re", "s0_im"), state))
    named.update(zip(_MIXER_WEIGHT_NAMES, weights))
    late_names = [name for name in _LATE_OPERANDS if name in named] if n_tiles == 1 else []

    def cast_chunks(w):
        n = max(k for k in range(1, n_tiles + 1) if w.shape[0] % (k * BF16_ROWS) == 0)
        return pl.BlockSpec((w.shape[0] // n, w.shape[1]), lambda i: (jnp.minimum(i, n - 1), 0))
    state_block = pl.BlockSpec(state_shape, lambda i: (0, 0))
    hist_window_bytes = (2 if state else 1) * POOL_HIST * n_batch * POOL_WIDTH * 4
    vmem_limit = min(VMEM_CAP_BYTES, VMEM_LIMIT_BYTES + hist_window_bytes - hist_window_bytes % (1 << 20))
    outs = pl.pallas_call(
        kern,
        grid=(n_tiles,),
        in_specs=[pl.BlockSpec(memory_space=pl.ANY) if n_tiles > 1
                  else pl.BlockSpec((n_batch, n_steps, D_MODEL), lambda i: (0, i, 0))]
        + [pl.BlockSpec(memory_space=pl.ANY) if name in late_names else _resident(a.shape)
           for name, a in named.items()] + [cast_chunks(w) for w in cast],
        out_specs=[pl.BlockSpec((rows, D_MODEL), lambda i: (i, 0)),
                   pl.BlockSpec((POOL_HIST, n_batch, POOL_WIDTH), lambda i: (0, 0, 0)) if n_tiles > 1
                   else pl.BlockSpec(memory_space=pl.ANY),
                   state_block, state_block] + [cast_chunks(w) for w in cast],
        out_shape=[jax.ShapeDtypeStruct((n_steps_total * n_batch, D_MODEL), _F32),
                   jax.ShapeDtypeStruct((POOL_HIST, n_batch, POOL_WIDTH) if n_tiles > 1 else hist_rows_shape, _F32),
                   jax.ShapeDtypeStruct(state_shape, _F32),
                   jax.ShapeDtypeStruct(state_shape, _F32)] + [jax.ShapeDtypeStruct(w.shape, _BF16) for w in cast],
        scratch_shapes=([pltpu.VMEM((hist_rows, POOL_WIDTH), _F32)] if n_tiles > 1 else [])
        + [pltpu.VMEM((2, n_steps, n_batch, D_MODEL) if n_tiles > 1 else (rows, D_MODEL), _F32),
           pltpu.VMEM((sub_rows, IN_WIDTH), _F32),
           pltpu.VMEM((hist_rows + rows, POOL_WIDTH), _F32),
           pltpu.VMEM((2, sub_rows, HALF_LANES), _F32)]
        + ([pltpu.VMEM((n_batch, N_STATES), _F32)] * 2 if state_batch_minor else [])
        + [pltpu.VMEM(named[name].shape, named[name].dtype) for name in late_names if name != "hist0"]
        + [pltpu.SemaphoreType.DMA((2 if n_tiles > 1 else len(late_names) + 1,))],
        compiler_params=pltpu.CompilerParams(dimension_semantics=("arbitrary",), vmem_limit_bytes=vmem_limit),
        name="mixer",
    )(x, *state, *weights, *cast)
    return (outs[0], outs[1].reshape(POOL_HIST, n_batch, POOL_WIDTH), *outs[2:])


def _ffn_kernel(h_a_ref, h_b_ref, norm_ffn_ref, w_gate_ref, w_up_ref, w_down_ref, norm_final_ref,
                y_a_ref, y_b_ref, gate_ref, up_ref, y_buf, sem, *, n_batch_a, n_sub, n_steps_a):
    j = pl.program_id(0)
    sub_rows = h_a_ref.shape[0] // n_sub
    sub_steps = sub_rows // n_batch_a
    steps = n_sub * sub_steps

    def tile(h):
        hn = _rmsnorm(h, norm_ffn_ref[...]).astype(_BF16)
        gate_ref[...] = _dot(hn, w_gate_ref[...])
        up_ref[...] = _dot(hn, w_up_ref[...])
        gate = gate_ref[...]
        f = (gate * jax.nn.sigmoid(gate) * up_ref[...]).astype(_BF16)
        return _rmsnorm(h + _dot(f, w_down_ref[...]), norm_final_ref[...])

    def copies(slot, step):
        t0 = step * steps if isinstance(step, int) else pl.multiple_of(step * steps, steps)
        return [pltpu.make_async_copy(y_buf.at[slot, :, b, :], y_a_ref.at[b, pl.ds(t0, steps), :], sem.at[slot])
                for b in range(n_batch_a)]

    @pl.when(j < n_steps_a)
    def _():
        slot = lax.rem(j, 2)
        for s in range(n_sub):
            y = tile(h_a_ref[s * sub_rows:(s + 1) * sub_rows, :])
            y_buf[slot, s * sub_steps:(s + 1) * sub_steps] = y.reshape(sub_steps, n_batch_a, D_MODEL)

        @pl.when(j > 0)
        def _():
            for c in copies(1 - slot, j - 1):
                c.wait()
        for b, c in enumerate(copies(slot, j)):
            c.start(priority=b % 2)

    @pl.when(j == n_steps_a)
    def _():
        n_batch_b, steps_b, _ = y_b_ref.shape
        y = tile(h_b_ref[...]).reshape(steps_b, n_batch_b, D_MODEL)
        y_b_ref[...] = jnp.swapaxes(y, 0, 1)
        for c in copies((n_steps_a - 1) % 2, n_steps_a - 1):
            c.wait()


def _ffn(h_a, h_b, norms, weights, *, n_batch_a, n_batch_b):
    norm_ffn, norm_final = norms
    w_gate, w_up, w_down = weights
    n_sub = FFN_SUB_TILES
    rows_a = ROW_TILE * n_sub
    n_steps_a = h_a.shape[0] // rows_a
    assert n_steps_a * rows_a == h_a.shape[0] and rows_a % n_batch_a == 0 and h_b.shape[0] == ROW_TILE
    t_a, t_b = h_a.shape[0] // n_batch_a, h_b.shape[0] // n_batch_b
    return pl.pallas_call(
        functools.partial(_ffn_kernel, n_batch_a=n_batch_a, n_sub=n_sub, n_steps_a=n_steps_a),
        grid=(n_steps_a + 1,),
        in_specs=[pl.BlockSpec((rows_a, D_MODEL), lambda i: (jnp.minimum(i, n_steps_a - 1), 0)),
                  _resident(h_b.shape), _resident(norm_ffn.shape),
                  _resident(w_gate.shape), _resident(w_up.shape), _resident(w_down.shape),
                  _resident(norm_final.shape)],
        out_specs=(pl.BlockSpec(memory_space=pl.ANY),
                   pl.BlockSpec((n_batch_b, t_b, D_MODEL), lambda i: (0, 0, 0))),
        out_shape=(jax.ShapeDtypeStruct((n_batch_a, t_a, D_MODEL), _F32),
                   jax.ShapeDtypeStruct((n_batch_b, t_b, D_MODEL), _F32)),
        scratch_shapes=[pltpu.VMEM((ROW_TILE, D_FF), _F32), pltpu.VMEM((ROW_TILE, D_FF), _F32),
                        pltpu.VMEM((2, rows_a // n_batch_a, n_batch_a, D_MODEL), _F32),
                        pltpu.SemaphoreType.DMA((2,))],
        compiler_params=pltpu.CompilerParams(dimension_semantics=("arbitrary",),
                                             vmem_limit_bytes=VMEM_LIMIT_BYTES),
        name="ffn",
    )(h_a, h_b, norm_ffn, w_gate, w_up, w_down, norm_final)


def kernel(x_prompt, x_sample, state_pool, state_ssm_re, state_ssm_im, norm_mix, w_in, pool_w, pool_scale, ssm_a_re, ssm_a_im, ssm_log_dt, ssm_b_re, ssm_b_im, ssm_c_re, ssm_c_im, ssm_d, glu_w, glu_b, w_branch_pool, w_branch_ssm, w_out, norm_ffn, ffn_w_gate, ffn_w_up, ffn_w_down, norm_final):
    assert w_in.shape[0] == 1, "single-layer trunk"
    (ab_re, ab_im, b_mat, c_mat, pool_mat, w_in_bf, glu_w_bf, w_bp_bf, w_bs_bf, w_out_bf) = _prep(
        ssm_a_re[0], ssm_a_im[0], ssm_log_dt, ssm_b_re[0], ssm_b_im[0], ssm_c_re[0], ssm_c_im[0], pool_w[0],
        (w_in[0], glu_w[0], w_branch_pool[0], w_branch_ssm[0], w_out[0]))
    mixer_w = (norm_mix[0].reshape(1, D_MODEL), w_in_bf, pool_mat,
               pool_scale[0].reshape(1, POOL_WIDTH), ab_re, ab_im, b_mat, c_mat, ssm_d[0].reshape(1, SSM_WIDTH),
               glu_w_bf, glu_b[0].reshape(1, SSM_WIDTH), w_bp_bf, w_bs_bf, w_out_bf)
    ffn_norms = (norm_ffn[0].reshape(1, D_MODEL), norm_final.reshape(1, D_MODEL))

    def mix(x, state, start_pos, cast=()):
        b = x.shape[0]
        batch_minor = b % LANES == 0
        if state is not None:
            flat = ((lambda a: a.transpose(1, 2, 0).reshape(N_STATES, b)) if batch_minor
                    else (lambda a: a.reshape(b, N_STATES)))
            state = (state[0].transpose(1, 0, 2), flat(state[1]), flat(state[2]))
        h_rows, new_hist, s_re, s_im, *cast_bf = _mixer(x, state, mixer_w, start_pos=start_pos,
                                                       state_batch_minor=batch_minor, cast=cast)
        unflat = ((lambda a: a.reshape(SSM_GROUPS, SSM_STATE, b).transpose(2, 0, 1)[None]) if batch_minor
                  else (lambda a: a.reshape(1, b, SSM_GROUPS, SSM_STATE)))
        return h_rows, new_hist.transpose(1, 0, 2)[None], unflat(s_re), unflat(s_im), cast_bf

    h_p, pool_p, re_p, im_p, ffn_w = mix(x_prompt, None, 0, cast=(ffn_w_gate[0], ffn_w_up[0], ffn_w_down[0]))
    h_s, pool_s, re_s, im_s, _ = mix(x_sample, (state_pool[0], state_ssm_re[0], state_ssm_im[0]), PAST_LEN)
    y_p, y_s = _ffn(h_p, h_s, ffn_norms, ffn_w,
                    n_batch_a=x_prompt.shape[0], n_batch_b=x_sample.shape[0])
    return (y_p, y_s, pool_p, re_p, im_p, pool_s, re_s, im_s)
```

```python
import functools
import math

import jax
import jax.numpy as jnp
from jax import lax
from jax.experimental import pallas as pl
from jax.experimental.pallas import tpu as pltpu

D_MODEL = 1024
PAST_LEN = 16384
POOL_WIDTH = 512
POOL_WINDOWS = (2, 4, 8, 16)
POOL_GROUP_DIM = 128
POOL_HIST = 15
HIST_STEPS = POOL_HIST + 1
SSM_WIDTH = 512
SSM_GROUPS = 32
SSM_GROUP_DIM = 16
SSM_STATE = 64
N_STATES = SSM_GROUPS * SSM_STATE
IN_WIDTH = POOL_WIDTH + SSM_WIDTH + 2 * D_MODEL
D_FF = 2816
RMS_EPS = 1e-6

SUBLANES = 8
BF16_ROWS = 16
LANES = 128
MXU_DIM = 256
CHUNK_LANES = 2 * LANES
N_CHUNKS = N_STATES // LANES
HALF_CHUNKS = N_CHUNKS // 2
HALF_LANES = HALF_CHUNKS * CHUNK_LANES
HALF_ROWS = MXU_DIM
ROW_TILE = 512
PREP_STEPS = 4
N_STRUCTURED_IN = 8
N_STRUCTURED_OUT = 5
N_MIXER_WEIGHTS = 14
MIXER_SUB_TILES = 2
FFN_SUB_TILES = 2
VMEM_LIMIT_BYTES = 56 * 1024 * 1024
VMEM_CAP_BYTES = 60 * 1024 * 1024

_F32 = jnp.float32
_BF16 = jnp.bfloat16


def _rmsnorm(x, g):
    return x * lax.rsqrt(jnp.mean(x * x, axis=-1, keepdims=True) + RMS_EPS) * g


def _dot(a, b):
    return jnp.dot(a, b, preferred_element_type=_F32)


def _structured_weights(a_re_ref, a_im_ref, log_dt_ref, b_re_ref, b_im_ref, c_re_ref, c_im_ref, pool_w_ref,
                        ab_re_ref, ab_im_ref, b_mat_ref, c_mat_ref, pool_mat_ref):
    g_n, h_n, p_n = SSM_GROUPS, SSM_GROUP_DIM, SSM_STATE
    twice = lambda v: jnp.concatenate([v, v], axis=-1)

    diag = lax.broadcasted_iota(jnp.int32, (g_n, g_n), 0) == lax.broadcasted_iota(jnp.int32, (g_n, g_n), 1)
    log_dt = jnp.sum(jnp.where(diag, jnp.broadcast_to(log_dt_ref[...], (g_n, g_n)), 0.0), axis=1, keepdims=True)
    dt = jnp.exp(log_dt)
    a_re = a_re_ref[...]
    a_im = a_im_ref[...]
    mag = jnp.exp(a_re * dt)
    ab_re = mag * jnp.cos(a_im * dt)
    ab_im = mag * jnp.sin(a_im * dt)
    num_re = ab_re - 1.0
    num_im = ab_im
    den = a_re * a_re + a_im * a_im
    coef_re = ((num_re * a_re + num_im * a_im) / den)[:, None, :]
    coef_im = ((num_im * a_re - num_re * a_im) / den)[:, None, :]
    ab_re_ref[...] = twice(ab_re)
    ab_im_ref[...] = twice(ab_im)

    b_re = b_re_ref[...]
    b_im = b_im_ref[...]
    bp = [twice(v).reshape(g_n * h_n, LANES)
          for v in (coef_re * b_re - coef_im * b_im, coef_re * b_im + coef_im * b_re)]
    row = lax.broadcasted_iota(jnp.int32, (HALF_ROWS, LANES), 0)
    lane = lax.broadcasted_iota(jnp.int32, (HALF_ROWS, LANES), 1)
    own_lanes = (lane >> 6) == ((row >> 4) & 1)
    for half in range(2):
        for part in range(2):
            mine = jnp.where(own_lanes, bp[part][half * HALF_ROWS:(half + 1) * HALF_ROWS, :], 0.0)
            for cl in range(HALF_CHUNKS):
                blk = jnp.where((row >> 5) == cl, mine, 0.0)
                lo = cl * CHUNK_LANES + part * LANES
                b_mat_ref[half, :, lo:lo + LANES] = blk.astype(_BF16)

    row_c = lax.broadcasted_iota(jnp.int32, (LANES, HALF_ROWS), 0)
    col_c = lax.broadcasted_iota(jnp.int32, (LANES, HALF_ROWS), 1)
    own_cols = (row_c >> 6) == ((col_c >> 4) & 1)
    for part, c_ref in enumerate((c_re_ref, c_im_ref)):
        ct = c_ref[...].reshape(g_n * h_n, p_n).T
        ct = jnp.concatenate([ct, ct], axis=0)
        if part == 1:
            ct = -ct
        for half in range(2):
            mine = jnp.where(own_cols, ct[:, half * HALF_ROWS:(half + 1) * HALF_ROWS], 0.0)
            for cl in range(HALF_CHUNKS):
                blk = jnp.where((col_c >> 5) == cl, mine, 0.0)
                lo = cl * CHUNK_LANES + part * LANES
                c_mat_ref[half, lo:lo + LANES, :] = blk.astype(_BF16)

    pool_mat_ref[...] = jnp.zeros(pool_mat_ref.shape, _BF16)
    for g in range(len(POOL_WINDOWS)):
        lo = (g % 2) * POOL_GROUP_DIM
        pool_mat_ref[g // 2, lo:lo + POOL_GROUP_DIM, lo:lo + POOL_GROUP_DIM] = pool_w_ref[g].astype(_BF16)


def _prep_kernel(*refs):
    n_dense = (len(refs) - N_STRUCTURED_IN - N_STRUCTURED_OUT) // 2
    dense_in = refs[N_STRUCTURED_IN:N_STRUCTURED_IN + n_dense]
    structured_out = refs[N_STRUCTURED_IN + n_dense:N_STRUCTURED_IN + n_dense + N_STRUCTURED_OUT]
    dense_out = refs[N_STRUCTURED_IN + n_dense + N_STRUCTURED_OUT:]

    @pl.when(pl.program_id(0) == 0)
    def _():
        _structured_weights(*refs[:N_STRUCTURED_IN], *structured_out)

    for src, dst in zip(dense_in, dense_out):
        dst[...] = src[...].astype(_BF16)


def _prep(a_re, a_im, log_dt, b_re, b_im, c_re, c_im, pool_w, dense):
    structured = (a_re, a_im, log_dt, b_re.transpose(0, 2, 1), b_im.transpose(0, 2, 1), c_re, c_im, pool_w)
    structured_out = (jax.ShapeDtypeStruct((SSM_GROUPS, LANES), _F32), jax.ShapeDtypeStruct((SSM_GROUPS, LANES), _F32),
                      jax.ShapeDtypeStruct((2, HALF_ROWS, HALF_LANES), _BF16),
                      jax.ShapeDtypeStruct((2, HALF_LANES, HALF_ROWS), _BF16),
                      jax.ShapeDtypeStruct((len(POOL_WINDOWS) // 2, MXU_DIM, MXU_DIM), _BF16))
    assert len(structured) == N_STRUCTURED_IN and len(structured_out) == N_STRUCTURED_OUT
    assert all(w.shape[0] % (PREP_STEPS * BF16_ROWS) == 0 for w in dense)
    whole = lambda a: pl.BlockSpec(a.shape, lambda i, n=len(a.shape): (0,) * n)
    chunk = lambda w: pl.BlockSpec((w.shape[0] // PREP_STEPS, w.shape[1]), lambda i: (i, 0))
    return pl.pallas_call(
        _prep_kernel,
        grid=(PREP_STEPS,),
        in_specs=[whole(a) for a in structured] + [chunk(w) for w in dense],
        out_specs=[whole(a) for a in structured_out] + [chunk(w) for w in dense],
        out_shape=structured_out + tuple(jax.ShapeDtypeStruct(w.shape, _BF16) for w in dense),
        compiler_params=pltpu.CompilerParams(dimension_semantics=("arbitrary",)),
        name="prep",
    )(*structured, *dense)


def _scan_half(xs_ref, s_re_ref, s_im_ref, ab_re_ref, ab_im_ref, half, n_batch, n_steps):
    lane = lax.broadcasted_iota(jnp.int32, (1, LANES), 1)

    def coeff(ref, c):
        row = jnp.where(lane < SSM_STATE, ref[2 * c:2 * c + 1, :], ref[2 * c + 1:2 * c + 2, :])
        return jnp.broadcast_to(row, (SUBLANES, LANES))

    def recur(r0, row_of_step, cls):
        chunks = [half * HALF_CHUNKS + cl for cl in cls]
        ar = [coeff(ab_re_ref, c) for c in chunks]
        ai = [coeff(ab_im_ref, c) for c in chunks]
        sr = [s_re_ref[pl.ds(r0, SUBLANES), c * LANES:(c + 1) * LANES] for c in chunks]
        si = [s_im_ref[pl.ds(r0, SUBLANES), c * LANES:(c + 1) * LANES] for c in chunks]
        for t in range(n_steps):
            rows = pl.ds(row_of_step(t), SUBLANES)
            for k, cl in enumerate(cls):
                lo = cl * CHUNK_LANES
                nr = ar[k] * sr[k] - ai[k] * si[k] + xs_ref[rows, lo:lo + LANES]
                ni = ar[k] * si[k] + ai[k] * sr[k] + xs_ref[rows, lo + LANES:lo + CHUNK_LANES]
                xs_ref[rows, lo:lo + LANES] = nr
                xs_ref[rows, lo + LANES:lo + CHUNK_LANES] = ni
                sr[k], si[k] = nr, ni
        for k, c in enumerate(chunks):
            s_re_ref[pl.ds(r0, SUBLANES), c * LANES:(c + 1) * LANES] = sr[k]
            s_im_ref[pl.ds(r0, SUBLANES), c * LANES:(c + 1) * LANES] = si[k]

    for r0 in range(0, n_batch, SUBLANES):
        recur(r0, lambda t, r0=r0: t * n_batch + r0, list(range(HALF_CHUNKS)))


_MIXER_WEIGHT_NAMES = ("norm", "w_in", "pool_w", "pool_scale", "ab_re", "ab_im", "b", "c", "d", "glu_w", "glu_b",
                       "w_bp", "w_bs", "w_out")
_LATE_OPERANDS = ("hist0", "b", "s0_re", "s0_im", "c", "glu_w", "w_bp", "w_bs", "w_out")


def _mixer_kernel(*refs, n_batch, n_steps, n_sub, n_tiles, start_pos, zero_state, state_batch_minor, n_cast,
                  state_grouped=False):
    n_state_in = 0 if zero_state else 3
    x_ref = refs[0]
    hist0_ref, s0_re_ref, s0_im_ref = refs[1:1 + n_state_in] if n_state_in else (None, None, None)
    (norm_ref, w_in_ref, pool_w_ref, pool_scale_ref, ab_re_ref, ab_im_ref, b_ref, c_ref, d_ref, glu_w_ref,
     glu_b_ref, w_bp_ref, w_bs_ref, w_out_ref) = refs[1 + n_state_in:1 + n_state_in + N_MIXER_WEIGHTS]
    rest = refs[1 + n_state_in + N_MIXER_WEIGHTS:]
    cast_in, rest = rest[:n_cast], rest[n_cast:]
    (h_ref, hist_out_ref, s_re_out_ref, s_im_out_ref), rest = rest[:4], rest[4:]
    cast_out, scratch = rest[:n_cast], rest[n_cast:]
    i = pl.program_id(0)
    rows = n_batch * n_steps
    sub_steps = n_steps // n_sub
    sub_rows = n_batch * sub_steps
    hist_rows = HIST_STEPS * n_batch

    scratch = list(scratch)
    hist_scratch = [scratch.pop(0)] if n_tiles > 1 else []
    x_stage_ref, proj_ref, ext_ref, xs_ref = scratch[:4]
    s_re_ref, s_im_ref = scratch[4:6] if state_batch_minor or state_grouped else (s_re_out_ref, s_im_out_ref)
    dma_in = n_tiles > 1
    x_sem = scratch[-1] if dma_in else None
    hist_ref = hist_scratch[0] if hist_scratch else ext_ref.at[0:hist_rows]

    named = dict(zip(("hist0", "s0_re", "s0_im")[:n_state_in], refs[1:1 + n_state_in]))
    named.update(zip(_MIXER_WEIGHT_NAMES, refs[1 + n_state_in:1 + n_state_in + N_MIXER_WEIGHTS]))
    late = {}
    if not dma_in:
        late_names = [name for name in _LATE_OPERANDS if name in named]
        late_sem = scratch[-1]
        landing = iter(scratch[-1 - sum(name != "hist0" for name in late_names):-1])
        for k, name in enumerate(late_names):
            buf = ext_ref.at[n_batch:hist_rows] if name == "hist0" else next(landing)
            late[name] = [pltpu.make_async_copy(named[name], buf, late_sem.at[k]), buf]
            late[name][0].start()
        hist_out_copy = pltpu.make_async_copy(ext_ref.at[rows + n_batch:rows + hist_rows], hist_out_ref,
                                              late_sem.at[len(late_names)])

    def operand(name):
        if name not in late:
            return named[name]
        copy, buf = late[name]
        if copy is not None:
            copy.wait()
            late[name][0] = None
        return buf

    @pl.when(i == 0)
    def _():
        if zero_state:
            hist_ref[...] = jnp.zeros((hist_rows, POOL_WIDTH), _F32)
            s_re_ref[...] = jnp.zeros(s_re_ref.shape, _F32)
            s_im_ref[...] = jnp.zeros(s_im_ref.shape, _F32)
        else:
            hist_ref[0:n_batch, :] = jnp.zeros((n_batch, POOL_WIDTH), _F32)
            if "hist0" not in late:
                hist_ref[n_batch:hist_rows, :] = hist0_ref[...].reshape(POOL_HIST * n_batch, POOL_WIDTH)
            if "s0_re" not in late:
                s_re_ref[...] = s0_re_ref[...].T if state_batch_minor else s0_re_ref[...]
                s_im_ref[...] = s0_im_ref[...].T if state_batch_minor else s0_im_ref[...]

    if dma_in:
        slot = lax.rem(i, 2)

        def copies(slot_, step):
            t0 = pl.multiple_of(step * n_steps, n_steps)
            return [pltpu.make_async_copy(x_ref.at[b, pl.ds(t0, n_steps), :], x_stage_ref.at[slot_, :, b, :],
                                          x_sem.at[slot_]) for b in range(n_batch)]

        @pl.when(i == 0)
        def _():
            for c in copies(0, 0):
                c.start()
        for c in copies(slot, i):
            c.wait()

        @pl.when(i + 1 < n_tiles)
        def _():
            for c in copies(1 - slot, i + 1):
                c.start()
        x_rows = lambda sub: x_stage_ref[slot, sub * sub_steps:(sub + 1) * sub_steps].reshape(sub_rows, D_MODEL)
    else:
        x_stage_ref[...] = jnp.swapaxes(x_ref[...], 0, 1).reshape(rows, D_MODEL)
        x_rows = lambda sub: x_stage_ref[sub * sub_rows:(sub + 1) * sub_rows, :]
    u_width = POOL_WIDTH + SSM_WIDTH
    if hist_scratch:
        ext_ref[0:hist_rows, :] = hist_ref[...]

    def tile(sub):
        base = hist_rows + sub * sub_rows
        xn = _rmsnorm(x_rows(sub), norm_ref[...]).astype(_BF16)
        proj_ref[...] = _dot(xn, w_in_ref[...])

        ext_ref[base:base + sub_rows, :] = proj_ref[:, 0:POOL_WIDTH]
        if "hist0" in late:
            operand("hist0")
        if not dma_in and sub == n_sub - 1:
            hist_out_copy.start()
        diffs = []
        for g, w in enumerate(POOL_WINDOWS):
            lo = g * POOL_GROUP_DIM
            u_g = ext_ref[base:base + sub_rows, lo:lo + POOL_GROUP_DIM]
            acc = u_g
            for j in range(1, w):
                r0 = base - j * n_batch
                acc = acc + ext_ref[r0:r0 + sub_rows, lo:lo + POOL_GROUP_DIM]
            if start_pos + sub * sub_steps + 1 >= w:
                pooled = acc * (1.0 / w)
            else:
                head = min(sub_rows, hist_rows)
                row = lax.broadcasted_iota(jnp.int32, (head, POOL_GROUP_DIM), 0)
                step = lax.shift_right_logical(row, int(math.log2(n_batch)))
                pos = start_pos + i * n_steps + sub * sub_steps + step
                pooled = acc[:head] / jnp.minimum(w, pos + 1).astype(_F32)
                if head < sub_rows:
                    pooled = jnp.concatenate([pooled, acc[head:] * (1.0 / w)], axis=0)
            diffs.append(pooled - u_g)
        mixed = jnp.concatenate(
            [_dot(jnp.concatenate(diffs[0:2], axis=1).astype(_BF16), pool_w_ref[0]),
             _dot(jnp.concatenate(diffs[2:4], axis=1).astype(_BF16), pool_w_ref[1])], axis=1)
        a_out = (mixed * pool_scale_ref[...]).astype(_BF16)

        u_ssm = proj_ref[:, POOL_WIDTH:u_width]
        u_bf = u_ssm.astype(_BF16)
        for name in late:
            operand(name)
        for half in range(2):
            xs_ref[half] = _dot(u_bf[:, half * HALF_ROWS:(half + 1) * HALF_ROWS], operand("b")[half])
        if sub == 0 and "s0_re" in late:
            for s_ref, name in ((s_re_ref, "s0_re"), (s_im_ref, "s0_im")):
                s0 = operand(name)[...]
                s_ref[...] = s0.T if state_batch_minor else s0
        for half in range(2):
            _scan_half(xs_ref.at[half], s_re_ref, s_im_ref, ab_re_ref, ab_im_ref, half, n_batch, sub_steps)
        ys = [_dot(xs_ref[half].astype(_BF16), operand("c")[half]) for half in range(2)]
        y_ssm = jnp.concatenate(ys, axis=1) + d_ref[...] * u_ssm
        z = jax.nn.gelu(y_ssm)
        glu = jax.nn.sigmoid(_dot(z.astype(_BF16), operand("glu_w")[...]) + glu_b_ref[...])
        b_out = (z * glu).astype(_BF16)

        merged = (jax.nn.sigmoid(proj_ref[:, u_width:u_width + D_MODEL]) * _dot(a_out, operand("w_bp")[...])
                  + jax.nn.sigmoid(proj_ref[:, u_width + D_MODEL:u_width + 2 * D_MODEL])
                  * _dot(b_out, operand("w_bs")[...]))
        h_ref[sub * sub_rows:(sub + 1) * sub_rows, :] = (x_rows(sub)
                                                         + _dot(merged.astype(_BF16), operand("w_out")[...]))

    for sub in range(n_sub):
        tile(sub)
    if hist_scratch:
        hist_ref[...] = ext_ref[rows:rows + hist_rows, :]

    for src, dst in zip(cast_in, cast_out):
        dst[...] = src[...].astype(_BF16)

    @pl.when(i == n_tiles - 1)
    def _():
        if dma_in:
            hist_out_ref[...] = hist_ref[n_batch:hist_rows, :].reshape(POOL_HIST, n_batch, POOL_WIDTH)
        else:
            hist_out_copy.wait()
        if state_batch_minor:
            s_re_out_ref[...] = s_re_ref[...].T
            s_im_out_ref[...] = s_im_ref[...].T
        elif state_grouped:
            s_re_out_ref[...] = s_re_ref[...].reshape(s_re_out_ref.shape)
            s_im_out_ref[...] = s_im_ref[...].reshape(s_im_out_ref.shape)


def _resident(shape):
    zeros = (0,) * len(shape)
    return pl.BlockSpec(shape, lambda i: zeros, pipeline_mode=pl.Buffered(1))


def _tiling(n_batch, n_steps_total, rows_per_step=ROW_TILE):
    n_steps = min(n_steps_total, rows_per_step // n_batch)
    n_tiles = n_steps_total // n_steps
    assert n_tiles * n_steps == n_steps_total and n_batch % SUBLANES == 0
    assert n_batch & (n_batch - 1) == 0
    return n_steps, n_tiles


def _mixer(x, state, weights, *, start_pos, state_batch_minor, cast=()):
    n_batch, n_steps_total, _ = x.shape
    n_sub = max(1, min(MIXER_SUB_TILES, n_steps_total * n_batch // ROW_TILE))
    n_steps, n_tiles = _tiling(n_batch, n_steps_total, ROW_TILE * n_sub)
    rows = n_steps * n_batch
    sub_rows = rows // n_sub
    hist_rows = HIST_STEPS * n_batch
    state = () if state is None else tuple(state)
    hist_rows_shape = (POOL_HIST * n_batch, POOL_WIDTH)
    if state and n_tiles == 1:
        state = (state[0].reshape(hist_rows_shape),) + state[1:]
    state_grouped = not state_batch_minor and not state
    state_shape = ((N_STATES, n_batch) if state_batch_minor else
                   (n_batch, SSM_GROUPS, SSM_STATE) if state_grouped else (n_batch, N_STATES))
    assert len(weights) == N_MIXER_WEIGHTS
    kern = functools.partial(_mixer_kernel, n_batch=n_batch, n_steps=n_steps, n_sub=n_sub, n_tiles=n_tiles,
                             start_pos=start_pos,
                             zero_state=not state, state_batch_minor=state_batch_minor, n_cast=len(cast),
                             state_grouped=state_grouped)

    named = dict(zip(("hist0", "s0_re", "s0_im"), state))
    named.update(zip(_MIXER_WEIGHT_NAMES, weights))
    late_names = [name for name in _LATE_OPERANDS if name in named] if n_tiles == 1 else []

    def cast_chunks(w):
        n = max(k for k in range(1, n_tiles + 1) if w.shape[0] % (k * BF16_ROWS) == 0)
        return pl.BlockSpec((w.shape[0] // n, w.shape[1]), lambda i: (jnp.minimum(i, n - 1), 0))
    state_block = pl.BlockSpec(state_shape, lambda i: (0,) * len(state_shape))
    hist_window_bytes = (2 if state else 1) * POOL_HIST * n_batch * POOL_WIDTH * 4
    vmem_limit = min(VMEM_CAP_BYTES, VMEM_LIMIT_BYTES + hist_window_bytes - hist_window_bytes % (1 << 20))
    outs = pl.pallas_call(
        kern,
        grid=(n_tiles,),
        in_specs=[pl.BlockSpec(memory_space=pl.ANY) if n_tiles > 1
                  else pl.BlockSpec((n_batch, n_steps, D_MODEL), lambda i: (0, i, 0))]
        + [pl.BlockSpec(memory_space=pl.ANY) if name in late_names else _resident(a.shape)
           for name, a in named.items()] + [cast_chunks(w) for w in cast],
        out_specs=[pl.BlockSpec((rows, D_MODEL), lambda i: (i, 0)),
                   pl.BlockSpec((POOL_HIST, n_batch, POOL_WIDTH), lambda i: (0, 0, 0)) if n_tiles > 1
                   else pl.BlockSpec(memory_space=pl.ANY),
                   state_block, state_block] + [cast_chunks(w) for w in cast],
        out_shape=[jax.ShapeDtypeStruct((n_steps_total * n_batch, D_MODEL), _F32),
                   jax.ShapeDtypeStruct((POOL_HIST, n_batch, POOL_WIDTH) if n_tiles > 1 else hist_rows_shape, _F32),
                   jax.ShapeDtypeStruct(state_shape, _F32),
                   jax.ShapeDtypeStruct(state_shape, _F32)] + [jax.ShapeDtypeStruct(w.shape, _BF16) for w in cast],
        scratch_shapes=([pltpu.VMEM((hist_rows, POOL_WIDTH), _F32)] if n_tiles > 1 else [])
        + [pltpu.VMEM((2, n_steps, n_batch, D_MODEL) if n_tiles > 1 else (rows, D_MODEL), _F32),
           pltpu.VMEM((sub_rows, IN_WIDTH), _F32),
           pltpu.VMEM((hist_rows + rows, POOL_WIDTH), _F32),
           pltpu.VMEM((2, sub_rows, HALF_LANES), _F32)]
        + ([pltpu.VMEM((n_batch, N_STATES), _F32)] * 2 if state_batch_minor or state_grouped else [])
        + [pltpu.VMEM(named[name].shape, named[name].dtype) for name in late_names if name != "hist0"]
        + [pltpu.SemaphoreType.DMA((2 if n_tiles > 1 else len(late_names) + 1,))],
        compiler_params=pltpu.CompilerParams(dimension_semantics=("arbitrary",), vmem_limit_bytes=vmem_limit),
        name="mixer",
    )(x, *state, *weights, *cast)
    return (outs[0], outs[1].reshape(POOL_HIST, n_batch, POOL_WIDTH), *outs[2:])


def _ffn_kernel(h_a_ref, h_b_ref, norm_ffn_ref, w_gate_ref, w_up_ref, w_down_ref, norm_final_ref,
                y_a_ref, y_b_ref, gate_ref, up_ref, y_buf, sem, *, n_batch_a, n_sub, n_steps_a):
    j = pl.program_id(0)
    sub_rows = h_a_ref.shape[0] // n_sub
    sub_steps = sub_rows // n_batch_a
    steps = n_sub * sub_steps

    def tile(h):
        hn = _rmsnorm(h, norm_ffn_ref[...]).astype(_BF16)
        gate_ref[...] = _dot(hn, w_gate_ref[...])
        up_ref[...] = _dot(hn, w_up_ref[...])
        gate = gate_ref[...]
        f = (gate * jax.nn.sigmoid(gate) * up_ref[...]).astype(_BF16)
        return _rmsnorm(h + _dot(f, w_down_ref[...]), norm_final_ref[...])

    def copies(slot, step):
        t0 = step * steps if isinstance(step, int) else pl.multiple_of(step * steps, steps)
        return [pltpu.make_async_copy(y_buf.at[slot, :, b, :], y_a_ref.at[b, pl.ds(t0, steps), :], sem.at[slot])
                for b in range(n_batch_a)]

    @pl.when(j < n_steps_a)
    def _():
        slot = lax.rem(j, 2)
        for s in range(n_sub):
            y = tile(h_a_ref[s * sub_rows:(s + 1) * sub_rows, :])
            y_buf[slot, s * sub_steps:(s + 1) * sub_steps] = y.reshape(sub_steps, n_batch_a, D_MODEL)

        @pl.when(j > 0)
        def _():
            for c in copies(1 - slot, j - 1):
                c.wait()
        for c in copies(slot, j):
            c.start()

    @pl.when(j == n_steps_a)
    def _():
        n_batch_b, steps_b, _ = y_b_ref.shape
        y = tile(h_b_ref[...]).reshape(steps_b, n_batch_b, D_MODEL)
        y_b_ref[...] = jnp.swapaxes(y, 0, 1)
        for c in copies((n_steps_a - 1) % 2, n_steps_a - 1):
            c.wait()


def _ffn(h_a, h_b, norms, weights, *, n_batch_a, n_batch_b):
    norm_ffn, norm_final = norms
    w_gate, w_up, w_down = weights
    n_sub = FFN_SUB_TILES
    rows_a = ROW_TILE * n_sub
    n_steps_a = h_a.shape[0] // rows_a
    assert n_steps_a * rows_a == h_a.shape[0] and rows_a % n_batch_a == 0 and h_b.shape[0] == ROW_TILE
    t_a, t_b = h_a.shape[0] // n_batch_a, h_b.shape[0] // n_batch_b
    return pl.pallas_call(
        functools.partial(_ffn_kernel, n_batch_a=n_batch_a, n_sub=n_sub, n_steps_a=n_steps_a),
        grid=(n_steps_a + 1,),
        in_specs=[pl.BlockSpec((rows_a, D_MODEL), lambda i: (jnp.minimum(i, n_steps_a - 1), 0)),
                  _resident(h_b.shape), _resident(norm_ffn.shape),
                  _resident(w_gate.shape), _resident(w_up.shape), _resident(w_down.shape),
                  _resident(norm_final.shape)],
        out_specs=(pl.BlockSpec(memory_space=pl.ANY),
                   pl.BlockSpec((n_batch_b, t_b, D_MODEL), lambda i: (0, 0, 0))),
        out_shape=(jax.ShapeDtypeStruct((n_batch_a, t_a, D_MODEL), _F32),
                   jax.ShapeDtypeStruct((n_batch_b, t_b, D_MODEL), _F32)),
        scratch_shapes=[pltpu.VMEM((ROW_TILE, D_FF), _F32), pltpu.VMEM((ROW_TILE, D_FF), _F32),
                        pltpu.VMEM((2, rows_a // n_batch_a, n_batch_a, D_MODEL), _F32),
                        pltpu.SemaphoreType.DMA((2,))],
        compiler_params=pltpu.CompilerParams(dimension_semantics=("arbitrary",),
                                             vmem_limit_bytes=VMEM_LIMIT_BYTES),
        name="ffn",
    )(h_a, h_b, norm_ffn, w_gate, w_up, w_down, norm_final)


def kernel(x_prompt, x_sample, state_pool, state_ssm_re, state_ssm_im, norm_mix, w_in, pool_w, pool_scale, ssm_a_re, ssm_a_im, ssm_log_dt, ssm_b_re, ssm_b_im, ssm_c_re, ssm_c_im, ssm_d, glu_w, glu_b, w_branch_pool, w_branch_ssm, w_out, norm_ffn, ffn_w_gate, ffn_w_up, ffn_w_down, norm_final):
    assert w_in.shape[0] == 1, "single-layer trunk"
    (ab_re, ab_im, b_mat, c_mat, pool_mat, w_in_bf, glu_w_bf, w_bp_bf, w_bs_bf, w_out_bf) = _prep(
        ssm_a_re[0], ssm_a_im[0], ssm_log_dt, ssm_b_re[0], ssm_b_im[0], ssm_c_re[0], ssm_c_im[0], pool_w[0],
        (w_in[0], glu_w[0], w_branch_pool[0], w_branch_ssm[0], w_out[0]))
    mixer_w = (norm_mix[0].reshape(1, D_MODEL), w_in_bf, pool_mat,
               pool_scale[0].reshape(1, POOL_WIDTH), ab_re, ab_im, b_mat, c_mat, ssm_d[0].reshape(1, SSM_WIDTH),
               glu_w_bf, glu_b[0].reshape(1, SSM_WIDTH), w_bp_bf, w_bs_bf, w_out_bf)
    ffn_norms = (norm_ffn[0].reshape(1, D_MODEL), norm_final.reshape(1, D_MODEL))

    def mix(x, state, start_pos, cast=()):
        b = x.shape[0]
        batch_minor = b % LANES == 0
        if state is not None:
            flat = ((lambda a: a.transpose(1, 2, 0).reshape(N_STATES, b)) if batch_minor
                    else (lambda a: a.reshape(b, N_STATES)))
            state = (state[0].transpose(1, 0, 2), flat(state[1]), flat(state[2]))
        h_rows, new_hist, s_re, s_im, *cast_bf = _mixer(x, state, mixer_w, start_pos=start_pos,
                                                       state_batch_minor=batch_minor, cast=cast)
        unflat = ((lambda a: a.reshape(SSM_GROUPS, SSM_STATE, b).transpose(2, 0, 1)[None]) if batch_minor
                  else (lambda a: a.reshape(1, b, SSM_GROUPS, SSM_STATE)))
        return h_rows, new_hist.transpose(1, 0, 2)[None], unflat(s_re), unflat(s_im), cast_bf

    h_p, pool_p, re_p, im_p, ffn_w = mix(x_prompt, None, 0, cast=(ffn_w_gate[0], ffn_w_up[0], ffn_w_down[0]))
    h_s, pool_s, re_s, im_s, _ = mix(x_sample, (state_pool[0], state_ssm_re[0], state_ssm_im[0]), PAST_LEN)
    y_p, y_s = _ffn(h_p, h_s, ffn_norms, ffn_w,
                    n_batch_a=x_prompt.shape[0], n_batch_b=x_sample.shape[0])
    return (y_p, y_s, pool_p, re_p, im_p, pool_s, re_s, im_s)
```
